```python
import jax, jax.numpy as jnp
from jax import lax
import numpy as np


D_MODEL = 1024
BATCH = 2
SEQ = 8192
DEPTH = 1

GRID_W = 64
CTX_LEN = 256

D_MIX = D_MODEL
M_HEADS = 4
M_DH = 128
M_WIDTH = M_HEADS * M_DH
A_HEADS = 8
A_KV_HEADS = 2
A_DH = 64
A_GROUP = A_HEADS // A_KV_HEADS
A_WIDTH = A_HEADS * A_DH
A_KV_WIDTH = A_KV_HEADS * A_DH
WINDOW = 128
A_BLOCK = 128
CHUNK = 64
CONV_W = 5
D_FF = 2816
ROPE_BASE = 10000.0
N_MOD = 9
LN_EPS = 1e-5
FFN_RES = 0.5
ALPHA = (2.0 * DEPTH) ** 0.25
BETA = (8.0 * DEPTH) ** -0.25

OFF_QM = 0
OFF_KM = OFF_QM + M_WIDTH
OFF_VM = OFF_KM + M_WIDTH
OFF_OM = OFF_VM + M_WIDTH
OFF_G = OFF_OM + M_WIDTH
N_GATES = 4 * M_HEADS
OFF_QA = OFF_G + N_GATES
OFF_KA = OFF_QA + A_WIDTH
OFF_VA = OFF_KA + A_KV_WIDTH
N_IN = OFF_VA + A_KV_WIDTH

kernel_name = 'hybrid_mlstm_swa_macaron_dit_block'

F32 = jnp.float32


def _layer_norm(x, g, b):
    xf = x.astype(F32)
    mu = xf.mean(-1, keepdims=True)
    var = jnp.square(xf - mu).mean(-1, keepdims=True)
    return ((xf - mu) * lax.rsqrt(var + LN_EPS)).astype(x.dtype) * g + b


def _modulate(x, shift, scale):
    return x * (1.0 + scale) + shift


def _swiglu(h, w13, w2):
    a = h @ w13
    return (jax.nn.silu(a[..., :D_FF]) * a[..., D_FF:]) @ w2


def _ffn_sublayer(x, shift, scale, gate, w13, w2, g, b):
    y = _swiglu(_modulate(x, shift, scale), w13, w2)
    return _layer_norm(ALPHA * x + FFN_RES * gate * y, g, b)


def _centred_dwconv(x, w, b):
    T = x.shape[1]
    pad = CONV_W // 2
    xp = jnp.pad(x, ((0, 0), (pad, pad), (0, 0)))
    out = xp[:, 0:T] * w[0]
    for j in range(1, CONV_W):
        out = out + xp[:, j:j + T] * w[j]
    return out + b


def _axial_rope_tables(rows):
    row = jnp.repeat(jnp.arange(rows, dtype=F32), GRID_W)
    col = jnp.tile(jnp.arange(GRID_W, dtype=F32), rows)
    n_freq = A_DH // 4
    inv = ROPE_BASE ** (-jnp.arange(n_freq, dtype=F32) / n_freq)
    ang = jnp.concatenate([row[:, None] * inv, col[:, None] * inv], -1)
    return jnp.cos(ang), jnp.sin(ang)


def _rope(x, cos, sin):
    xf = x.astype(F32)
    half = A_DH // 2
    x1, x2 = xf[..., :half], xf[..., half:]
    cs = cos[None, :, None, :]
    sn = sin[None, :, None, :]
    return jnp.concatenate([x1 * cs - x2 * sn, x2 * cs + x1 * sn], -1).astype(x.dtype)


def _rev(a):
    return jnp.flip(a, axis=1)


def _mlstm_scan(q, k, v, li, lf, state, return_h):
    B, T, H, d = q.shape
    nc = T // CHUNK

    def to_chunks(a):
        a = a.reshape((B, nc, CHUNK) + a.shape[2:])
        return jnp.swapaxes(jnp.moveaxis(a, 1, 0), 2, 3)

    tril = jnp.tril(jnp.ones((CHUNK, CHUNK), dtype=bool))

    def body(carry, inp):
        C, nv, m = carry
        qc, kc, vc, lic, lfc = inp
        b = jnp.cumsum(lfc, axis=-1)
        b_last = b[..., -1]
        log_w = b_last[..., None] - b + lic
        m_new = jnp.maximum(b_last + m, log_w.max(-1))
        decay = jnp.exp(b_last + m - m_new)
        w = jnp.exp(log_w - m_new[..., None])
        C_new = decay[..., None, None] * C + jnp.einsum('bhs,bhsv,bhsk->bhvk', w, vc, kc)
        n_new = decay[..., None] * nv + jnp.einsum('bhs,bhsk->bhk', w, kc)
        if not return_h:
            return (C_new, n_new, m_new), None
        log_d = b[..., :, None] - b[..., None, :] + lic[..., None, :]
        log_d = jnp.where(tril, log_d, -jnp.inf)
        log_inter = b + m[..., None]
        m_t = jnp.maximum(log_inter, log_d.max(-1))
        dmat = jnp.exp(log_d - m_t[..., None])
        inter = jnp.exp(log_inter - m_t)
        s = jnp.einsum('bhtk,bhsk->bhts', qc, kc) * dmat
        num = jnp.einsum('bhts,bhsv->bhtv', s, vc) + inter[..., None] * jnp.einsum('bhtk,bhvk->bhtv', qc, C)
        den = s.sum(-1) + inter * jnp.einsum('bhtk,bhk->bht', qc, nv)
        h = num / jnp.maximum(jnp.abs(den), jnp.exp(-m_t))[..., None]
        return (C_new, n_new, m_new), h

    xs = (to_chunks(q), to_chunks(k), to_chunks(v), to_chunks(li), to_chunks(lf))
    state_out, hs = lax.scan(body, state, xs)
    if not return_h:
        return state_out, None
    h = jnp.moveaxis(jnp.swapaxes(hs, 2, 3), 0, 1).reshape(B, T, H, d)
    return state_out, h


def _bidir_mlstm(q, k, v, fwd, bwd, init_f, init_b, return_h):
    st_f, h_f = _mlstm_scan(q, k, v, fwd[0], fwd[1], init_f, return_h)
    st_b, h_b = _mlstm_scan(_rev(q), _rev(k), _rev(v), _rev(bwd[0]), _rev(bwd[1]), init_b, return_h)
    h = h_f + _rev(h_b) if return_h else None
    return h, st_f, st_b


def _mlstm_inputs(p, conv_w, conv_b, b_gates):
    B, T = p.shape[:2]
    qk = jax.nn.silu(_centred_dwconv(p[..., OFF_QM:OFF_VM], conv_w, conv_b))
    qk = qk.astype(F32).reshape(B, T, 2, M_HEADS, M_DH)
    q = qk[:, :, 0] * (M_DH ** -0.5)
    k = qk[:, :, 1]
    v = p[..., OFF_VM:OFF_OM].astype(F32).reshape(B, T, M_HEADS, M_DH)
    g = (p[..., OFF_G:OFF_QA] + b_gates).astype(F32).reshape(B, T, 4, M_HEADS)
    fwd = (g[:, :, 0], jax.nn.log_sigmoid(g[:, :, 1]))
    bwd = (g[:, :, 2], jax.nn.log_sigmoid(g[:, :, 3]))
    return q, k, v, fwd, bwd


def _mlstm_readout(h, o_pre, g):
    B, T = h.shape[:2]
    mu = h.mean(-1, keepdims=True)
    var = jnp.square(h - mu).mean(-1, keepdims=True)
    hn = ((h - mu) * lax.rsqrt(var + LN_EPS)).reshape(B, T, M_WIDTH)
    return (hn * g * jax.nn.sigmoid(o_pre.astype(F32))).astype(o_pre.dtype)


def _banded_attention(q, k, v, kc, vc, sink):
    B, n = q.shape[:2]
    nb = n // A_BLOCK
    scale = A_DH ** -0.5
    qb = q.reshape(B, nb, A_BLOCK, A_KV_HEADS, A_GROUP, A_DH)

    def band(a):
        ap = jnp.pad(a, ((0, 0), (A_BLOCK, A_BLOCK), (0, 0), (0, 0)))
        ap = ap.reshape(B, nb + 2, A_BLOCK, A_KV_HEADS, A_DH)
        return jnp.concatenate([ap[:, :-2], ap[:, 1:-1], ap[:, 2:]], axis=2)

    kb, vb = band(k), band(v)
    s_loc = jnp.einsum('bjikgd,bjrkd->bjkgir', qb, kb).astype(F32) * scale
    s_ctx = jnp.einsum('bjikgd,bckd->bjkgic', qb, kc).astype(F32) * scale
    i = jnp.arange(A_BLOCK)
    r = jnp.arange(3 * A_BLOCK)
    j = jnp.arange(nb)
    rel = r[None, :] - A_BLOCK - i[:, None]
    key_pos = (j[:, None] - 1) * A_BLOCK + r[None, :]
    valid = (jnp.abs(rel) <= WINDOW)[None] & ((key_pos >= 0) & (key_pos < n))[:, None, :]
    s_loc = jnp.where(valid[None, :, None, None], s_loc, -jnp.inf)
    sink_f = sink.astype(F32).reshape(1, 1, A_KV_HEADS, A_GROUP, 1)
    m = jnp.maximum(jnp.maximum(s_loc.max(-1), s_ctx.max(-1)), sink_f)
    e_loc = jnp.exp(s_loc - m[..., None])
    e_ctx = jnp.exp(s_ctx - m[..., None])
    den = e_loc.sum(-1) + e_ctx.sum(-1) + jnp.exp(sink_f - m)
    out = (jnp.einsum('bjkgir,bjrkd->bjkgid', e_loc, vb.astype(F32))
           + jnp.einsum('bjkgic,bckd->bjkgid', e_ctx, vc.astype(F32))) / den[..., None]
    return out.transpose(0, 1, 4, 2, 3, 5).reshape(B, n, A_WIDTH).astype(q.dtype)


def _context_attention(q, k, v, sink):
    B, Tc = q.shape[:2]
    qg = q.reshape(B, Tc, A_KV_HEADS, A_GROUP, A_DH)
    s = jnp.einsum('bikgd,bckd->bkgic', qg, k).astype(F32) * (A_DH ** -0.5)
    sink_b = jnp.broadcast_to(sink.astype(F32).reshape(1, A_KV_HEADS, A_GROUP, 1, 1), s.shape[:-1] + (1,))
    p = jax.nn.softmax(jnp.concatenate([s, sink_b], -1), axis=-1)[..., :-1]
    out = jnp.einsum('bkgic,bckd->bikgd', p.astype(v.dtype), v)
    return out.reshape(B, Tc, A_WIDTH)


def _token_mixer(h_lat, h_ctx, w_in, b_gates, conv_w, conv_b, mh_g, sink, w_out, cos, sin, need_ctx):
    B, n = h_lat.shape[:2]
    Tc = h_ctx.shape[1]
    p_l = h_lat @ w_in
    p_c = h_ctx @ w_in
    qc, kc, vc, fwd_c, bwd_c = _mlstm_inputs(p_c, conv_w, conv_b, b_gates)
    zero = (jnp.zeros((B, M_HEADS, M_DH, M_DH), F32), jnp.zeros((B, M_HEADS, M_DH), F32),
            jnp.zeros((B, M_HEADS), F32))
    hm_c, st_f, st_b = _bidir_mlstm(qc, kc, vc, fwd_c, bwd_c, zero, zero, need_ctx)
    ql, kl, vl, fwd_l, bwd_l = _mlstm_inputs(p_l, conv_w, conv_b, b_gates)
    hm_l, _, _ = _bidir_mlstm(ql, kl, vl, fwd_l, bwd_l, st_f, st_b, True)
    m_l = _mlstm_readout(hm_l, p_l[..., OFF_OM:OFF_G], mh_g)
    qa_l = _rope(p_l[..., OFF_QA:OFF_KA].reshape(B, n, A_HEADS, A_DH), cos, sin)
    ka_l = _rope(p_l[..., OFF_KA:OFF_VA].reshape(B, n, A_KV_HEADS, A_DH), cos, sin)
    va_l = p_l[..., OFF_VA:N_IN].reshape(B, n, A_KV_HEADS, A_DH)
    ka_c = p_c[..., OFF_KA:OFF_VA].reshape(B, Tc, A_KV_HEADS, A_DH)
    va_c = p_c[..., OFF_VA:N_IN].reshape(B, Tc, A_KV_HEADS, A_DH)
    a_l = _banded_attention(qa_l, ka_l, va_l, ka_c, va_c, sink)
    y_l = jnp.concatenate([m_l, a_l], -1) @ w_out
    if not need_ctx:
        return y_l, None
    m_c = _mlstm_readout(hm_c, p_c[..., OFF_OM:OFF_G], mh_g)
    qa_c = p_c[..., OFF_QA:OFF_KA].reshape(B, Tc, A_HEADS, A_DH)
    a_c = _context_attention(qa_c, ka_c, va_c, sink)
    y_c = jnp.concatenate([m_c, a_c], -1) @ w_out
    return y_l, y_c


def setup_inputs(seed: int = 0) -> dict:
    key = jax.random.key(seed)
    ks = jax.random.split(key, 24)

    def nrm(k, shape, scale):
        return jax.random.normal(k, shape, F32) * scale

    x = nrm(ks[0], (BATCH, SEQ, D_MODEL), 1.0)
    c = nrm(ks[1], (BATCH, D_MODEL), 1.0)
    ctx = nrm(ks[2], (BATCH, CTX_LEN, D_MODEL), 1.0)
    c_ctx = nrm(ks[3], (D_MODEL,), 1.0)
    w_ada = nrm(ks[4], (DEPTH, D_MODEL, N_MOD * D_MODEL), 0.5 * D_MODEL ** -0.5)
    b_ada = nrm(ks[5], (DEPTH, N_MOD * D_MODEL), 0.02)
    ln_g = 1.0 + nrm(ks[6], (DEPTH, 3, D_MODEL), 0.02)
    ln_b = nrm(ks[7], (DEPTH, 3, D_MODEL), 0.02)
    ffn_w13 = nrm(ks[8], (DEPTH, 2, D_MODEL, 2 * D_FF), D_MODEL ** -0.5)
    ffn_w2 = nrm(ks[9], (DEPTH, 2, D_FF, D_MODEL), BETA * D_FF ** -0.5)
    w_in = nrm(ks[10], (DEPTH, D_MODEL, N_IN), D_MODEL ** -0.5)
    f_bias = jnp.linspace(3.0, 6.0, M_HEADS, dtype=F32)[None, :]
    bi_f = nrm(ks[11], (DEPTH, M_HEADS), 0.1)
    bf_f = f_bias + nrm(ks[12], (DEPTH, M_HEADS), 0.1)
    bi_b = nrm(ks[13], (DEPTH, M_HEADS), 0.1)
    bf_b = f_bias + nrm(ks[14], (DEPTH, M_HEADS), 0.1)
    b_gates = jnp.concatenate([bi_f, bf_f, bi_b, bf_b], -1)
    conv_w = nrm(ks[15], (DEPTH, CONV_W, 2 * M_WIDTH), CONV_W ** -0.5)
    conv_b = nrm(ks[16], (DEPTH, 2 * M_WIDTH), 0.02)
    mh_norm_g = 1.0 + nrm(ks[17], (DEPTH, M_WIDTH), 0.02)
    attn_sink = nrm(ks[18], (DEPTH, A_HEADS), 0.5)
    w_out = nrm(ks[19], (DEPTH, D_MIX, D_MODEL), BETA * D_MIX ** -0.5)
    return {'x': x, 'c': c, 'ctx': ctx, 'c_ctx': c_ctx, 'w_ada': w_ada, 'b_ada': b_ada,
            'ln_g': ln_g, 'ln_b': ln_b, 'ffn_w13': ffn_w13, 'ffn_w2': ffn_w2, 'w_in': w_in,
            'b_gates': b_gates, 'conv_w': conv_w, 'conv_b': conv_b, 'mh_norm_g': mh_norm_g,
            'attn_sink': attn_sink, 'w_out': w_out}


def reference(x, c, ctx, c_ctx, w_ada, b_ada, ln_g, ln_b, ffn_w13, ffn_w2, w_in, b_gates,
              conv_w, conv_b, mh_norm_g, attn_sink, w_out):
    B, n, D = x.shape
    rows = n // GRID_W
    cos, sin = _axial_rope_tables(rows)
    x_lat, x_ctx = x, ctx
    for l in range(DEPTH):
        need_ctx = l < DEPTH - 1
        mod_l = (jax.nn.silu(c) @ w_ada[l] + b_ada[l]).reshape(B, N_MOD, 1, D)
        mod_c = (jax.nn.silu(c_ctx) @ w_ada[l] + b_ada[l]).reshape(N_MOD, 1, 1, D)
        ml = [mod_l[:, i] for i in range(N_MOD)]
        mc = [mod_c[i] for i in range(N_MOD)]
        x_lat = _ffn_sublayer(x_lat, ml[0], ml[1], ml[2], ffn_w13[l, 0], ffn_w2[l, 0], ln_g[l, 0], ln_b[l, 0])
        x_ctx = _ffn_sublayer(x_ctx, mc[0], mc[1], mc[2], ffn_w13[l, 0], ffn_w2[l, 0], ln_g[l, 0], ln_b[l, 0])
        y_lat, y_ctx = _token_mixer(_modulate(x_lat, ml[3], ml[4]), _modulate(x_ctx, mc[3], mc[4]),
                                    w_in[l], b_gates[l], conv_w[l], conv_b[l], mh_norm_g[l],
                                    attn_sink[l], w_out[l], cos, sin, need_ctx)
        x_lat = _layer_norm(ALPHA * x_lat + ml[5] * y_lat, ln_g[l, 1], ln_b[l, 1])
        x_lat = _ffn_sublayer(x_lat, ml[6], ml[7], ml[8], ffn_w13[l, 1], ffn_w2[l, 1], ln_g[l, 2], ln_b[l, 2])
        if need_ctx:
            x_ctx = _layer_norm(ALPHA * x_ctx + mc[5] * y_ctx, ln_g[l, 1], ln_b[l, 1])
            x_ctx = _ffn_sublayer(x_ctx, mc[6], mc[7], mc[8], ffn_w13[l, 1], ffn_w2[l, 1], ln_g[l, 2], ln_b[l, 2])
    return x_lat
```

```python
import functools

import jax
import jax.numpy as jnp
from jax import lax
from jax.experimental import pallas as pl
from jax.experimental.pallas import tpu as pltpu

F32 = jnp.float32
BF16 = jnp.bfloat16

D_MODEL = 1024
GRID_W = 64
M_HEADS = 4
M_DH = 128
M_WIDTH = M_HEADS * M_DH
A_HEADS = 8
A_KV_HEADS = 2
A_DH = 64
A_WIDTH = A_HEADS * A_DH
A_KV_WIDTH = A_KV_HEADS * A_DH
A_BLOCK = 128
CONV_W = 5
D_FF = 2816
ROPE_BASE = 10000.0
N_MOD = 9
LN_EPS = 1e-5
FFN_RES = 0.5
DEPTH = 1
ALPHA = (2.0 * DEPTH) ** 0.25

OFF_QM = 0
OFF_KM = OFF_QM + M_WIDTH
OFF_VM = OFF_KM + M_WIDTH
OFF_OM = OFF_VM + M_WIDTH
OFF_G = OFF_OM + M_WIDTH
N_GATES = 4 * M_HEADS
OFF_QA = OFF_G + N_GATES
OFF_KA = OFF_QA + A_WIDTH
OFF_VA = OFF_KA + A_KV_WIDTH
N_IN = OFF_VA + A_KV_WIDTH

LANES = 128
SUBLANES = 8
HALO = SUBLANES
MLSTM_CHUNK = LANES
FF_CHUNK = 256
GATE_GROUP = SUBLANES
VMEM_LIMIT = 56 * 1024 * 1024


def _sigmoid(x):
    return 1.0 / (1.0 + jnp.exp(-x))


def _log_sigmoid(x):
    return jnp.minimum(x, 0.0) - jnp.log1p(jnp.exp(-jnp.abs(x)))


def _layer_norm(z, g, b):
    mu = jnp.mean(z, axis=-1, keepdims=True)
    zc = z - mu
    var = jnp.mean(zc * zc, axis=-1, keepdims=True)
    return zc * lax.rsqrt(var + LN_EPS) * g + b


def _params(n_grid):
    return pltpu.CompilerParams(dimension_semantics=("arbitrary",) * n_grid,
                                vmem_limit_bytes=VMEM_LIMIT)


def _resident(shape):
    nd = len(shape)
    return pl.BlockSpec(shape, lambda *_: (0,) * nd, pipeline_mode=pl.Buffered(1))


def _mod_kernel(c_ref, w_ref, b_ref, o_ref):
    c = c_ref[...]
    s = (c * _sigmoid(c)).astype(BF16)
    o_ref[...] = jnp.dot(s, w_ref[...].astype(BF16), preferred_element_type=F32) + b_ref[...]


def _mod_call(cc, w_ada, b_ada):
    n = w_ada.shape[1]
    tn = D_MODEL
    return pl.pallas_call(
        _mod_kernel,
        grid=(n // tn,),
        in_specs=[pl.BlockSpec((SUBLANES, D_MODEL), lambda j: (0, 0)),
                  pl.BlockSpec((D_MODEL, tn), lambda j: (0, j)),
                  pl.BlockSpec((1, tn), lambda j: (0, j))],
        out_specs=pl.BlockSpec((SUBLANES, tn), lambda j: (0, j)),
        out_shape=jax.ShapeDtypeStruct((SUBLANES, n), F32),
        compiler_params=_params(1),
        name="mod",
    )(cc, w_ada, b_ada.reshape(1, n))


def _ffn_kernel(x_ref, mod_ref, w13_ref, w2_ref, g_ref, b_ref, o_ref, h_ref, acc_ref, *, k0):
    x = x_ref[...]
    shift = mod_ref[0, k0:k0 + 1, :]
    scale = mod_ref[0, k0 + 1:k0 + 2, :]
    gate = mod_ref[0, k0 + 2:k0 + 3, :]
    h_ref[...] = (x * (1.0 + scale) + shift).astype(BF16)
    for j in range(D_FF // FF_CHUNK):
        lo = j * FF_CHUNK
        h = h_ref[...]
        a1 = jnp.dot(h, w13_ref[:, lo:lo + FF_CHUNK], preferred_element_type=F32)
        a3 = jnp.dot(h, w13_ref[:, D_FF + lo:D_FF + lo + FF_CHUNK], preferred_element_type=F32)
        g = (a1 * _sigmoid(a1) * a3).astype(BF16)
        y = jnp.dot(g, w2_ref[lo:lo + FF_CHUNK, :], preferred_element_type=F32)
        if j == 0:
            acc_ref[...] = y
        else:
            acc_ref[...] += y
    z = ALPHA * x + (FFN_RES * gate) * acc_ref[...]
    o_ref[...] = _layer_norm(z, g_ref[...], b_ref[...])


def _ffn_call(x, mod, mod_index, k0, w13, w2, ln_g, ln_b, tm):
    r = x.shape[0]
    return pl.pallas_call(
        functools.partial(_ffn_kernel, k0=k0),
        grid=(r // tm,),
        in_specs=[pl.BlockSpec((tm, D_MODEL), lambda i: (i, 0)),
                  pl.BlockSpec((1, N_MOD, D_MODEL), lambda i: (mod_index(i), 0, 0)),
                  _resident((D_MODEL, 2 * D_FF)),
                  _resident((D_FF, D_MODEL)),
                  _resident((1, D_MODEL)),
                  _resident((1, D_MODEL))],
        out_specs=pl.BlockSpec((tm, D_MODEL), lambda i: (i, 0)),
        out_shape=jax.ShapeDtypeStruct((r, D_MODEL), F32),
        scratch_shapes=[pltpu.VMEM((tm, D_MODEL), BF16), pltpu.VMEM((tm, D_MODEL), F32)],
        compiler_params=_params(1),
        name="ffn",
    )(x, mod, w13, w2, ln_g.reshape(1, D_MODEL), ln_b.reshape(1, D_MODEL))


N_WA = 2 * M_WIDTH + 2 * M_WIDTH
N_WT = A_WIDTH + A_KV_WIDTH + 2 * A_KV_WIDTH
GATE_ROWS = 4 * GATE_GROUP


def _prefix_scan(x, lane):
    k = 1
    while k < LANES:
        x = x + jnp.where(lane >= k, pltpu.roll(x, k, 1), 0.0)
        k *= 2
    return x


def _suffix_scan(x, lane):
    k = 1
    while k < LANES:
        x = x + jnp.where(lane < LANES - k, pltpu.roll(x, LANES - k, 1), 0.0)
        k *= 2
    return x


def _rope(x, cos, sin_signed, first_half):
    swapped = jnp.where(first_half, pltpu.roll(x, LANES - A_DH // 2, 1), pltpu.roll(x, A_DH // 2, 1))
    return x * cos + swapped * sin_signed


def _inproj_kernel(*refs, tiles_per_seq, rope, tm):
    if rope:
        (xp_ref, x_ref, xn_ref, mod_ref, wa_ref, wg_ref, bg_ref, wt_ref, cw_ref, cb_ref, cos_ref, sin_ref,
         q_ref, kt_ref, v_ref, o_ref, grow_ref, gcol_ref, qa_ref, kat_ref, va_ref, xs_ref) = refs
    else:
        (xp_ref, x_ref, xn_ref, mod_ref, wa_ref, wg_ref, bg_ref, wt_ref, cw_ref, cb_ref,
         q_ref, kt_ref, v_ref, o_ref, grow_ref, gcol_ref, qa_ref, kat_ref, va_ref, xs_ref) = refs
    i = pl.program_id(0)
    pos = i % tiles_per_seq
    shift = mod_ref[0, 3:4, :]
    scale1 = 1.0 + mod_ref[0, 4:5, :]
    h2 = (x_ref[...] * scale1 + shift).astype(BF16)
    xh = jnp.concatenate([xp_ref[...], xn_ref[...]], axis=0)
    hh = (xh * scale1 + shift).astype(BF16)

    pa = jnp.dot(h2, wa_ref[...], preferred_element_type=F32)
    ph = jnp.dot(hh, wa_ref[:, :2 * M_WIDTH], preferred_element_type=F32)
    xs_ref[0:HALO, :] = jnp.where(pos == 0, 0.0, ph[0:HALO])
    xs_ref[HALO:HALO + tm, :] = pa[:, :2 * M_WIDTH]
    xs_ref[HALO + tm:2 * HALO + tm, :] = jnp.where(pos == tiles_per_seq - 1, 0.0, ph[HALO:2 * HALO])
    acc = cb_ref[...] + cw_ref[0:1, :] * xs_ref[pl.ds(HALO - CONV_W // 2, tm), :]
    for j in range(1, CONV_W):
        acc = acc + cw_ref[j:j + 1, :] * xs_ref[pl.ds(HALO - CONV_W // 2 + j, tm), :]
    qk = acc * _sigmoid(acc)
    q_ref[...] = (qk[:, :M_WIDTH] * (M_DH ** -0.5)).astype(BF16)
    kt_ref[...] = qk[:, M_WIDTH:].T.astype(BF16)
    v_ref[...] = pa[:, 2 * M_WIDTH:3 * M_WIDTH].astype(BF16)
    o_ref[...] = pa[:, 3 * M_WIDTH:]

    pg = jnp.dot(h2, wg_ref[...], preferred_element_type=F32) + bg_ref[...]
    pgt = pg.T
    li_f = pgt[0:GATE_GROUP]
    lf_f = _log_sigmoid(pgt[GATE_GROUP:2 * GATE_GROUP])
    li_b = pgt[2 * GATE_GROUP:3 * GATE_GROUP]
    lf_b = _log_sigmoid(pgt[3 * GATE_GROUP:4 * GATE_GROUP])
    lane = lax.broadcasted_iota(jnp.int32, (GATE_GROUP, LANES), 1)
    b_f = jnp.concatenate([_prefix_scan(lf_f[:, c * LANES:(c + 1) * LANES], lane)
                           for c in range(tm // LANES)], axis=1)
    e_b = jnp.concatenate([_suffix_scan(lf_b[:, c * LANES:(c + 1) * LANES], lane)
                           for c in range(tm // LANES)], axis=1)
    grow = jnp.concatenate([li_f - b_f, b_f, li_b - e_b, e_b], axis=0)
    grow_ref[...] = grow
    gcol_ref[...] = jnp.concatenate([grow, jnp.zeros((LANES - GATE_ROWS, tm), F32)], axis=0).T

    pt = jnp.dot(h2, wt_ref[...], preferred_element_type=F32)
    ka = pt[:, A_WIDTH:A_WIDTH + A_KV_WIDTH]
    if rope:
        cos = cos_ref[...]
        sin = sin_ref[...]
        lane_t = lax.broadcasted_iota(jnp.int32, (tm, LANES), 1)
        first_half = (lane_t % A_DH) < (A_DH // 2)
        for g in range(A_WIDTH // LANES):
            xg = _rope(pt[:, g * LANES:(g + 1) * LANES], cos, sin, first_half)
            qa_ref[:, g * LANES:(g + 1) * LANES] = (xg * (A_DH ** -0.5)).astype(BF16)
        ka = _rope(ka, cos, sin, first_half)
    else:
        qa_ref[...] = (pt[:, :A_WIDTH] * (A_DH ** -0.5)).astype(BF16)
    kat_ref[...] = ka.T.astype(BF16)
    va_ref[...] = pt[:, A_WIDTH + A_KV_WIDTH:].astype(BF16)


def _inproj_call(x, mod, mod_index, wa, wg, bg, wt, conv_w, conv_b, cos, sin, tm, seq_len):
    r = x.shape[0]
    tiles_per_seq = seq_len // tm
    rope = cos is not None
    hb = tm // HALO
    n_halo = r // HALO
    in_specs = [pl.BlockSpec((HALO, D_MODEL), lambda i: (jnp.maximum(i * hb - 1, 0), 0)),
                pl.BlockSpec((tm, D_MODEL), lambda i: (i, 0)),
                pl.BlockSpec((HALO, D_MODEL), lambda i: (jnp.minimum((i + 1) * hb, n_halo - 1), 0)),
                pl.BlockSpec((1, N_MOD, D_MODEL), lambda i: (mod_index(i), 0, 0)),
                _resident((D_MODEL, N_WA)),
                _resident((D_MODEL, LANES)),
                _resident((1, LANES)),
                _resident((D_MODEL, N_WT)),
                _resident((CONV_W, 2 * M_WIDTH)),
                _resident((1, 2 * M_WIDTH))]
    args = [x, x, x, mod, wa, wg, bg, wt, conv_w, conv_b]
    if rope:
        in_specs += [pl.BlockSpec((tm, LANES), lambda i: (i % tiles_per_seq, 0)),
                     pl.BlockSpec((tm, LANES), lambda i: (i % tiles_per_seq, 0))]
        args += [cos, sin]
    row = lambda w: pl.BlockSpec((tm, w), lambda i: (i, 0))
    col = lambda h: pl.BlockSpec((h, tm), lambda i: (0, i))
    out_specs = [row(M_WIDTH), col(M_WIDTH), row(M_WIDTH), row(M_WIDTH), col(GATE_ROWS), row(LANES),
                 row(A_WIDTH), col(A_KV_WIDTH), row(2 * A_KV_WIDTH)]
    out_shape = [jax.ShapeDtypeStruct((r, M_WIDTH), BF16),
                 jax.ShapeDtypeStruct((M_WIDTH, r), BF16),
                 jax.ShapeDtypeStruct((r, M_WIDTH), BF16),
                 jax.ShapeDtypeStruct((r, M_WIDTH), F32),
                 jax.ShapeDtypeStruct((GATE_ROWS, r), F32),
                 jax.ShapeDtypeStruct((r, LANES), F32),
                 jax.ShapeDtypeStruct((r, A_WIDTH), BF16),
                 jax.ShapeDtypeStruct((A_KV_WIDTH, r), BF16),
                 jax.ShapeDtypeStruct((r, 2 * A_KV_WIDTH), BF16)]
    return pl.pallas_call(
        functools.partial(_inproj_kernel, tiles_per_seq=tiles_per_seq, rope=rope, tm=tm),
        grid=(r // tm,),
        in_specs=in_specs,
        out_specs=out_specs,
        out_shape=out_shape,
        scratch_shapes=[pltpu.VMEM((tm + 2 * HALO, 2 * M_WIDTH), F32)],
        compiler_params=_params(1),
        name="inproj",
    )(*args)


M_AUG = 2 * M_DH
ROW_A = (0, 2 * GATE_GROUP)
ROW_B = (GATE_GROUP, 3 * GATE_GROUP)


def _mlstm_kernel(*refs, emit_h):
    if emit_h:
        (qf_ref, ktf_ref, vf_ref, grf_ref, gcf_ref, qb_ref, ktb_ref, vb_ref, grb_ref, gcb_ref, c0_ref, m0_ref,
         hf_ref, hb_ref, cout_ref, mout_ref, c_ref, m_ref) = refs
        q_refs, gc_refs, h_refs = (qf_ref, qb_ref), (gcf_ref, gcb_ref), (hf_ref, hb_ref)
    else:
        (ktf_ref, vf_ref, grf_ref, ktb_ref, vb_ref, grb_ref, c0_ref, m0_ref,
         cout_ref, mout_ref, c_ref, m_ref) = refs
    kt_refs, v_refs, gr_refs = (ktf_ref, ktb_ref), (vf_ref, vb_ref), (grf_ref, grb_ref)
    L = MLSTM_CHUNK
    c = pl.program_id(1)

    @pl.when(c == 0)
    def _():
        c_ref[...] = c0_ref[0]
        m_ref[...] = m0_ref[0]

    t_idx = lax.broadcasted_iota(jnp.int32, (L, L), 0)
    s_idx = lax.broadcasted_iota(jnp.int32, (L, L), 1)
    masks = (s_idx <= t_idx, s_idx >= t_idx)
    ones = jnp.ones((L, M_DH), BF16)
    for d in range(2):
        last = (L - 1, 0)[d]
        for h in range(M_HEADS):
            r = d * M_HEADS + h
            sl = slice(h * M_DH, (h + 1) * M_DH)
            a_row = gr_refs[d][ROW_A[d] + h:ROW_A[d] + h + 1, :]
            b_tot = gr_refs[d][ROW_B[d] + h:ROW_B[d] + h + 1, last:last + 1]
            m_st = m_ref[r:r + 1, 0:1]
            ct = c_ref[r]
            kt = kt_refs[d][sl, :]
            v_aug = jnp.concatenate([v_refs[d][:, sl], ones], axis=1)
            if emit_h:
                q = q_refs[d][:, sl]
                b_col = gc_refs[d][:, ROW_B[d] + h:ROW_B[d] + h + 1]
                am = jnp.where(masks[d], a_row, -jnp.inf)
                mu = jnp.maximum(m_st, jnp.max(am, axis=1, keepdims=True))
                dmat = jnp.exp(am - mu)
                s = jnp.dot(q, kt, preferred_element_type=F32)
                p = (s * dmat).astype(BF16)
                tot = (jnp.dot(p, v_aug, preferred_element_type=F32)
                       + jnp.exp(m_st - mu) * jnp.dot(q, ct.astype(BF16), preferred_element_type=F32))
                clamp = jnp.exp(-(b_col + mu))
                h_refs[d][:, sl] = tot[:, :M_DH] / jnp.maximum(jnp.abs(tot[:, M_DH:]), clamp)
            mp = jnp.maximum(m_st, jnp.max(a_row, axis=1, keepdims=True))
            ktw = (kt.astype(F32) * jnp.exp(a_row - mp)).astype(BF16)
            c_ref[r] = jnp.exp(m_st - mp) * ct + jnp.dot(ktw, v_aug, preferred_element_type=F32)
            m_ref[r:r + 1, :] = jnp.broadcast_to(b_tot + mp, (1, LANES))

    @pl.when(c == pl.num_programs(1) - 1)
    def _():
        cout_ref[0] = c_ref[...]
        mout_ref[0] = m_ref[...]


def _mlstm_call(q, kt, v, grow, gcol, c0, m0, n_batch, seq_len, emit_h):
    L = MLSTM_CHUNK
    nc = seq_len // L
    r = n_batch * seq_len
    n_str = 2 * M_HEADS
    fwd = lambda b, c: b * nc + c
    bwd = lambda b, c: b * nc + nc - 1 - c
    row = lambda w, f: pl.BlockSpec((L, w), lambda b, c: (f(b, c), 0))
    col = lambda hgt, f: pl.BlockSpec((hgt, L), lambda b, c: (0, f(b, c)))
    state_specs = [pl.BlockSpec((1, n_str, M_DH, M_AUG), lambda b, c: (b, 0, 0, 0)),
                   pl.BlockSpec((1, n_str, LANES), lambda b, c: (b, 0, 0))]
    state_shapes = [jax.ShapeDtypeStruct((n_batch, n_str, M_DH, M_AUG), F32),
                    jax.ShapeDtypeStruct((n_batch, n_str, LANES), F32)]
    if emit_h:
        in_specs = [row(M_WIDTH, fwd), col(M_WIDTH, fwd), row(M_WIDTH, fwd), col(GATE_ROWS, fwd), row(LANES, fwd),
                    row(M_WIDTH, bwd), col(M_WIDTH, bwd), row(M_WIDTH, bwd), col(GATE_ROWS, bwd), row(LANES, bwd)]
        args = [q, kt, v, grow, gcol, q, kt, v, grow, gcol]
        out_specs = [row(M_WIDTH, fwd), row(M_WIDTH, bwd)] + state_specs
        out_shape = [jax.ShapeDtypeStruct((r, M_WIDTH), F32)] * 2 + state_shapes
    else:
        in_specs = [col(M_WIDTH, fwd), row(M_WIDTH, fwd), col(GATE_ROWS, fwd),
                    col(M_WIDTH, bwd), row(M_WIDTH, bwd), col(GATE_ROWS, bwd)]
        args = [kt, v, grow, kt, v, grow]
        out_specs = state_specs
        out_shape = state_shapes
    return pl.pallas_call(
        functools.partial(_mlstm_kernel, emit_h=emit_h),
        grid=(n_batch, nc),
        in_specs=in_specs + state_specs,
        out_specs=out_specs,
        out_shape=out_shape,
        scratch_shapes=[pltpu.VMEM((n_str, M_DH, M_AUG), F32), pltpu.VMEM((n_str, LANES), F32)],
        compiler_params=_params(2),
        name="mlstm" if emit_h else "mlstm_ctx",
    )(*args, c0, m0)


def _attn_kernel(sink_ref, q_ref, ktp_ref, ktc_ref, ktn_ref, vp_ref, vc_ref, vn_ref, ktx_ref, vx_ref, o_ref,
                 *, n_ctx):
    j = pl.program_id(1)
    nb = pl.num_programs(1)
    T = A_BLOCK
    n_loc = 3 * T
    n_key = n_loc + n_ctx
    half = LANES // 2
    i_idx = lax.broadcasted_iota(jnp.int32, (T, T), 0)
    r_idx = lax.broadcasted_iota(jnp.int32, (T, T), 1)
    prev_ok = (r_idx >= i_idx) & (j > 0)
    next_ok = (r_idx <= i_idx) & (j < nb - 1)
    lane_v = lax.broadcasted_iota(jnp.int32, (1, LANES), 1)
    low = lane_v < half
    v_rows = (vp_ref, vc_ref, vn_ref, vx_ref)
    zk = jnp.zeros((A_DH, n_key), BF16)
    for k in range(A_KV_HEADS):
        ks = slice(k * A_DH, (k + 1) * A_DH)
        kt_all = jnp.concatenate([ktp_ref[ks, :], ktc_ref[ks, :], ktn_ref[ks, :], ktx_ref[ks, :]], axis=1)
        rhs = jnp.concatenate([jnp.concatenate([kt_all, zk], axis=1),
                               jnp.concatenate([zk, kt_all], axis=1)], axis=0)
        top_sl = slice(0, LANES) if k == 0 else slice(LANES, 2 * LANES)
        bot_sl = slice(LANES, 2 * LANES) if k == 0 else slice(0, LANES)
        v_top = jnp.concatenate([jnp.where(low, vr[:, top_sl], 0) for vr in v_rows], axis=0)
        v_bot = jnp.concatenate([jnp.where(low, 0, vr[:, bot_sl]) for vr in v_rows], axis=0)
        ones_top = jnp.broadcast_to(jnp.where(low, 1.0, 0.0).astype(BF16), (n_key, LANES))
        ones_bot = jnp.broadcast_to(jnp.where(low, 0.0, 1.0).astype(BF16), (n_key, LANES))
        v2 = jnp.concatenate([jnp.concatenate([v_top.astype(BF16), ones_top], axis=1),
                              jnp.concatenate([v_bot.astype(BF16), ones_bot], axis=1)], axis=0)
        for p in range(A_HEADS // A_KV_HEADS // 2):
            g = k * (A_HEADS // A_KV_HEADS // 2) + p
            q2 = q_ref[:, g * LANES:(g + 1) * LANES]
            s2 = jnp.dot(q2, rhs, preferred_element_type=F32)
            es = []
            sink_terms = []
            for u in range(2):
                base = u * n_key
                sink = sink_ref[2 * g + u]
                s_prev = jnp.where(prev_ok, s2[:, base:base + T], -jnp.inf)
                s_cur = s2[:, base + T:base + 2 * T]
                s_next = jnp.where(next_ok, s2[:, base + 2 * T:base + 3 * T], -jnp.inf)
                s_ctx = s2[:, base + n_loc:base + n_key]
                m = jnp.maximum(jnp.maximum(s_prev, s_cur), s_next)
                for cc in range(n_ctx // T):
                    m = jnp.maximum(m, s_ctx[:, cc * T:(cc + 1) * T])
                m = jnp.maximum(jnp.max(m, axis=1, keepdims=True), sink)
                es += [jnp.exp(s_prev - m), jnp.exp(s_cur - m), jnp.exp(s_next - m), jnp.exp(s_ctx - m)]
                sink_terms.append(jnp.exp(sink - m))
            p2 = jnp.concatenate(es, axis=1).astype(BF16)
            o = jnp.dot(p2, v2, preferred_element_type=F32)
            den = o[:, LANES:] + jnp.where(low, sink_terms[0], sink_terms[1])
            o_ref[:, g * LANES:(g + 1) * LANES] = (o[:, :LANES] / den).astype(BF16)


def _attn_call(sink, qa, kat, va, kat_c, va_c, n_batch, seq_len, n_ctx):
    T = A_BLOCK
    nb = seq_len // T
    r = n_batch * seq_len
    cur = lambda b, j: b * nb + j
    prv = lambda b, j: b * nb + jnp.maximum(j - 1, 0)
    nxt = lambda b, j: b * nb + jnp.minimum(j + 1, nb - 1)
    kspec = lambda f: pl.BlockSpec((A_KV_WIDTH, T), lambda b, j: (0, f(b, j)))
    vspec = lambda f: pl.BlockSpec((T, 2 * A_KV_WIDTH), lambda b, j: (f(b, j), 0))
    return pl.pallas_call(
        functools.partial(_attn_kernel, n_ctx=n_ctx),
        grid=(n_batch, nb),
        in_specs=[pl.BlockSpec(memory_space=pltpu.SMEM),
                  pl.BlockSpec((T, A_WIDTH), lambda b, j: (cur(b, j), 0)),
                  kspec(prv), kspec(cur), kspec(nxt), vspec(prv), vspec(cur), vspec(nxt),
                  pl.BlockSpec((A_KV_WIDTH, n_ctx), lambda b, j: (0, b)),
                  pl.BlockSpec((n_ctx, 2 * A_KV_WIDTH), lambda b, j: (b, 0))],
        out_specs=pl.BlockSpec((T, A_WIDTH), lambda b, j: (cur(b, j), 0)),
        out_shape=jax.ShapeDtypeStruct((r, A_WIDTH), BF16),
        compiler_params=_params(2),
        name="attn",
    )(sink, qa, kat, kat, kat, va, va, va, kat_c, va_c)


def _outproj_kernel(hf_ref, hb_ref, op_ref, a_ref, x_ref, mod_ref, mg_ref, wo_ref, g_ref, b_ref, o_ref):
    h = hf_ref[...] + hb_ref[...]
    gate = _sigmoid(op_ref[...])
    parts = []
    for hd in range(M_HEADS):
        sl = slice(hd * M_DH, (hd + 1) * M_DH)
        seg = h[:, sl]
        mu = jnp.mean(seg, axis=-1, keepdims=True)
        sc = seg - mu
        var = jnp.mean(sc * sc, axis=-1, keepdims=True)
        parts.append((sc * lax.rsqrt(var + LN_EPS) * mg_ref[:, sl] * gate[:, sl]).astype(BF16))
    mixed = jnp.concatenate(parts + [a_ref[...]], axis=1)
    y = jnp.dot(mixed, wo_ref[...], preferred_element_type=F32)
    z = ALPHA * x_ref[...] + mod_ref[0, 5:6, :] * y
    o_ref[...] = _layer_norm(z, g_ref[...], b_ref[...])


def _outproj_call(hf, hb, o_pre, a_l, x1, mod, mod_index, mh_g, w_out, ln_g, ln_b, tm):
    r = x1.shape[0]
    row = lambda w: pl.BlockSpec((tm, w), lambda i: (i, 0))
    return pl.pallas_call(
        _outproj_kernel,
        grid=(r // tm,),
        in_specs=[row(M_WIDTH), row(M_WIDTH), row(M_WIDTH), row(A_WIDTH), row(D_MODEL),
                  pl.BlockSpec((1, N_MOD, D_MODEL), lambda i: (mod_index(i), 0, 0)),
                  _resident((1, M_WIDTH)),
                  _resident((D_MODEL, D_MODEL)),
                  _resident((1, D_MODEL)),
                  _resident((1, D_MODEL))],
        out_specs=row(D_MODEL),
        out_shape=jax.ShapeDtypeStruct((r, D_MODEL), F32),
        compiler_params=_params(1),
        name="outproj",
    )(hf, hb, o_pre, a_l, x1, mod, mh_g.reshape(1, M_WIDTH), w_out, ln_g.reshape(1, D_MODEL),
      ln_b.reshape(1, D_MODEL))


def _rope_tables(seq_len):
    t = jnp.arange(seq_len)
    row = (t // GRID_W).astype(F32)
    col = (t % GRID_W).astype(F32)
    n_freq = A_DH // 4
    inv = ROPE_BASE ** (-jnp.arange(n_freq, dtype=F32) / n_freq)
    ang = jnp.concatenate([row[:, None] * inv, col[:, None] * inv], -1)
    cos, sin = jnp.cos(ang), jnp.sin(ang)
    return jnp.tile(cos, (1, 4)), jnp.tile(jnp.concatenate([-sin, sin], -1), (1, 2))


def _spread_gates(a):
    lead = a.shape[:-1]
    a = a.reshape(lead + (4, M_HEADS))
    a = jnp.pad(a, [(0, 0)] * len(lead) + [(0, 0), (0, GATE_GROUP - M_HEADS)])
    a = a.reshape(lead + (GATE_ROWS,))
    return jnp.pad(a, [(0, 0)] * len(lead) + [(0, LANES - GATE_ROWS)])


def kernel(x, c, ctx, c_ctx, w_ada, b_ada, ln_g, ln_b, ffn_w13, ffn_w2, w_in, b_gates, conv_w, conv_b,
           mh_norm_g, attn_sink, w_out):
    n_batch, seq_len, d = x.shape
    n_ctx = ctx.shape[1]
    l = 0
    tm = 512
    tm_ctx = n_ctx
    tiles_per_batch = seq_len // tm

    cc = jnp.concatenate([c, c_ctx[None, :], jnp.zeros((SUBLANES - n_batch - 1, d), F32)], axis=0)
    mod = _mod_call(cc, w_ada[l], b_ada[l])[:n_batch + 1].reshape(n_batch + 1, N_MOD, d)
    lat_mod = lambda i: i // tiles_per_batch
    ctx_mod = lambda i: n_batch

    w13 = ffn_w13[l].astype(BF16)
    w2 = ffn_w2[l].astype(BF16)
    wi = w_in[l]
    wa = wi[:, OFF_QM:OFF_G].astype(BF16)
    wg = _spread_gates(wi[:, OFF_G:OFF_QA]).astype(BF16)
    bg = _spread_gates(b_gates[l]).reshape(1, LANES)
    w_va = wi[:, OFF_VA:N_IN]
    w_va_sw = jnp.concatenate([w_va[:, A_DH:], w_va[:, :A_DH]], axis=1)
    wt = jnp.concatenate([wi[:, OFF_QA:OFF_VA], w_va, w_va_sw], axis=1).astype(BF16)
    cb = conv_b[l].reshape(1, 2 * M_WIDTH)
    cos, sin = _rope_tables(seq_len)

    x_lat = x.reshape(n_batch * seq_len, d)
    x_ctx = ctx.reshape(n_batch * n_ctx, d)

    x1 = _ffn_call(x_lat, mod, lat_mod, 0, w13[0], w2[0], ln_g[l, 0], ln_b[l, 0], tm)
    xc1 = _ffn_call(x_ctx, mod, ctx_mod, 0, w13[0], w2[0], ln_g[l, 0], ln_b[l, 0], tm_ctx)

    (_, kt_c, v_c, _, grow_c, _, _, kat_c, va_c) = _inproj_call(
        xc1, mod, ctx_mod, wa, wg, bg, wt, conv_w[l], cb, None, None, tm_ctx, n_ctx)
    (q_l, kt_l, v_l, o_pre, grow_l, gcol_l, qa_l, kat_l, va_l) = _inproj_call(
        x1, mod, lat_mod, wa, wg, bg, wt, conv_w[l], cb, cos, sin, tm, seq_len)

    n_str = 2 * M_HEADS
    c0 = jnp.zeros((n_batch, n_str, M_DH, M_AUG), F32)
    m0 = jnp.zeros((n_batch, n_str, LANES), F32)
    c_st, m_st = _mlstm_call(None, kt_c, v_c, grow_c, None, c0, m0, n_batch, n_ctx, False)
    hf, hb, _, _ = _mlstm_call(q_l, kt_l, v_l, grow_l, gcol_l, c_st, m_st, n_batch, seq_len, True)

    a_l = _attn_call(attn_sink[l], qa_l, kat_l, va_l, kat_c, va_c, n_batch, seq_len, n_ctx)

    x2 = _outproj_call(hf, hb, o_pre, a_l, x1, mod, lat_mod, mh_norm_g[l], w_out[l].astype(BF16),
                       ln_g[l, 1], ln_b[l, 1], tm)

    out = _ffn_call(x2, mod, lat_mod, 6, w13[1], w2[1], ln_g[l, 2], ln_b[l, 2], tm)
    return out.reshape(n_batch, seq_len, d)
```

```python
import functools

import jax
import jax.numpy as jnp
from jax import lax
from jax.experimental import pallas as pl
from jax.experimental.pallas import tpu as pltpu

F32 = jnp.float32
BF16 = jnp.bfloat16

D_MODEL = 1024
GRID_W = 64
M_HEADS = 4
M_DH = 128
M_WIDTH = M_HEADS * M_DH
A_HEADS = 8
A_KV_HEADS = 2
A_DH = 64
A_WIDTH = A_HEADS * A_DH
A_KV_WIDTH = A_KV_HEADS * A_DH
A_BLOCK = 128
CONV_W = 5
D_FF = 2816
ROPE_BASE = 10000.0
N_MOD = 9
LN_EPS = 1e-5
FFN_RES = 0.5
DEPTH = 1
ALPHA = (2.0 * DEPTH) ** 0.25

OFF_QM = 0
OFF_KM = OFF_QM + M_WIDTH
OFF_VM = OFF_KM + M_WIDTH
OFF_OM = OFF_VM + M_WIDTH
OFF_G = OFF_OM + M_WIDTH
N_GATES = 4 * M_HEADS
OFF_QA = OFF_G + N_GATES
OFF_KA = OFF_QA + A_WIDTH
OFF_VA = OFF_KA + A_KV_WIDTH
N_IN = OFF_VA + A_KV_WIDTH

LANES = 128
SUBLANES = 8
HALO = SUBLANES
MLSTM_CHUNK = LANES
FF_CHUNK = 256
GATE_GROUP = SUBLANES
VMEM_LIMIT = 56 * 1024 * 1024


def _sigmoid(x):
    return 1.0 / (1.0 + jnp.exp(-x))


def _log_sigmoid(x):
    return jnp.minimum(x, 0.0) - jnp.log1p(jnp.exp(-jnp.abs(x)))


def _layer_norm(z, g, b):
    mu = jnp.mean(z, axis=-1, keepdims=True)
    zc = z - mu
    var = jnp.mean(zc * zc, axis=-1, keepdims=True)
    return zc * lax.rsqrt(var + LN_EPS) * g + b


def _params(n_grid):
    return pltpu.CompilerParams(dimension_semantics=("arbitrary",) * n_grid,
                                vmem_limit_bytes=VMEM_LIMIT)


def _resident(shape):
    nd = len(shape)
    return pl.BlockSpec(shape, lambda *_: (0,) * nd, pipeline_mode=pl.Buffered(1))


def _mod_kernel(c_ref, w_ref, b_ref, o_ref):
    c = c_ref[...]
    s = (c * _sigmoid(c)).astype(BF16)
    o_ref[...] = jnp.dot(s, w_ref[...].astype(BF16), preferred_element_type=F32) + b_ref[...]


def _mod_call(cc, w_ada, b_ada):
    n = w_ada.shape[1]
    tn = D_MODEL
    return pl.pallas_call(
        _mod_kernel,
        grid=(n // tn,),
        in_specs=[pl.BlockSpec((SUBLANES, D_MODEL), lambda j: (0, 0)),
                  pl.BlockSpec((D_MODEL, tn), lambda j: (0, j)),
                  pl.BlockSpec((1, tn), lambda j: (0, j))],
        out_specs=pl.BlockSpec((SUBLANES, tn), lambda j: (0, j)),
        out_shape=jax.ShapeDtypeStruct((SUBLANES, n), F32),
        compiler_params=_params(1),
        name="mod",
    )(cc, w_ada, b_ada.reshape(1, n))


def _ffn_body(x, mod_ref, k0, w13_ref, w2_ref, g_ref, b_ref, h_ref, acc_ref):
    shift = mod_ref[0, k0:k0 + 1, :]
    scale = mod_ref[0, k0 + 1:k0 + 2, :]
    gate = mod_ref[0, k0 + 2:k0 + 3, :]
    h_ref[...] = (x * (1.0 + scale) + shift).astype(BF16)
    for j in range(D_FF // FF_CHUNK):
        lo = j * FF_CHUNK
        h = h_ref[...]
        a1 = jnp.dot(h, w13_ref[:, lo:lo + FF_CHUNK], preferred_element_type=F32)
        a3 = jnp.dot(h, w13_ref[:, D_FF + lo:D_FF + lo + FF_CHUNK], preferred_element_type=F32)
        g = (a1 * _sigmoid(a1) * a3).astype(BF16)
        y = jnp.dot(g, w2_ref[lo:lo + FF_CHUNK, :], preferred_element_type=F32)
        if j == 0:
            acc_ref[...] = y
        else:
            acc_ref[...] += y
    z = ALPHA * x + (FFN_RES * gate) * acc_ref[...]
    return _layer_norm(z, g_ref[...], b_ref[...])


def _ffn_kernel(x_ref, mod_ref, w13_ref, w2_ref, g_ref, b_ref, o_ref, h_ref, acc_ref, *, k0):
    o_ref[...] = _ffn_body(x_ref[...], mod_ref, k0, w13_ref, w2_ref, g_ref, b_ref, h_ref, acc_ref)


def _ffn_call(x, mod, mod_index, k0, w13, w2, ln_g, ln_b, tm):
    r = x.shape[0]
    return pl.pallas_call(
        functools.partial(_ffn_kernel, k0=k0),
        grid=(r // tm,),
        in_specs=[pl.BlockSpec((tm, D_MODEL), lambda i: (i, 0)),
                  pl.BlockSpec((1, N_MOD, D_MODEL), lambda i: (mod_index(i), 0, 0)),
                  _resident((D_MODEL, 2 * D_FF)),
                  _resident((D_FF, D_MODEL)),
                  _resident((1, D_MODEL)),
                  _resident((1, D_MODEL))],
        out_specs=pl.BlockSpec((tm, D_MODEL), lambda i: (i, 0)),
        out_shape=jax.ShapeDtypeStruct((r, D_MODEL), F32),
        scratch_shapes=[pltpu.VMEM((tm, D_MODEL), BF16), pltpu.VMEM((tm, D_MODEL), F32)],
        compiler_params=_params(1),
        name="ffn",
    )(x, mod, w13, w2, ln_g.reshape(1, D_MODEL), ln_b.reshape(1, D_MODEL))


N_WA = 2 * M_WIDTH + 2 * M_WIDTH
OFF_TG = A_WIDTH + A_KV_WIDTH + 2 * A_KV_WIDTH
N_WT = OFF_TG + LANES
ROW_A = (0, 3 * GATE_GROUP)
ROW_B = (GATE_GROUP, 4 * GATE_GROUP)
ROW_C = (2 * GATE_GROUP, 5 * GATE_GROUP)
GATE_ROWS = 6 * GATE_GROUP


def _scan(x, lane, op, reverse):
    fill = 0.0 if op is jnp.add else -jnp.inf
    k = 1
    while k < LANES:
        if reverse:
            other = jnp.where(lane < LANES - k, pltpu.roll(x, LANES - k, 1), fill)
        else:
            other = jnp.where(lane >= k, pltpu.roll(x, k, 1), fill)
        x = op(x, other)
        k *= 2
    return x


def _chunk_scan(x, lane, op, reverse):
    return jnp.concatenate([_scan(x[:, c * LANES:(c + 1) * LANES], lane, op, reverse)
                            for c in range(x.shape[1] // LANES)], axis=1)


def _rope(x, cos, sin_signed, first_half):
    swapped = jnp.where(first_half, pltpu.roll(x, LANES - A_DH // 2, 1), pltpu.roll(x, A_DH // 2, 1))
    return x * cos + swapped * sin_signed


def _inproj_kernel(*refs, tiles_per_seq, rope, tm):
    if rope:
        (xp_ref, x_ref, xn_ref, mod_ref, wa_ref, wt_ref, bg_ref, cw_ref, cb_ref, cos_ref, sin_ref,
         q_ref, kt_ref, v_ref, o_ref, grow_ref, qa_ref, kat_ref, va_ref, xs_ref) = refs
    else:
        (xp_ref, x_ref, xn_ref, mod_ref, wa_ref, wt_ref, bg_ref, cw_ref, cb_ref,
         q_ref, kt_ref, v_ref, o_ref, grow_ref, qa_ref, kat_ref, va_ref, xs_ref) = refs
    i = pl.program_id(0)
    pos = i % tiles_per_seq
    shift = mod_ref[0, 3:4, :]
    scale1 = 1.0 + mod_ref[0, 4:5, :]
    h2 = (x_ref[...] * scale1 + shift).astype(BF16)
    xh = jnp.concatenate([xp_ref[...], xn_ref[...]], axis=0)
    hh = (xh * scale1 + shift).astype(BF16)

    pa = jnp.dot(h2, wa_ref[...], preferred_element_type=F32)
    ph = jnp.dot(hh, wa_ref[:, :2 * M_WIDTH], preferred_element_type=F32)
    xs_ref[0:HALO, :] = jnp.where(pos == 0, 0.0, ph[0:HALO])
    xs_ref[HALO:HALO + tm, :] = pa[:, :2 * M_WIDTH]
    xs_ref[HALO + tm:2 * HALO + tm, :] = jnp.where(pos == tiles_per_seq - 1, 0.0, ph[HALO:2 * HALO])
    acc = cb_ref[...] + cw_ref[0:1, :] * xs_ref[pl.ds(HALO - CONV_W // 2, tm), :]
    for j in range(1, CONV_W):
        acc = acc + cw_ref[j:j + 1, :] * xs_ref[pl.ds(HALO - CONV_W // 2 + j, tm), :]
    qk = acc * _sigmoid(acc)
    q_ref[...] = (qk[:, :M_WIDTH] * (M_DH ** -0.5)).astype(BF16)
    kt_ref[...] = qk[:, M_WIDTH:].T.astype(BF16)
    v_ref[...] = pa[:, 2 * M_WIDTH:3 * M_WIDTH].astype(BF16)
    o_ref[...] = pa[:, 3 * M_WIDTH:]

    pt = jnp.dot(h2, wt_ref[...], preferred_element_type=F32)

    pgt = (pt[:, OFF_TG:] + bg_ref[...]).T
    li_f = pgt[0:GATE_GROUP]
    lf_f = _log_sigmoid(pgt[GATE_GROUP:2 * GATE_GROUP])
    li_b = pgt[2 * GATE_GROUP:3 * GATE_GROUP]
    lf_b = _log_sigmoid(pgt[3 * GATE_GROUP:4 * GATE_GROUP])
    lane = lax.broadcasted_iota(jnp.int32, (GATE_GROUP, LANES), 1)
    b_f = _chunk_scan(lf_f, lane, jnp.add, False)
    e_b = _chunk_scan(lf_b, lane, jnp.add, True)
    a_f = li_f - b_f
    a_b = li_b - e_b
    grow_ref[...] = jnp.concatenate([a_f, b_f, _chunk_scan(a_f, lane, jnp.maximum, False),
                                     a_b, e_b, _chunk_scan(a_b, lane, jnp.maximum, True)], axis=0)

    ka = pt[:, A_WIDTH:A_WIDTH + A_KV_WIDTH]
    if rope:
        cos = cos_ref[...]
        sin = sin_ref[...]
        lane_t = lax.broadcasted_iota(jnp.int32, (tm, LANES), 1)
        first_half = (lane_t % A_DH) < (A_DH // 2)
        for g in range(A_WIDTH // LANES):
            xg = _rope(pt[:, g * LANES:(g + 1) * LANES], cos, sin, first_half)
            qa_ref[:, g * LANES:(g + 1) * LANES] = (xg * (A_DH ** -0.5)).astype(BF16)
        ka = _rope(ka, cos, sin, first_half)
    else:
        qa_ref[...] = (pt[:, :A_WIDTH] * (A_DH ** -0.5)).astype(BF16)
    kat_ref[...] = ka.T.astype(BF16)
    va_ref[...] = pt[:, A_WIDTH + A_KV_WIDTH:OFF_TG].astype(BF16)


def _inproj_call(x, mod, mod_index, wa, wt, bg, conv_w, conv_b, cos, sin, tm, seq_len):
    r = x.shape[0]
    tiles_per_seq = seq_len // tm
    rope = cos is not None
    hb = tm // HALO
    n_halo = r // HALO
    in_specs = [pl.BlockSpec((HALO, D_MODEL), lambda i: (jnp.maximum(i * hb - 1, 0), 0)),
                pl.BlockSpec((tm, D_MODEL), lambda i: (i, 0)),
                pl.BlockSpec((HALO, D_MODEL), lambda i: (jnp.minimum((i + 1) * hb, n_halo - 1), 0)),
                pl.BlockSpec((1, N_MOD, D_MODEL), lambda i: (mod_index(i), 0, 0)),
                _resident((D_MODEL, N_WA)),
                _resident((D_MODEL, N_WT)),
                _resident((1, LANES)),
                _resident((CONV_W, 2 * M_WIDTH)),
                _resident((1, 2 * M_WIDTH))]
    args = [x, x, x, mod, wa, wt, bg, conv_w, conv_b]
    if rope:
        in_specs += [pl.BlockSpec((tm, LANES), lambda i: (i % tiles_per_seq, 0)),
                     pl.BlockSpec((tm, LANES), lambda i: (i % tiles_per_seq, 0))]
        args += [cos, sin]
    row = lambda w: pl.BlockSpec((tm, w), lambda i: (i, 0))
    col = lambda h: pl.BlockSpec((h, tm), lambda i: (0, i))
    out_specs = [row(M_WIDTH), col(M_WIDTH), row(M_WIDTH), row(M_WIDTH), col(GATE_ROWS),
                 row(A_WIDTH), col(A_KV_WIDTH), row(2 * A_KV_WIDTH)]
    out_shape = [jax.ShapeDtypeStruct((r, M_WIDTH), BF16),
                 jax.ShapeDtypeStruct((M_WIDTH, r), BF16),
                 jax.ShapeDtypeStruct((r, M_WIDTH), BF16),
                 jax.ShapeDtypeStruct((r, M_WIDTH), F32),
                 jax.ShapeDtypeStruct((GATE_ROWS, r), F32),
                 jax.ShapeDtypeStruct((r, A_WIDTH), BF16),
                 jax.ShapeDtypeStruct((A_KV_WIDTH, r), BF16),
                 jax.ShapeDtypeStruct((r, 2 * A_KV_WIDTH), BF16)]
    return pl.pallas_call(
        functools.partial(_inproj_kernel, tiles_per_seq=tiles_per_seq, rope=rope, tm=tm),
        grid=(r // tm,),
        in_specs=in_specs,
        out_specs=out_specs,
        out_shape=out_shape,
        scratch_shapes=[pltpu.VMEM((tm + 2 * HALO, 2 * M_WIDTH), F32)],
        compiler_params=_params(1),
        name="inproj",
    )(*args)


M_AUG = 2 * M_DH
N_STREAMS = 2 * M_HEADS


def _col_replicated(row):
    return jnp.broadcast_to(row, (LANES, row.shape[1])).T


def _mlstm_kernel(*refs, emit_h):
    if emit_h:
        (qf_ref, ktf_ref, vf_ref, grf_ref, qb_ref, ktb_ref, vb_ref, grb_ref, c0_ref, m0_ref,
         hf_ref, hb_ref, cout_ref, mout_ref, c_ref, m_ref) = refs
        q_refs, h_refs = (qf_ref, qb_ref), (hf_ref, hb_ref)
    else:
        (ktf_ref, vf_ref, grf_ref, ktb_ref, vb_ref, grb_ref, c0_ref, m0_ref,
         cout_ref, mout_ref, c_ref, m_ref) = refs
    kt_refs, v_refs, gr_refs = (ktf_ref, ktb_ref), (vf_ref, vb_ref), (grf_ref, grb_ref)
    L = MLSTM_CHUNK
    c = pl.program_id(1)

    @pl.when(c == 0)
    def _():
        c_ref[...] = c0_ref[0]
        m_ref[...] = m0_ref[0]

    t_idx = lax.broadcasted_iota(jnp.int32, (L, L), 0)
    s_idx = lax.broadcasted_iota(jnp.int32, (L, L), 1)
    masks = (s_idx <= t_idx, s_idx >= t_idx)
    ones = jnp.ones((L, M_DH), BF16)
    zeros_k = jnp.zeros((M_DH, L), BF16)
    m_all = m_ref[...]
    m_new = []
    for d in range(2):
        last = (L - 1, 0)[d]
        gr = gr_refs[d][...]
        for pair in range(M_HEADS // 2):
            if emit_h:
                rows = slice(2 * pair * M_DH, (2 * pair + 2) * M_DH)
                kt2 = kt_refs[d][rows, :]
                rhs = jnp.concatenate(
                    [jnp.concatenate([kt2[:M_DH], zeros_k], axis=1),
                     jnp.concatenate([zeros_k, kt2[M_DH:]], axis=1)], axis=0)
                s2 = jnp.dot(q_refs[d][:, rows], rhs, preferred_element_type=F32)
            for hh in range(2):
                h = 2 * pair + hh
                r = d * M_HEADS + h
                sl = slice(h * M_DH, (h + 1) * M_DH)
                a_row = gr[ROW_A[d] + h:ROW_A[d] + h + 1]
                b_row = gr[ROW_B[d] + h:ROW_B[d] + h + 1]
                m_row = m_all[r:r + 1]
                ct = c_ref[r]
                kt = kt_refs[d][sl, :]
                v_aug = jnp.concatenate([v_refs[d][:, sl], ones], axis=1)
                if emit_h:
                    mu_row = jnp.maximum(m_row, gr[ROW_C[d] + h:ROW_C[d] + h + 1])
                    mu = _col_replicated(mu_row)
                    mt = _col_replicated(b_row + mu_row)
                    dmat = jnp.where(masks[d], jnp.exp(a_row - mu), 0.0)
                    p = (s2[:, hh * L:(hh + 1) * L] * dmat).astype(BF16)
                    qs = (q_refs[d][:, sl].astype(F32) * jnp.exp(m_row - mu)).astype(BF16)
                    tot = jnp.dot(jnp.concatenate([p, qs], axis=1),
                                  jnp.concatenate([v_aug, ct.astype(BF16)], axis=0),
                                  preferred_element_type=F32)
                    h_refs[d][:, sl] = tot[:, :M_DH] / jnp.maximum(jnp.abs(tot[:, M_DH:]), jnp.exp(-mt))
                mp = jnp.maximum(m_row, jnp.max(a_row, axis=1, keepdims=True))
                ktw = (kt.astype(F32) * jnp.exp(a_row - mp)).astype(BF16)
                decay = jnp.exp(m_row - mp)
                c_ref[r] = (jnp.concatenate([decay, decay], axis=1) * ct
                            + jnp.dot(ktw, v_aug, preferred_element_type=F32))
                m_new.append(b_row[:, last:last + 1] + mp)
    m_ref[...] = jnp.concatenate(m_new, axis=0)

    @pl.when(c == pl.num_programs(1) - 1)
    def _():
        cout_ref[0] = c_ref[...]
        mout_ref[0] = m_ref[...]


def _mlstm_call(q, kt, v, grow, c0, m0, n_batch, seq_len, emit_h):
    L = MLSTM_CHUNK
    nc = seq_len // L
    r = n_batch * seq_len
    fwd = lambda b, c: b * nc + c
    bwd = lambda b, c: b * nc + nc - 1 - c
    row = lambda w, f: pl.BlockSpec((L, w), lambda b, c: (f(b, c), 0))
    col = lambda hgt, f: pl.BlockSpec((hgt, L), lambda b, c: (0, f(b, c)))
    state_specs = [pl.BlockSpec((1, N_STREAMS, M_DH, M_AUG), lambda b, c: (b, 0, 0, 0)),
                   pl.BlockSpec((1, N_STREAMS, LANES), lambda b, c: (b, 0, 0))]
    state_shapes = [jax.ShapeDtypeStruct((n_batch, N_STREAMS, M_DH, M_AUG), F32),
                    jax.ShapeDtypeStruct((n_batch, N_STREAMS, LANES), F32)]
    if emit_h:
        in_specs = [row(M_WIDTH, fwd), col(M_WIDTH, fwd), row(M_WIDTH, fwd), col(GATE_ROWS, fwd),
                    row(M_WIDTH, bwd), col(M_WIDTH, bwd), row(M_WIDTH, bwd), col(GATE_ROWS, bwd)]
        args = [q, kt, v, grow, q, kt, v, grow]
        out_specs = [row(M_WIDTH, fwd), row(M_WIDTH, bwd)] + state_specs
        out_shape = [jax.ShapeDtypeStruct((r, M_WIDTH), F32)] * 2 + state_shapes
    else:
        in_specs = [col(M_WIDTH, fwd), row(M_WIDTH, fwd), col(GATE_ROWS, fwd),
                    col(M_WIDTH, bwd), row(M_WIDTH, bwd), col(GATE_ROWS, bwd)]
        args = [kt, v, grow, kt, v, grow]
        out_specs = state_specs
        out_shape = state_shapes
    return pl.pallas_call(
        functools.partial(_mlstm_kernel, emit_h=emit_h),
        grid=(n_batch, nc),
        in_specs=in_specs + state_specs,
        out_specs=out_specs,
        out_shape=out_shape,
        scratch_shapes=[pltpu.VMEM((N_STREAMS, M_DH, M_AUG), F32), pltpu.VMEM((N_STREAMS, LANES), F32)],
        compiler_params=_params(2),
        name="mlstm" if emit_h else "mlstm_ctx",
    )(*args, c0, m0)


def _attn_kernel(sink_ref, q_ref, ktp_ref, ktc_ref, ktn_ref, vp_ref, vc_ref, vn_ref, ktx_ref, vx_ref, o_ref,
                 *, n_ctx):
    j = pl.program_id(1)
    nb = pl.num_programs(1)
    T = A_BLOCK
    n_loc = 3 * T
    n_key = n_loc + n_ctx
    half = LANES // 2
    i_idx = lax.broadcasted_iota(jnp.int32, (T, T), 0)
    r_idx = lax.broadcasted_iota(jnp.int32, (T, T), 1)
    prev_ok = (r_idx >= i_idx) & (j > 0)
    next_ok = (r_idx <= i_idx) & (j < nb - 1)
    lane_v = lax.broadcasted_iota(jnp.int32, (1, LANES), 1)
    low = lane_v < half
    v_rows = (vp_ref, vc_ref, vn_ref, vx_ref)
    zk = jnp.zeros((A_DH, n_key), BF16)
    for k in range(A_KV_HEADS):
        ks = slice(k * A_DH, (k + 1) * A_DH)
        kt_all = jnp.concatenate([ktp_ref[ks, :], ktc_ref[ks, :], ktn_ref[ks, :], ktx_ref[ks, :]], axis=1)
        rhs = jnp.concatenate([jnp.concatenate([kt_all, zk], axis=1),
                               jnp.concatenate([zk, kt_all], axis=1)], axis=0)
        top_sl = slice(0, LANES) if k == 0 else slice(LANES, 2 * LANES)
        bot_sl = slice(LANES, 2 * LANES) if k == 0 else slice(0, LANES)
        v_top = jnp.concatenate([jnp.where(low, vr[:, top_sl], 0) for vr in v_rows], axis=0)
        v_bot = jnp.concatenate([jnp.where(low, 0, vr[:, bot_sl]) for vr in v_rows], axis=0)
        ones_top = jnp.broadcast_to(jnp.where(low, 1.0, 0.0).astype(BF16), (n_key, LANES))
        ones_bot = jnp.broadcast_to(jnp.where(low, 0.0, 1.0).astype(BF16), (n_key, LANES))
        v2 = jnp.concatenate([jnp.concatenate([v_top.astype(BF16), ones_top], axis=1),
                              jnp.concatenate([v_bot.astype(BF16), ones_bot], axis=1)], axis=0)
        for p in range(A_HEADS // A_KV_HEADS // 2):
            g = k * (A_HEADS // A_KV_HEADS // 2) + p
            q2 = q_ref[:, g * LANES:(g + 1) * LANES]
            s2 = jnp.dot(q2, rhs, preferred_element_type=F32)
            es = []
            sink_terms = []
            for u in range(2):
                base = u * n_key
                sink = sink_ref[2 * g + u]
                s_prev = jnp.where(prev_ok, s2[:, base:base + T], -jnp.inf)
                s_cur = s2[:, base + T:base + 2 * T]
                s_next = jnp.where(next_ok, s2[:, base + 2 * T:base + 3 * T], -jnp.inf)
                s_ctx = s2[:, base + n_loc:base + n_key]
                m = jnp.maximum(jnp.maximum(s_prev, s_cur), s_next)
                for cc in range(n_ctx // T):
                    m = jnp.maximum(m, s_ctx[:, cc * T:(cc + 1) * T])
                m = jnp.maximum(jnp.max(m, axis=1, keepdims=True), sink)
                es += [jnp.exp(s_prev - m), jnp.exp(s_cur - m), jnp.exp(s_next - m), jnp.exp(s_ctx - m)]
                sink_terms.append(jnp.exp(sink - m))
            p2 = jnp.concatenate(es, axis=1).astype(BF16)
            o = jnp.dot(p2, v2, preferred_element_type=F32)
            den = o[:, LANES:] + jnp.where(low, sink_terms[0], sink_terms[1])
            o_ref[:, g * LANES:(g + 1) * LANES] = (o[:, :LANES] / den).astype(BF16)


def _attn_call(sink, qa, kat, va, kat_c, va_c, n_batch, seq_len, n_ctx):
    T = A_BLOCK
    nb = seq_len // T
    r = n_batch * seq_len
    cur = lambda b, j: b * nb + j
    prv = lambda b, j: b * nb + jnp.maximum(j - 1, 0)
    nxt = lambda b, j: b * nb + jnp.minimum(j + 1, nb - 1)
    kspec = lambda f: pl.BlockSpec((A_KV_WIDTH, T), lambda b, j: (0, f(b, j)))
    vspec = lambda f: pl.BlockSpec((T, 2 * A_KV_WIDTH), lambda b, j: (f(b, j), 0))
    return pl.pallas_call(
        functools.partial(_attn_kernel, n_ctx=n_ctx),
        grid=(n_batch, nb),
        in_specs=[pl.BlockSpec(memory_space=pltpu.SMEM),
                  pl.BlockSpec((T, A_WIDTH), lambda b, j: (cur(b, j), 0)),
                  kspec(prv), kspec(cur), kspec(nxt), vspec(prv), vspec(cur), vspec(nxt),
                  pl.BlockSpec((A_KV_WIDTH, n_ctx), lambda b, j: (0, b)),
                  pl.BlockSpec((n_ctx, 2 * A_KV_WIDTH), lambda b, j: (b, 0))],
        out_specs=pl.BlockSpec((T, A_WIDTH), lambda b, j: (cur(b, j), 0)),
        out_shape=jax.ShapeDtypeStruct((r, A_WIDTH), BF16),
        compiler_params=_params(2),
        name="attn",
    )(sink, qa, kat, kat, kat, va, va, va, kat_c, va_c)


def _mix_ffn_kernel(hf_ref, hb_ref, op_ref, a_ref, x_ref, mod_ref, mg_ref, wo_ref, g1_ref, b1_ref,
                    w13_ref, w2_ref, g2_ref, b2_ref, o_ref, h_ref, acc_ref):
    h = hf_ref[...] + hb_ref[...]
    gate = _sigmoid(op_ref[...])
    parts = []
    for hd in range(M_HEADS):
        sl = slice(hd * M_DH, (hd + 1) * M_DH)
        seg = h[:, sl]
        mu = jnp.mean(seg, axis=-1, keepdims=True)
        sc = seg - mu
        var = jnp.mean(sc * sc, axis=-1, keepdims=True)
        parts.append((sc * lax.rsqrt(var + LN_EPS) * mg_ref[:, sl] * gate[:, sl]).astype(BF16))
    mixed = jnp.concatenate(parts + [a_ref[...]], axis=1)
    y = jnp.dot(mixed, wo_ref[...], preferred_element_type=F32)
    x2 = _layer_norm(ALPHA * x_ref[...] + mod_ref[0, 5:6, :] * y, g1_ref[...], b1_ref[...])
    o_ref[...] = _ffn_body(x2, mod_ref, 6, w13_ref, w2_ref, g2_ref, b2_ref, h_ref, acc_ref)


def _mix_ffn_call(hf, hb, o_pre, a_l, x1, mod, mod_index, mh_g, w_out, g1, b1, w13, w2, g2, b2, tm):
    r = x1.shape[0]
    row = lambda w: pl.BlockSpec((tm, w), lambda i: (i, 0))
    vec = lambda a: a.reshape(1, a.shape[-1])
    return pl.pallas_call(
        _mix_ffn_kernel,
        grid=(r // tm,),
        in_specs=[row(M_WIDTH), row(M_WIDTH), row(M_WIDTH), row(A_WIDTH), row(D_MODEL),
                  pl.BlockSpec((1, N_MOD, D_MODEL), lambda i: (mod_index(i), 0, 0)),
                  _resident((1, M_WIDTH)),
                  _resident((D_MODEL, D_MODEL)),
                  _resident((1, D_MODEL)),
                  _resident((1, D_MODEL)),
                  _resident((D_MODEL, 2 * D_FF)),
                  _resident((D_FF, D_MODEL)),
                  _resident((1, D_MODEL)),
                  _resident((1, D_MODEL))],
        out_specs=row(D_MODEL),
        out_shape=jax.ShapeDtypeStruct((r, D_MODEL), F32),
        scratch_shapes=[pltpu.VMEM((tm, D_MODEL), BF16), pltpu.VMEM((tm, D_MODEL), F32)],
        compiler_params=_params(1),
        name="mix_ffn",
    )(hf, hb, o_pre, a_l, x1, mod, vec(mh_g), w_out, vec(g1), vec(b1), w13, w2, vec(g2), vec(b2))


def _rope_tables(seq_len):
    t = jnp.arange(seq_len)
    row = (t // GRID_W).astype(F32)
    col = (t % GRID_W).astype(F32)
    n_freq = A_DH // 4
    inv = ROPE_BASE ** (-jnp.arange(n_freq, dtype=F32) / n_freq)
    ang = jnp.concatenate([row[:, None] * inv, col[:, None] * inv], -1)
    cos, sin = jnp.cos(ang), jnp.sin(ang)
    return jnp.tile(cos, (1, 4)), jnp.tile(jnp.concatenate([-sin, sin], -1), (1, 2))


def _spread_gates(a):
    lead = a.shape[:-1]
    a = a.reshape(lead + (4, M_HEADS))
    a = jnp.pad(a, [(0, 0)] * len(lead) + [(0, 0), (0, GATE_GROUP - M_HEADS)])
    a = a.reshape(lead + (4 * GATE_GROUP,))
    return jnp.pad(a, [(0, 0)] * len(lead) + [(0, LANES - 4 * GATE_GROUP)])


def kernel(x, c, ctx, c_ctx, w_ada, b_ada, ln_g, ln_b, ffn_w13, ffn_w2, w_in, b_gates, conv_w, conv_b,
           mh_norm_g, attn_sink, w_out):
    n_batch, seq_len, d = x.shape
    n_ctx = ctx.shape[1]
    l = 0
    tm = 512
    tm_ctx = n_ctx
    tiles_per_batch = seq_len // tm

    cc = jnp.concatenate([c, c_ctx[None, :], jnp.zeros((SUBLANES - n_batch - 1, d), F32)], axis=0)
    mod = _mod_call(cc, w_ada[l], b_ada[l])[:n_batch + 1].reshape(n_batch + 1, N_MOD, d)
    lat_mod = lambda i: i // tiles_per_batch
    ctx_mod = lambda i: n_batch

    w13_a, w13_b = ffn_w13[l, 0].astype(BF16), ffn_w13[l, 1].astype(BF16)
    w2_a, w2_b = ffn_w2[l, 0].astype(BF16), ffn_w2[l, 1].astype(BF16)
    wi = w_in[l]
    wa = wi[:, OFF_QM:OFF_G].astype(BF16)
    w_va = wi[:, OFF_VA:N_IN]
    wt = jnp.concatenate([wi[:, OFF_QA:OFF_VA], w_va, w_va[:, A_DH:], w_va[:, :A_DH],
                          _spread_gates(wi[:, OFF_G:OFF_QA])], axis=1).astype(BF16)
    bg = _spread_gates(b_gates[l]).reshape(1, LANES)
    cb = conv_b[l].reshape(1, 2 * M_WIDTH)
    cos, sin = _rope_tables(seq_len)

    x_lat = x.reshape(n_batch * seq_len, d)
    x_ctx = ctx.reshape(n_batch * n_ctx, d)

    x1 = _ffn_call(x_lat, mod, lat_mod, 0, w13_a, w2_a, ln_g[l, 0], ln_b[l, 0], tm)
    xc1 = _ffn_call(x_ctx, mod, ctx_mod, 0, w13_a, w2_a, ln_g[l, 0], ln_b[l, 0], tm_ctx)

    (_, kt_c, v_c, _, grow_c, _, kat_c, va_c) = _inproj_call(
        xc1, mod, ctx_mod, wa, wt, bg, conv_w[l], cb, None, None, tm_ctx, n_ctx)
    (q_l, kt_l, v_l, o_pre, grow_l, qa_l, kat_l, va_l) = _inproj_call(
        x1, mod, lat_mod, wa, wt, bg, conv_w[l], cb, cos, sin, tm, seq_len)

    c0 = jnp.zeros((n_batch, N_STREAMS, M_DH, M_AUG), F32)
    m0 = jnp.zeros((n_batch, N_STREAMS, LANES), F32)
    c_st, m_st = _mlstm_call(None, kt_c, v_c, grow_c, c0, m0, n_batch, n_ctx, False)
    hf, hb, _, _ = _mlstm_call(q_l, kt_l, v_l, grow_l, c_st, m_st, n_batch, seq_len, True)

    a_l = _attn_call(attn_sink[l], qa_l, kat_l, va_l, kat_c, va_c, n_batch, seq_len, n_ctx)

    out = _mix_ffn_call(hf, hb, o_pre, a_l, x1, mod, lat_mod, mh_norm_g[l], w_out[l].astype(BF16),
                        ln_g[l, 1], ln_b[l, 1], w13_b, w2_b, ln_g[l, 2], ln_b[l, 2], tm)
    return out.reshape(n_batch, seq_len, d)
```

```python
import functools

import jax
import jax.numpy as jnp
from jax import lax
from jax.experimental import pallas as pl
from jax.experimental.pallas import tpu as pltpu

F32 = jnp.float32
BF16 = jnp.bfloat16

D_MODEL = 1024
GRID_W = 64
M_HEADS = 4
M_DH = 128
M_WIDTH = M_HEADS * M_DH
A_HEADS = 8
A_KV_HEADS = 2
A_DH = 64
A_WIDTH = A_HEADS * A_DH
A_KV_WIDTH = A_KV_HEADS * A_DH
A_BLOCK = 128
CONV_W = 5
D_FF = 2816
ROPE_BASE = 10000.0
N_MOD = 9
LN_EPS = 1e-5
FFN_RES = 0.5
DEPTH = 1
ALPHA = (2.0 * DEPTH) ** 0.25

OFF_QM = 0
OFF_KM = OFF_QM + M_WIDTH
OFF_VM = OFF_KM + M_WIDTH
OFF_OM = OFF_VM + M_WIDTH
OFF_G = OFF_OM + M_WIDTH
N_GATES = 4 * M_HEADS
OFF_QA = OFF_G + N_GATES
OFF_KA = OFF_QA + A_WIDTH
OFF_VA = OFF_KA + A_KV_WIDTH
N_IN = OFF_VA + A_KV_WIDTH

LANES = 128
SUBLANES = 8
HALO = SUBLANES
MLSTM_CHUNK = LANES
FF_CHUNK = 256
COL_BLOCK = 256
ROW_PART = 128
GATE_GROUP = SUBLANES
VMEM_LIMIT = 56 * 1024 * 1024


def _sigmoid(x):
    return 1.0 / (1.0 + jnp.exp(-x))


def _log_sigmoid(x):
    return jnp.minimum(x, 0.0) - jnp.log1p(jnp.exp(-jnp.abs(x)))


def _layer_norm(z, g, b):
    mu = jnp.mean(z, axis=-1, keepdims=True)
    zc = z - mu
    var = jnp.mean(zc * zc, axis=-1, keepdims=True)
    return zc * lax.rsqrt(var + LN_EPS) * g + b


def _params(n_grid, flags=None):
    return pltpu.CompilerParams(dimension_semantics=("arbitrary",) * n_grid,
                                vmem_limit_bytes=VMEM_LIMIT, flags=flags)


def _resident(shape):
    nd = len(shape)
    return pl.BlockSpec(shape, lambda *_: (0,) * nd, pipeline_mode=pl.Buffered(1))


def _mod_kernel(c_ref, w_ref, b_ref, o_ref):
    c = c_ref[...]
    s = (c * _sigmoid(c)).astype(BF16)
    o_ref[...] = jnp.dot(s, w_ref[...].astype(BF16), preferred_element_type=F32) + b_ref[...]


def _mod_call(cc, w_ada, b_ada):
    n = w_ada.shape[1]
    tn = D_MODEL
    return pl.pallas_call(
        _mod_kernel,
        grid=(n // tn,),
        in_specs=[pl.BlockSpec((SUBLANES, D_MODEL), lambda j: (0, 0)),
                  pl.BlockSpec((D_MODEL, tn), lambda j: (0, j)),
                  pl.BlockSpec((1, tn), lambda j: (0, j))],
        out_specs=pl.BlockSpec((SUBLANES, tn), lambda j: (0, j)),
        out_shape=jax.ShapeDtypeStruct((SUBLANES, n), F32),
        compiler_params=_params(1),
        name="mod",
    )(cc, w_ada, b_ada.reshape(1, n))


def _ffn_body(x, mod_ref, k0, w13_ref, w2_ref, g_ref, b_ref, h_ref, acc_ref):
    shift = mod_ref[0, k0:k0 + 1, :]
    scale = mod_ref[0, k0 + 1:k0 + 2, :]
    gate = mod_ref[0, k0 + 2:k0 + 3, :]
    h_ref[...] = (x * (1.0 + scale) + shift).astype(BF16)
    for j in range(D_FF // FF_CHUNK):
        lo = j * FF_CHUNK
        h = h_ref[...]
        a1 = jnp.dot(h, w13_ref[:, lo:lo + FF_CHUNK], preferred_element_type=F32)
        a3 = jnp.dot(h, w13_ref[:, D_FF + lo:D_FF + lo + FF_CHUNK], preferred_element_type=F32)
        g = (a1 * _sigmoid(a1) * a3).astype(BF16)
        y = jnp.dot(g, w2_ref[lo:lo + FF_CHUNK, :], preferred_element_type=F32)
        if j == 0:
            acc_ref[...] = y
        else:
            acc_ref[...] += y
    z = ALPHA * x + (FFN_RES * gate) * acc_ref[...]
    return _layer_norm(z, g_ref[...], b_ref[...])


def _ffn_kernel(x_ref, mod_ref, w13_ref, w2_ref, g_ref, b_ref, o_ref, h_ref, acc_ref, *, k0):
    o_ref[...] = _ffn_body(x_ref[...], mod_ref, k0, w13_ref, w2_ref, g_ref, b_ref, h_ref, acc_ref)


def _ffn_call(x, mod, mod_index, k0, w13, w2, ln_g, ln_b, tm):
    r = x.shape[0]
    return pl.pallas_call(
        functools.partial(_ffn_kernel, k0=k0),
        grid=(r // tm,),
        in_specs=[pl.BlockSpec((tm, D_MODEL), lambda i: (i, 0)),
                  pl.BlockSpec((1, N_MOD, D_MODEL), lambda i: (mod_index(i), 0, 0)),
                  _resident((D_MODEL, 2 * D_FF)),
                  _resident((D_FF, D_MODEL)),
                  _resident((1, D_MODEL)),
                  _resident((1, D_MODEL))],
        out_specs=pl.BlockSpec((tm, D_MODEL), lambda i: (i, 0)),
        out_shape=jax.ShapeDtypeStruct((r, D_MODEL), F32),
        scratch_shapes=[pltpu.VMEM((tm, D_MODEL), BF16), pltpu.VMEM((tm, D_MODEL), F32)],
        compiler_params=_params(1),
        name="ffn",
    )(x, mod, w13, w2, ln_g.reshape(1, D_MODEL), ln_b.reshape(1, D_MODEL))


N_WA = 2 * M_WIDTH + 2 * M_WIDTH
OFF_TG = A_WIDTH + A_KV_WIDTH + 2 * A_KV_WIDTH
N_WT = OFF_TG + LANES
ROW_A = (0, 3 * GATE_GROUP)
ROW_B = (GATE_GROUP, 4 * GATE_GROUP)
ROW_C = (2 * GATE_GROUP, 5 * GATE_GROUP)
GATE_ROWS = 6 * GATE_GROUP


def _scan(x, lane, op, reverse):
    fill = 0.0 if op is jnp.add else -jnp.inf
    k = 1
    while k < LANES:
        if reverse:
            other = jnp.where(lane < LANES - k, pltpu.roll(x, LANES - k, 1), fill)
        else:
            other = jnp.where(lane >= k, pltpu.roll(x, k, 1), fill)
        x = op(x, other)
        k *= 2
    return x


def _chunk_scan(x, lane, op, reverse):
    return jnp.concatenate([_scan(x[:, c * LANES:(c + 1) * LANES], lane, op, reverse)
                            for c in range(x.shape[1] // LANES)], axis=1)


def _rope(x, cos, sin_signed, first_half):
    swapped = jnp.where(first_half, pltpu.roll(x, LANES - A_DH // 2, 1), pltpu.roll(x, A_DH // 2, 1))
    return x * cos + swapped * sin_signed


def _inproj_kernel(*refs, tiles_per_seq, rope, tm):
    n_qk = 2 * M_WIDTH // COL_BLOCK
    raw_refs = refs[-n_qk:]
    refs = refs[:-n_qk]
    if rope:
        (xp_ref, x_ref, xn_ref, mod_ref, wa_ref, wt_ref, bg_ref, cw_ref, cb_ref, cos_ref, sin_ref,
         q_ref, kt_ref, v_ref, o_ref, grow_ref, qa_ref, kat_ref, va_ref, h_ref, pt_ref) = refs
    else:
        (xp_ref, x_ref, xn_ref, mod_ref, wa_ref, wt_ref, bg_ref, cw_ref, cb_ref,
         q_ref, kt_ref, v_ref, o_ref, grow_ref, qa_ref, kat_ref, va_ref, h_ref, pt_ref) = refs
    pos = pl.program_id(0) % tiles_per_seq
    slot0 = jnp.minimum(pl.program_id(0), 0)
    n_ext = tm + 2 * HALO
    shift = mod_ref[0, 3:4, :]
    scale1 = 1.0 + mod_ref[0, 4:5, :]
    h_ref[0:tm, :] = (x_ref[...] * scale1 + shift).astype(BF16)
    xh = jnp.concatenate([xp_ref[...], xn_ref[...]], axis=0)
    h_ref[tm:n_ext, :] = (xh * scale1 + shift).astype(BF16)

    def mm(w_ref, lo):
        return jnp.dot(h_ref[0:tm, :], w_ref[:, lo:lo + COL_BLOCK], preferred_element_type=F32)

    def qk_matmul(blk):
        cols = slice(blk * COL_BLOCK, (blk + 1) * COL_BLOCK)
        y = jnp.dot(h_ref[...], wa_ref[:, cols], preferred_element_type=F32)
        raw_refs[blk][0, 0:HALO, :] = jnp.where(pos == 0, 0.0, y[tm:tm + HALO])
        raw_refs[blk][0, HALO:HALO + tm, :] = y[0:tm]
        raw_refs[blk][0, HALO + tm:n_ext, :] = jnp.where(pos == tiles_per_seq - 1, 0.0, y[tm + HALO:n_ext])

    def qk_epilogue(blk, part):
        cols = slice(blk * COL_BLOCK, (blk + 1) * COL_BLOCK)
        r0 = part * ROW_PART
        base = HALO - CONV_W // 2 + r0
        acc = cb_ref[:, cols] + cw_ref[0:1, cols] * raw_refs[blk][slot0, pl.ds(base, ROW_PART), :]
        for j in range(1, CONV_W):
            acc = acc + cw_ref[j:j + 1, cols] * raw_refs[blk][slot0, pl.ds(base + j, ROW_PART), :]
        qk = acc * _sigmoid(acc)
        if blk < M_WIDTH // COL_BLOCK:
            q_ref[r0:r0 + ROW_PART, cols] = (qk * (M_DH ** -0.5)).astype(BF16)
        else:
            kt_ref[0, blk * COL_BLOCK - M_WIDTH:(blk + 1) * COL_BLOCK - M_WIDTH, r0:r0 + ROW_PART] = qk.T.astype(BF16)

    def v_matmul(blk):
        v_ref[:, blk * COL_BLOCK:(blk + 1) * COL_BLOCK] = mm(wa_ref, 2 * M_WIDTH + blk * COL_BLOCK).astype(BF16)

    def o_matmul(blk):
        o_ref[:, blk * COL_BLOCK:(blk + 1) * COL_BLOCK] = mm(wa_ref, 3 * M_WIDTH + blk * COL_BLOCK)

    def t_matmul(blk):
        pt_ref[:, blk * COL_BLOCK:(blk + 1) * COL_BLOCK] = mm(wt_ref, blk * COL_BLOCK)

    def pt_cols(lo, width):
        return pt_ref[:, lo:lo + width]

    def gate_epilogue():
        pgt = (pt_cols(OFF_TG, LANES) + bg_ref[...]).T
        li_f = pgt[0:GATE_GROUP]
        lf_f = _log_sigmoid(pgt[GATE_GROUP:2 * GATE_GROUP])
        li_b = pgt[2 * GATE_GROUP:3 * GATE_GROUP]
        lf_b = _log_sigmoid(pgt[3 * GATE_GROUP:4 * GATE_GROUP])
        lane = lax.broadcasted_iota(jnp.int32, (GATE_GROUP, LANES), 1)
        b_f = _chunk_scan(lf_f, lane, jnp.add, False)
        e_b = _chunk_scan(lf_b, lane, jnp.add, True)
        a_f = li_f - b_f
        a_b = li_b - e_b
        grow_ref[0] = jnp.concatenate([a_f, b_f, _chunk_scan(a_f, lane, jnp.maximum, False),
                                       a_b, e_b, _chunk_scan(a_b, lane, jnp.maximum, True)], axis=0)
        va_ref[:, A_KV_WIDTH:] = pt_cols(A_WIDTH + 2 * A_KV_WIDTH, A_KV_WIDTH).astype(BF16)

    def rotate(lo):
        xg = pt_cols(lo, LANES)
        if not rope:
            return xg
        lane_t = lax.broadcasted_iota(jnp.int32, (tm, LANES), 1)
        return _rope(xg, cos_ref[...], sin_ref[...], (lane_t % A_DH) < (A_DH // 2))

    def qa_epilogue(blk):
        for g in range(blk * COL_BLOCK // LANES, (blk + 1) * COL_BLOCK // LANES):
            qa_ref[:, g * LANES:(g + 1) * LANES] = (rotate(g * LANES) * (A_DH ** -0.5)).astype(BF16)

    def kv_epilogue():
        kat_ref[...] = rotate(A_WIDTH).T.astype(BF16)
        va_ref[:, :A_KV_WIDTH] = pt_cols(A_WIDTH + A_KV_WIDTH, A_KV_WIDTH).astype(BF16)

    n_part = tm // ROW_PART
    later = ([functools.partial(t_matmul, b) for b in range(N_WT // COL_BLOCK)]
             + [functools.partial(f, b) for b in range(M_WIDTH // COL_BLOCK) for f in (v_matmul, o_matmul)])
    qk_matmul(0)
    for blk in range(n_qk):
        if blk + 1 < n_qk:
            qk_matmul(blk + 1)
        else:
            later.pop(0)()
            later.pop(0)()
        for part in range(n_part):
            qk_epilogue(blk, part)
    for run in later:
        run()
    gate_epilogue()
    qa_epilogue(0)
    qa_epilogue(1)
    kv_epilogue()


def _inproj_call(x, mod, mod_index, wa, wt, bg, conv_w, conv_b, cos, sin, tm, seq_len):
    r = x.shape[0]
    tiles_per_seq = seq_len // tm
    rope = cos is not None
    hb = tm // HALO
    n_halo = r // HALO
    in_specs = [pl.BlockSpec((HALO, D_MODEL), lambda i: (jnp.maximum(i * hb - 1, 0), 0)),
                pl.BlockSpec((tm, D_MODEL), lambda i: (i, 0)),
                pl.BlockSpec((HALO, D_MODEL), lambda i: (jnp.minimum((i + 1) * hb, n_halo - 1), 0)),
                pl.BlockSpec((1, N_MOD, D_MODEL), lambda i: (mod_index(i), 0, 0)),
                _resident((D_MODEL, N_WA)),
                _resident((D_MODEL, N_WT)),
                _resident((1, LANES)),
                _resident((CONV_W, 2 * M_WIDTH)),
                _resident((1, 2 * M_WIDTH))]
    args = [x, x, x, mod, wa, wt, bg, conv_w, conv_b]
    if rope:
        in_specs += [pl.BlockSpec((tm, LANES), lambda i: (i % tiles_per_seq, 0)),
                     pl.BlockSpec((tm, LANES), lambda i: (i % tiles_per_seq, 0))]
        args += [cos, sin]
    n_seq = r // seq_len
    row = lambda w: pl.BlockSpec((tm, w), lambda i: (i, 0))
    col = lambda h: pl.BlockSpec((h, tm), lambda i: (0, i))
    seq_col = lambda h: pl.BlockSpec((1, h, tm), lambda i: (i // tiles_per_seq, 0, i % tiles_per_seq))
    out_specs = [row(M_WIDTH), seq_col(M_WIDTH), row(M_WIDTH), row(M_WIDTH), seq_col(GATE_ROWS),
                 row(A_WIDTH), col(A_KV_WIDTH), row(2 * A_KV_WIDTH)]
    out_shape = [jax.ShapeDtypeStruct((r, M_WIDTH), BF16),
                 jax.ShapeDtypeStruct((n_seq, M_WIDTH, seq_len), BF16),
                 jax.ShapeDtypeStruct((r, M_WIDTH), BF16),
                 jax.ShapeDtypeStruct((r, M_WIDTH), F32),
                 jax.ShapeDtypeStruct((n_seq, GATE_ROWS, seq_len), F32),
                 jax.ShapeDtypeStruct((r, A_WIDTH), BF16),
                 jax.ShapeDtypeStruct((A_KV_WIDTH, r), BF16),
                 jax.ShapeDtypeStruct((r, 2 * A_KV_WIDTH), BF16)]
    return pl.pallas_call(
        functools.partial(_inproj_kernel, tiles_per_seq=tiles_per_seq, rope=rope, tm=tm),
        grid=(r // tm,),
        in_specs=in_specs,
        out_specs=out_specs,
        out_shape=out_shape,
        scratch_shapes=([pltpu.VMEM((tm + 2 * HALO, D_MODEL), BF16), pltpu.VMEM((tm, N_WT), F32)]
                        + [pltpu.VMEM((1, tm + 2 * HALO, COL_BLOCK), F32)] * (2 * M_WIDTH // COL_BLOCK)),
        compiler_params=_params(1),
        name="inproj",
    )(*args)


M_AUG = 2 * M_DH
N_STREAMS = 2 * M_HEADS


def _col_replicated(row):
    return jnp.broadcast_to(row, (LANES, row.shape[1])).T


def _mlstm_kernel(*refs, emit_h):
    if emit_h:
        (qf_ref, ktf_ref, vf_ref, grf_ref, qb_ref, ktb_ref, vb_ref, grb_ref, c0_ref, m0_ref,
         hf_ref, hb_ref, cout_ref, mout_ref, c_ref, m_ref) = refs
        q_refs, h_refs = (qf_ref, qb_ref), (hf_ref, hb_ref)
    else:
        (ktf_ref, vf_ref, grf_ref, ktb_ref, vb_ref, grb_ref, c0_ref, m0_ref,
         cout_ref, mout_ref, c_ref, m_ref) = refs
    kt_refs, v_refs, gr_refs = (ktf_ref, ktb_ref), (vf_ref, vb_ref), (grf_ref, grb_ref)
    L = MLSTM_CHUNK
    n_batch = c_ref.shape[0] // N_STREAMS
    c = pl.program_id(0)

    @pl.when(c == 0)
    def _():
        c_ref[...] = c0_ref[...]
        m_ref[...] = m0_ref[...]

    t_idx = lax.broadcasted_iota(jnp.int32, (L, L), 0)
    s_idx = lax.broadcasted_iota(jnp.int32, (L, L), 1)
    masks = (s_idx <= t_idx, s_idx >= t_idx)
    ones = jnp.ones((L, M_DH), BF16)
    zeros_k = jnp.zeros((M_DH, L), BF16)
    m_all = m_ref[...]
    m_new = []
    for b in range(n_batch):
        for d in range(2):
            last = (L - 1, 0)[d]
            gr = gr_refs[d][b]
            for pair in range(M_HEADS // 2):
                if emit_h:
                    rows = slice(2 * pair * M_DH, (2 * pair + 2) * M_DH)
                    kt2 = kt_refs[d][b, rows, :]
                    rhs = jnp.concatenate(
                        [jnp.concatenate([kt2[:M_DH], zeros_k], axis=1),
                         jnp.concatenate([zeros_k, kt2[M_DH:]], axis=1)], axis=0)
                    s2 = jnp.dot(q_refs[d][b, :, rows], rhs, preferred_element_type=F32)
                for hh in range(2):
                    h = 2 * pair + hh
                    r = (b * 2 + d) * M_HEADS + h
                    sl = slice(h * M_DH, (h + 1) * M_DH)
                    a_row = gr[ROW_A[d] + h:ROW_A[d] + h + 1]
                    b_row = gr[ROW_B[d] + h:ROW_B[d] + h + 1]
                    m_row = m_all[r:r + 1]
                    ct = c_ref[r]
                    kt = kt_refs[d][b, sl, :]
                    v_aug = jnp.concatenate([v_refs[d][b, :, sl], ones], axis=1)
                    if emit_h:
                        mu_row = jnp.maximum(m_row, gr[ROW_C[d] + h:ROW_C[d] + h + 1])
                        mu = _col_replicated(mu_row)
                        mt = _col_replicated(b_row + mu_row)
                        dmat = jnp.where(masks[d], jnp.exp(a_row - mu), 0.0)
                        p = (s2[:, hh * L:(hh + 1) * L] * dmat).astype(BF16)
                        qs = (q_refs[d][b, :, sl].astype(F32) * jnp.exp(m_row - mu)).astype(BF16)
                        tot = jnp.dot(jnp.concatenate([p, qs], axis=1),
                                      jnp.concatenate([v_aug, ct.astype(BF16)], axis=0),
                                      preferred_element_type=F32)
                        h_refs[d][b, :, sl] = tot[:, :M_DH] / jnp.maximum(jnp.abs(tot[:, M_DH:]), jnp.exp(-mt))
                    mp = jnp.maximum(m_row, jnp.max(a_row, axis=1, keepdims=True))
                    ktw = (kt.astype(F32) * jnp.exp(a_row - mp)).astype(BF16)
                    decay = jnp.exp(m_row - mp)
                    c_ref[r] = (jnp.concatenate([decay, decay], axis=1) * ct
                                + jnp.dot(ktw, v_aug, preferred_element_type=F32))
                    m_new.append(b_row[:, last:last + 1] + mp)
    m_ref[...] = jnp.concatenate(m_new, axis=0)

    @pl.when(c == pl.num_programs(0) - 1)
    def _():
        cout_ref[...] = c_ref[...]
        mout_ref[...] = m_ref[...]


def _mlstm_call(q, kt, v, grow, c0, m0, n_batch, seq_len, emit_h):
    L = MLSTM_CHUNK
    nc = seq_len // L
    fwd = lambda c: c
    bwd = lambda c: nc - 1 - c
    row = lambda w, f: pl.BlockSpec((n_batch, L, w), lambda c: (0, f(c), 0))
    col = lambda hgt, f: pl.BlockSpec((n_batch, hgt, L), lambda c: (0, 0, f(c)))
    n_str = n_batch * N_STREAMS
    state_specs = [pl.BlockSpec((n_str, M_DH, M_AUG), lambda c: (0, 0, 0)),
                   pl.BlockSpec((n_str, LANES), lambda c: (0, 0))]
    state_shapes = [jax.ShapeDtypeStruct((n_str, M_DH, M_AUG), F32),
                    jax.ShapeDtypeStruct((n_str, LANES), F32)]
    if emit_h:
        in_specs = [row(M_WIDTH, fwd), col(M_WIDTH, fwd), row(M_WIDTH, fwd), col(GATE_ROWS, fwd),
                    row(M_WIDTH, bwd), col(M_WIDTH, bwd), row(M_WIDTH, bwd), col(GATE_ROWS, bwd)]
        args = [q, kt, v, grow, q, kt, v, grow]
        out_specs = [row(M_WIDTH, fwd), row(M_WIDTH, bwd)] + state_specs
        out_shape = [jax.ShapeDtypeStruct((n_batch, seq_len, M_WIDTH), F32)] * 2 + state_shapes
    else:
        in_specs = [col(M_WIDTH, fwd), row(M_WIDTH, fwd), col(GATE_ROWS, fwd),
                    col(M_WIDTH, bwd), row(M_WIDTH, bwd), col(GATE_ROWS, bwd)]
        args = [kt, v, grow, kt, v, grow]
        out_specs = state_specs
        out_shape = state_shapes
    return pl.pallas_call(
        functools.partial(_mlstm_kernel, emit_h=emit_h),
        grid=(nc,),
        in_specs=in_specs + state_specs,
        out_specs=out_specs,
        out_shape=out_shape,
        scratch_shapes=[pltpu.VMEM((n_str, M_DH, M_AUG), F32), pltpu.VMEM((n_str, LANES), F32)],
        compiler_params=_params(1),
        name="mlstm" if emit_h else "mlstm_ctx",
    )(*args, c0, m0)


def _attn_kernel(sink_ref, q_ref, ktp_ref, ktc_ref, ktn_ref, vp_ref, vc_ref, vn_ref, ktx_ref, vx_ref, o_ref,
                 *, n_ctx):
    j = pl.program_id(1)
    nb = pl.num_programs(1)
    T = A_BLOCK
    n_loc = 3 * T
    n_key = n_loc + n_ctx
    half = LANES // 2
    i_idx = lax.broadcasted_iota(jnp.int32, (T, T), 0)
    r_idx = lax.broadcasted_iota(jnp.int32, (T, T), 1)
    prev_ok = (r_idx >= i_idx) & (j > 0)
    next_ok = (r_idx <= i_idx) & (j < nb - 1)
    lane_v = lax.broadcasted_iota(jnp.int32, (1, LANES), 1)
    low = lane_v < half
    v_rows = (vp_ref, vc_ref, vn_ref, vx_ref)
    zk = jnp.zeros((A_DH, n_key), BF16)
    for k in range(A_KV_HEADS):
        ks = slice(k * A_DH, (k + 1) * A_DH)
        kt_all = jnp.concatenate([ktp_ref[ks, :], ktc_ref[ks, :], ktn_ref[ks, :], ktx_ref[ks, :]], axis=1)
        rhs = jnp.concatenate([jnp.concatenate([kt_all, zk], axis=1),
                               jnp.concatenate([zk, kt_all], axis=1)], axis=0)
        top_sl = slice(0, LANES) if k == 0 else slice(LANES, 2 * LANES)
        bot_sl = slice(LANES, 2 * LANES) if k == 0 else slice(0, LANES)
        v_top = jnp.concatenate([jnp.where(low, vr[:, top_sl], 0) for vr in v_rows], axis=0)
        v_bot = jnp.concatenate([jnp.where(low, 0, vr[:, bot_sl]) for vr in v_rows], axis=0)
        ones_top = jnp.broadcast_to(jnp.where(low, 1.0, 0.0).astype(BF16), (n_key, LANES))
        ones_bot = jnp.broadcast_to(jnp.where(low, 0.0, 1.0).astype(BF16), (n_key, LANES))
        v2 = jnp.concatenate([jnp.concatenate([v_top.astype(BF16), ones_top], axis=1),
                              jnp.concatenate([v_bot.astype(BF16), ones_bot], axis=1)], axis=0)
        for p in range(A_HEADS // A_KV_HEADS // 2):
            g = k * (A_HEADS // A_KV_HEADS // 2) + p
            q2 = q_ref[:, g * LANES:(g + 1) * LANES]
            s2 = jnp.dot(q2, rhs, preferred_element_type=F32)
            es = []
            sink_terms = []
            for u in range(2):
                base = u * n_key
                sink = sink_ref[2 * g + u]
                s_prev = jnp.where(prev_ok, s2[:, base:base + T], -jnp.inf)
                s_cur = s2[:, base + T:base + 2 * T]
                s_next = jnp.where(next_ok, s2[:, base + 2 * T:base + 3 * T], -jnp.inf)
                s_ctx = s2[:, base + n_loc:base + n_key]
                m = jnp.maximum(jnp.maximum(s_prev, s_cur), s_next)
                for cc in range(n_ctx // T):
                    m = jnp.maximum(m, s_ctx[:, cc * T:(cc + 1) * T])
                m = jnp.maximum(jnp.max(m, axis=1, keepdims=True), sink)
                es += [jnp.exp(s_prev - m), jnp.exp(s_cur - m), jnp.exp(s_next - m), jnp.exp(s_ctx - m)]
                sink_terms.append(jnp.exp(sink - m))
            p2 = jnp.concatenate(es, axis=1).astype(BF16)
            o = jnp.dot(p2, v2, preferred_element_type=F32)
            den = o[:, LANES:] + jnp.where(low, sink_terms[0], sink_terms[1])
            o_ref[:, g * LANES:(g + 1) * LANES] = (o[:, :LANES] / den).astype(BF16)


def _attn_call(sink, qa, kat, va, kat_c, va_c, n_batch, seq_len, n_ctx):
    T = A_BLOCK
    nb = seq_len // T
    r = n_batch * seq_len
    cur = lambda b, j: b * nb + j
    prv = lambda b, j: b * nb + jnp.maximum(j - 1, 0)
    nxt = lambda b, j: b * nb + jnp.minimum(j + 1, nb - 1)
    kspec = lambda f: pl.BlockSpec((A_KV_WIDTH, T), lambda b, j: (0, f(b, j)))
    vspec = lambda f: pl.BlockSpec((T, 2 * A_KV_WIDTH), lambda b, j: (f(b, j), 0))
    return pl.pallas_call(
        functools.partial(_attn_kernel, n_ctx=n_ctx),
        grid=(n_batch, nb),
        in_specs=[pl.BlockSpec(memory_space=pltpu.SMEM),
                  pl.BlockSpec((T, A_WIDTH), lambda b, j: (cur(b, j), 0)),
                  kspec(prv), kspec(cur), kspec(nxt), vspec(prv), vspec(cur), vspec(nxt),
                  pl.BlockSpec((A_KV_WIDTH, n_ctx), lambda b, j: (0, b)),
                  pl.BlockSpec((n_ctx, 2 * A_KV_WIDTH), lambda b, j: (b, 0))],
        out_specs=pl.BlockSpec((T, A_WIDTH), lambda b, j: (cur(b, j), 0)),
        out_shape=jax.ShapeDtypeStruct((r, A_WIDTH), BF16),
        compiler_params=_params(2),
        name="attn",
    )(sink, qa, kat, kat, kat, va, va, va, kat_c, va_c)


def _mix_ffn_kernel(hf_ref, hb_ref, op_ref, a_ref, x_ref, mod_ref, mg_ref, wo_ref, g1_ref, b1_ref,
                    w13_ref, w2_ref, g2_ref, b2_ref, o_ref, h_ref, acc_ref):
    h = hf_ref[...] + hb_ref[...]
    gate = _sigmoid(op_ref[...])
    parts = []
    for hd in range(M_HEADS):
        sl = slice(hd * M_DH, (hd + 1) * M_DH)
        seg = h[:, sl]
        mu = jnp.mean(seg, axis=-1, keepdims=True)
        sc = seg - mu
        var = jnp.mean(sc * sc, axis=-1, keepdims=True)
        parts.append((sc * lax.rsqrt(var + LN_EPS) * mg_ref[:, sl] * gate[:, sl]).astype(BF16))
    mixed = jnp.concatenate(parts + [a_ref[...]], axis=1)
    y = jnp.dot(mixed, wo_ref[...], preferred_element_type=F32)
    x2 = _layer_norm(ALPHA * x_ref[...] + mod_ref[0, 5:6, :] * y, g1_ref[...], b1_ref[...])
    o_ref[...] = _ffn_body(x2, mod_ref, 6, w13_ref, w2_ref, g2_ref, b2_ref, h_ref, acc_ref)


def _mix_ffn_call(hf, hb, o_pre, a_l, x1, mod, mod_index, mh_g, w_out, g1, b1, w13, w2, g2, b2, tm):
    r = x1.shape[0]
    row = lambda w: pl.BlockSpec((tm, w), lambda i: (i, 0))
    vec = lambda a: a.reshape(1, a.shape[-1])
    return pl.pallas_call(
        _mix_ffn_kernel,
        grid=(r // tm,),
        in_specs=[row(M_WIDTH), row(M_WIDTH), row(M_WIDTH), row(A_WIDTH), row(D_MODEL),
                  pl.BlockSpec((1, N_MOD, D_MODEL), lambda i: (mod_index(i), 0, 0)),
                  _resident((1, M_WIDTH)),
                  _resident((D_MODEL, D_MODEL)),
                  _resident((1, D_MODEL)),
                  _resident((1, D_MODEL)),
                  _resident((D_MODEL, 2 * D_FF)),
                  _resident((D_FF, D_MODEL)),
                  _resident((1, D_MODEL)),
                  _resident((1, D_MODEL))],
        out_specs=row(D_MODEL),
        out_shape=jax.ShapeDtypeStruct((r, D_MODEL), F32),
        scratch_shapes=[pltpu.VMEM((tm, D_MODEL), BF16), pltpu.VMEM((tm, D_MODEL), F32)],
        compiler_params=_params(1),
        name="mix_ffn",
    )(hf, hb, o_pre, a_l, x1, mod, vec(mh_g), w_out, vec(g1), vec(b1), w13, w2, vec(g2), vec(b2))


def _rope_tables(seq_len):
    t = jnp.arange(seq_len)
    row = (t // GRID_W).astype(F32)
    col = (t % GRID_W).astype(F32)
    n_freq = A_DH // 4
    inv = ROPE_BASE ** (-jnp.arange(n_freq, dtype=F32) / n_freq)
    ang = jnp.concatenate([row[:, None] * inv, col[:, None] * inv], -1)
    cos, sin = jnp.cos(ang), jnp.sin(ang)
    return jnp.tile(cos, (1, 4)), jnp.tile(jnp.concatenate([-sin, sin], -1), (1, 2))


def _spread_gates(a):
    lead = a.shape[:-1]
    a = a.reshape(lead + (4, M_HEADS))
    a = jnp.pad(a, [(0, 0)] * len(lead) + [(0, 0), (0, GATE_GROUP - M_HEADS)])
    a = a.reshape(lead + (4 * GATE_GROUP,))
    return jnp.pad(a, [(0, 0)] * len(lead) + [(0, LANES - 4 * GATE_GROUP)])


def kernel(x, c, ctx, c_ctx, w_ada, b_ada, ln_g, ln_b, ffn_w13, ffn_w2, w_in, b_gates, conv_w, conv_b,
           mh_norm_g, attn_sink, w_out):
    n_batch, seq_len, d = x.shape
    n_ctx = ctx.shape[1]
    l = 0
    tm = 512
    tm_ctx = n_ctx
    tiles_per_batch = seq_len // tm

    cc = jnp.concatenate([c, c_ctx[None, :], jnp.zeros((SUBLANES - n_batch - 1, d), F32)], axis=0)
    mod = _mod_call(cc, w_ada[l], b_ada[l])[:n_batch + 1].reshape(n_batch + 1, N_MOD, d)
    lat_mod = lambda i: i // tiles_per_batch
    ctx_mod = lambda i: n_batch

    w13_a, w13_b = ffn_w13[l, 0].astype(BF16), ffn_w13[l, 1].astype(BF16)
    w2_a, w2_b = ffn_w2[l, 0].astype(BF16), ffn_w2[l, 1].astype(BF16)
    wi = w_in[l]
    wa = wi[:, OFF_QM:OFF_G].astype(BF16)
    w_va = wi[:, OFF_VA:N_IN]
    wt = jnp.concatenate([wi[:, OFF_QA:OFF_VA], w_va, w_va[:, A_DH:], w_va[:, :A_DH],
                          _spread_gates(wi[:, OFF_G:OFF_QA])], axis=1).astype(BF16)
    bg = _spread_gates(b_gates[l]).reshape(1, LANES)
    cb = conv_b[l].reshape(1, 2 * M_WIDTH)
    cos, sin = _rope_tables(seq_len)

    x_lat = x.reshape(n_batch * seq_len, d)
    x_ctx = ctx.reshape(n_batch * n_ctx, d)

    x1 = _ffn_call(x_lat, mod, lat_mod, 0, w13_a, w2_a, ln_g[l, 0], ln_b[l, 0], tm)
    xc1 = _ffn_call(x_ctx, mod, ctx_mod, 0, w13_a, w2_a, ln_g[l, 0], ln_b[l, 0], tm_ctx)

    (_, kt_c, v_c, _, grow_c, _, kat_c, va_c) = _inproj_call(
        xc1, mod, ctx_mod, wa, wt, bg, conv_w[l], cb, None, None, tm_ctx, n_ctx)
    (q_l, kt_l, v_l, o_pre, grow_l, qa_l, kat_l, va_l) = _inproj_call(
        x1, mod, lat_mod, wa, wt, bg, conv_w[l], cb, cos, sin, tm, seq_len)

    c0 = jnp.zeros((n_batch * N_STREAMS, M_DH, M_AUG), F32)
    m0 = jnp.zeros((n_batch * N_STREAMS, LANES), F32)
    c_st, m_st = _mlstm_call(None, kt_c, v_c.reshape(n_batch, n_ctx, M_WIDTH), grow_c, c0, m0,
                             n_batch, n_ctx, False)
    hf, hb, _, _ = _mlstm_call(q_l.reshape(n_batch, seq_len, M_WIDTH), kt_l,
                               v_l.reshape(n_batch, seq_len, M_WIDTH), grow_l, c_st, m_st, n_batch, seq_len, True)
    hf = hf.reshape(n_batch * seq_len, M_WIDTH)
    hb = hb.reshape(n_batch * seq_len, M_WIDTH)

    a_l = _attn_call(attn_sink[l], qa_l, kat_l, va_l, kat_c, va_c, n_batch, seq_len, n_ctx)

    out = _mix_ffn_call(hf, hb, o_pre, a_l, x1, mod, lat_mod, mh_norm_g[l], w_out[l].astype(BF16),
                        ln_g[l, 1], ln_b[l, 1], w13_b, w2_b, ln_g[l, 2], ln_b[l, 2], tm)
    return out.reshape(n_batch, seq_len, d)
```

```python
import functools

import jax
import jax.numpy as jnp
import numpy as np
from jax import lax
from jax.experimental import pallas as pl
from jax.experimental.pallas import tpu as pltpu

F32 = jnp.float32
BF16 = jnp.bfloat16

D_MODEL = 1024
GRID_W = 64
M_HEADS = 4
M_DH = 128
M_WIDTH = M_HEADS * M_DH
A_HEADS = 8
A_KV_HEADS = 2
A_DH = 64
A_WIDTH = A_HEADS * A_DH
A_KV_WIDTH = A_KV_HEADS * A_DH
A_BLOCK = 128
CONV_W = 5
D_FF = 2816
ROPE_BASE = 10000.0
N_MOD = 9
LN_EPS = 1e-5
FFN_RES = 0.5
DEPTH = 1
ALPHA = (2.0 * DEPTH) ** 0.25

OFF_QM = 0
OFF_KM = OFF_QM + M_WIDTH
OFF_VM = OFF_KM + M_WIDTH
OFF_OM = OFF_VM + M_WIDTH
OFF_G = OFF_OM + M_WIDTH
N_GATES = 4 * M_HEADS
OFF_QA = OFF_G + N_GATES
OFF_KA = OFF_QA + A_WIDTH
OFF_VA = OFF_KA + A_KV_WIDTH
N_IN = OFF_VA + A_KV_WIDTH

LANES = 128
SUBLANES = 8
HALO = SUBLANES
MLSTM_CHUNK = LANES
MLSTM_STEP_CHUNKS = 4
FF_CHUNK = 256
COL_BLOCK = 256
ROW_PART = 128
MOD_COLS = 2304
GATE_GROUP = SUBLANES
VMEM_LIMIT = 56 * 1024 * 1024


def _sigmoid(x):
    return 1.0 / (1.0 + jnp.exp(-x))


def _log_sigmoid(x):
    return jnp.minimum(x, 0.0) - jnp.log1p(jnp.exp(-jnp.abs(x)))


def _layer_norm(z, g, b):
    mu = jnp.mean(z, axis=-1, keepdims=True)
    zc = z - mu
    var = jnp.mean(zc * zc, axis=-1, keepdims=True)
    return zc * lax.rsqrt(var + LN_EPS) * g + b


def _params(n_grid, flags=None):
    return pltpu.CompilerParams(dimension_semantics=("arbitrary",) * n_grid,
                                vmem_limit_bytes=VMEM_LIMIT, flags=flags)


def _resident(shape):
    nd = len(shape)
    return pl.BlockSpec(shape, lambda *_: (0,) * nd, pipeline_mode=pl.Buffered(1))


def _mod_kernel(c_ref, w_ref, b_ref, o_ref):
    c = c_ref[...]
    s = (c * _sigmoid(c)).astype(BF16)
    o_ref[...] = jnp.dot(s, w_ref[...].astype(BF16), preferred_element_type=F32) + b_ref[...]


def _mod_call(cc, w_ada, b_ada):
    n = w_ada.shape[1]
    tn = MOD_COLS
    return pl.pallas_call(
        _mod_kernel,
        grid=(n // tn,),
        in_specs=[pl.BlockSpec((SUBLANES, D_MODEL), lambda j: (0, 0)),
                  pl.BlockSpec((D_MODEL, tn), lambda j: (0, j)),
                  pl.BlockSpec((1, tn), lambda j: (0, j))],
        out_specs=pl.BlockSpec((SUBLANES, tn), lambda j: (0, j)),
        out_shape=jax.ShapeDtypeStruct((SUBLANES, n), F32),
        compiler_params=_params(1),
        name="mod",
    )(cc, w_ada, b_ada.reshape(1, n))


def _ffn_body(x, mod_ref, k0, w13_ref, w2_ref, g_ref, b_ref, h_ref, acc_ref):
    shift = mod_ref[0, k0:k0 + 1, :]
    scale = mod_ref[0, k0 + 1:k0 + 2, :]
    gate = mod_ref[0, k0 + 2:k0 + 3, :]
    h_ref[...] = (x * (1.0 + scale) + shift).astype(BF16)
    for j in range(D_FF // FF_CHUNK):
        lo = j * FF_CHUNK
        h = h_ref[...]
        a1 = jnp.dot(h, w13_ref[:, lo:lo + FF_CHUNK], preferred_element_type=F32)
        a3 = jnp.dot(h, w13_ref[:, D_FF + lo:D_FF + lo + FF_CHUNK], preferred_element_type=F32)
        g = (a1 * _sigmoid(a1) * a3).astype(BF16)
        y = jnp.dot(g, w2_ref[lo:lo + FF_CHUNK, :], preferred_element_type=F32)
        if j == 0:
            acc_ref[...] = y
        else:
            acc_ref[...] += y
    z = ALPHA * x + (FFN_RES * gate) * acc_ref[...]
    return _layer_norm(z, g_ref[...], b_ref[...])


W13_STAGE_COLS = 512
W2_STAGE_ROWS = 256


def _stage_ffn_weights(w13_hbm, w2_hbm, w13_ref, w2_ref, st13_ref, st2_ref, sem, layer, which):
    chunks = []
    for c in range(2 * D_FF // W13_STAGE_COLS):
        lo = c * W13_STAGE_COLS
        chunks.append((w13_hbm.at[layer, which, :, lo:lo + W13_STAGE_COLS], st13_ref, w13_ref,
                       (slice(None), slice(lo, lo + W13_STAGE_COLS))))
    for c in range(D_FF // W2_STAGE_ROWS):
        lo = c * W2_STAGE_ROWS
        chunks.append((w2_hbm.at[layer, which, lo:lo + W2_STAGE_ROWS, :], st2_ref, w2_ref,
                       (slice(lo, lo + W2_STAGE_ROWS), slice(None))))

    def copy(n):
        src, stage, _, _ = chunks[n]
        return pltpu.make_async_copy(src, stage.at[n % 2], sem.at[n % 2])

    copy(0).start()
    for n, (_, stage, dst, idx) in enumerate(chunks):
        if n + 1 < len(chunks):
            copy(n + 1).start()
        copy(n).wait()
        dst[idx] = stage[n % 2].astype(BF16)


def _ffn_kernel(x_ref, xc_ref, mod_ref, w13_hbm, w2_hbm, g_ref, b_ref, o_ref, oc_ref,
                w13_ref, w2_ref, st13_ref, st2_ref, sem, h_ref, acc_ref, *, layer, which):
    i = pl.program_id(0)
    n_lat = pl.num_programs(0) - 1

    @pl.when(i == 0)
    def _():
        _stage_ffn_weights(w13_hbm, w2_hbm, w13_ref, w2_ref, st13_ref, st2_ref, sem, layer, which)

    x = jnp.where(i < n_lat, x_ref[...], xc_ref[...])
    y = _ffn_body(x, mod_ref, 0, w13_ref, w2_ref, g_ref, b_ref, h_ref, acc_ref)

    @pl.when(i < n_lat)
    def _():
        o_ref[...] = y

    @pl.when(i == n_lat)
    def _():
        oc_ref[...] = y


def _ffn_scratch(tm):
    return [pltpu.VMEM((D_MODEL, 2 * D_FF), BF16), pltpu.VMEM((D_FF, D_MODEL), BF16),
            pltpu.VMEM((2, D_MODEL, W13_STAGE_COLS), F32), pltpu.VMEM((2, W2_STAGE_ROWS, D_MODEL), F32),
            pltpu.SemaphoreType.DMA((2,)),
            pltpu.VMEM((tm, D_MODEL), BF16), pltpu.VMEM((tm, D_MODEL), F32)]


def _ffn_call(x, x_ctx, mod, tiles_per_batch, ffn_w13, ffn_w2, layer, which, ln_g, ln_b, tm):
    r = x.shape[0]
    n_lat = r // tm
    assert x_ctx.shape[0] == tm
    lat = lambda i: jnp.minimum(i, n_lat - 1)
    return pl.pallas_call(
        functools.partial(_ffn_kernel, layer=layer, which=which),
        grid=(n_lat + 1,),
        in_specs=[pl.BlockSpec((tm, D_MODEL), lambda i: (lat(i), 0)),
                  pl.BlockSpec((tm, D_MODEL), lambda i: (0, 0)),
                  pl.BlockSpec((1, N_MOD, D_MODEL), lambda i: (i // tiles_per_batch, 0, 0)),
                  pl.BlockSpec(memory_space=pl.ANY),
                  pl.BlockSpec(memory_space=pl.ANY),
                  _resident((1, D_MODEL)),
                  _resident((1, D_MODEL))],
        out_specs=[pl.BlockSpec((tm, D_MODEL), lambda i: (lat(i), 0)),
                   pl.BlockSpec((tm, D_MODEL), lambda i: (0, 0))],
        out_shape=[jax.ShapeDtypeStruct((r, D_MODEL), F32), jax.ShapeDtypeStruct((tm, D_MODEL), F32)],
        scratch_shapes=_ffn_scratch(tm),
        compiler_params=_params(1),
        name="ffn",
    )(x, x_ctx, mod, ffn_w13, ffn_w2, ln_g.reshape(1, D_MODEL), ln_b.reshape(1, D_MODEL))


N_WA = 2 * M_WIDTH + 2 * M_WIDTH
OFF_TG = A_WIDTH + A_KV_WIDTH + 2 * A_KV_WIDTH
N_WT = OFF_TG + LANES
ROW_A = (0, 3 * GATE_GROUP)
ROW_B = (GATE_GROUP, 4 * GATE_GROUP)
ROW_C = (2 * GATE_GROUP, 5 * GATE_GROUP)
GATE_ROWS = 6 * GATE_GROUP


def _scan(x, lane, op, reverse):
    fill = 0.0 if op is jnp.add else -jnp.inf
    k = 1
    while k < LANES:
        if reverse:
            other = jnp.where(lane < LANES - k, pltpu.roll(x, LANES - k, 1), fill)
        else:
            other = jnp.where(lane >= k, pltpu.roll(x, k, 1), fill)
        x = op(x, other)
        k *= 2
    return x


def _chunk_scan(x, lane, op, reverse):
    return jnp.concatenate([_scan(x[:, c * LANES:(c + 1) * LANES], lane, op, reverse)
                            for c in range(x.shape[1] // LANES)], axis=1)


def _rope(x, cos, sin_signed, first_half):
    swapped = jnp.where(first_half, pltpu.roll(x, LANES - A_DH // 2, 1), pltpu.roll(x, A_DH // 2, 1))
    return x * cos + swapped * sin_signed


def _inproj_kernel(*refs, tiles_per_seq, rope, tm):
    n_qk = 2 * M_WIDTH // COL_BLOCK
    raw_refs = refs[-n_qk:]
    refs = refs[:-n_qk]
    if rope:
        (xp_ref, x_ref, xn_ref, mod_ref, wa_ref, wt_ref, bg_ref, cw_ref, cb_ref, cos_ref, sin_ref,
         q_ref, kt_ref, v_ref, o_ref, grow_ref, qa_ref, kat_ref, va_ref, h_ref, pt_ref) = refs
    else:
        (xp_ref, x_ref, xn_ref, mod_ref, wa_ref, wt_ref, bg_ref, cw_ref, cb_ref,
         q_ref, kt_ref, v_ref, o_ref, grow_ref, qa_ref, kat_ref, va_ref, h_ref, pt_ref) = refs
    pos = pl.program_id(0) % tiles_per_seq
    slot0 = jnp.minimum(pl.program_id(0), 0)
    n_ext = tm + 2 * HALO
    shift = mod_ref[0, 3:4, :]
    scale1 = 1.0 + mod_ref[0, 4:5, :]
    h_ref[0:tm, :] = (x_ref[...] * scale1 + shift).astype(BF16)
    xh = jnp.concatenate([xp_ref[...], xn_ref[...]], axis=0)
    h_ref[tm:n_ext, :] = (xh * scale1 + shift).astype(BF16)

    def mm(w_ref, lo):
        return jnp.dot(h_ref[0:tm, :], w_ref[:, lo:lo + COL_BLOCK], preferred_element_type=F32)

    def qk_matmul(blk):
        cols = slice(blk * COL_BLOCK, (blk + 1) * COL_BLOCK)
        y = jnp.dot(h_ref[...], wa_ref[:, cols], preferred_element_type=F32)
        raw_refs[blk][0, 0:HALO, :] = jnp.where(pos == 0, 0.0, y[tm:tm + HALO])
        raw_refs[blk][0, HALO:HALO + tm, :] = y[0:tm]
        raw_refs[blk][0, HALO + tm:n_ext, :] = jnp.where(pos == tiles_per_seq - 1, 0.0, y[tm + HALO:n_ext])

    def qk_epilogue(blk, part):
        cols = slice(blk * COL_BLOCK, (blk + 1) * COL_BLOCK)
        r0 = part * ROW_PART
        base = HALO - CONV_W // 2 + r0
        acc = cb_ref[:, cols] + cw_ref[0:1, cols] * raw_refs[blk][slot0, pl.ds(base, ROW_PART), :]
        for j in range(1, CONV_W):
            acc = acc + cw_ref[j:j + 1, cols] * raw_refs[blk][slot0, pl.ds(base + j, ROW_PART), :]
        qk = acc * _sigmoid(acc)
        if blk < M_WIDTH // COL_BLOCK:
            q_ref[r0:r0 + ROW_PART, cols] = (qk * (M_DH ** -0.5)).astype(BF16)
        else:
            kt_ref[0, part, blk * COL_BLOCK - M_WIDTH:(blk + 1) * COL_BLOCK - M_WIDTH, :] = qk.T.astype(BF16)

    def v_matmul(blk):
        v_ref[:, blk * COL_BLOCK:(blk + 1) * COL_BLOCK] = mm(wa_ref, 2 * M_WIDTH + blk * COL_BLOCK).astype(BF16)

    def o_matmul(blk):
        o_ref[:, blk * COL_BLOCK:(blk + 1) * COL_BLOCK] = mm(wa_ref, 3 * M_WIDTH + blk * COL_BLOCK)

    def t_matmul(blk):
        pt_ref[:, blk * COL_BLOCK:(blk + 1) * COL_BLOCK] = mm(wt_ref, blk * COL_BLOCK)

    def pt_cols(lo, width):
        return pt_ref[:, lo:lo + width]

    def gate_epilogue():
        pgt = (pt_cols(OFF_TG, LANES) + bg_ref[...]).T
        li_f = pgt[0:GATE_GROUP]
        lf_f = _log_sigmoid(pgt[GATE_GROUP:2 * GATE_GROUP])
        li_b = pgt[2 * GATE_GROUP:3 * GATE_GROUP]
        lf_b = _log_sigmoid(pgt[3 * GATE_GROUP:4 * GATE_GROUP])
        lane = lax.broadcasted_iota(jnp.int32, (GATE_GROUP, LANES), 1)
        b_f = _chunk_scan(lf_f, lane, jnp.add, False)
        e_b = _chunk_scan(lf_b, lane, jnp.add, True)
        a_f = li_f - b_f
        a_b = li_b - e_b
        rows = jnp.concatenate([a_f, b_f, _chunk_scan(a_f, lane, jnp.maximum, False),
                                a_b, e_b, _chunk_scan(a_b, lane, jnp.maximum, True)], axis=0)
        for c in range(tm // MLSTM_CHUNK):
            grow_ref[0, c] = rows[:, c * MLSTM_CHUNK:(c + 1) * MLSTM_CHUNK]
        va_ref[:, A_KV_WIDTH:] = pt_cols(A_WIDTH + 2 * A_KV_WIDTH, A_KV_WIDTH).astype(BF16)

    def rotate(lo):
        xg = pt_cols(lo, LANES)
        if not rope:
            return xg
        lane_t = lax.broadcasted_iota(jnp.int32, (tm, LANES), 1)
        return _rope(xg, cos_ref[...], sin_ref[...], (lane_t % A_DH) < (A_DH // 2))

    def qa_epilogue(blk):
        for g in range(blk * COL_BLOCK // LANES, (blk + 1) * COL_BLOCK // LANES):
            qa_ref[:, g * LANES:(g + 1) * LANES] = (rotate(g * LANES) * (A_DH ** -0.5)).astype(BF16)

    def kv_epilogue():
        kat = rotate(A_WIDTH).T.astype(BF16)
        for c in range(tm // A_BLOCK):
            kat_ref[c] = kat[:, c * A_BLOCK:(c + 1) * A_BLOCK]
        va_ref[:, :A_KV_WIDTH] = pt_cols(A_WIDTH + A_KV_WIDTH, A_KV_WIDTH).astype(BF16)

    n_part = tm // ROW_PART
    later = ([functools.partial(t_matmul, b) for b in range(N_WT // COL_BLOCK)]
             + [functools.partial(f, b) for b in range(M_WIDTH // COL_BLOCK) for f in (v_matmul, o_matmul)])
    qk_matmul(0)
    for blk in range(n_qk):
        if blk + 1 < n_qk:
            qk_matmul(blk + 1)
        else:
            later.pop(0)()
            later.pop(0)()
        for part in range(n_part):
            qk_epilogue(blk, part)
    for run in later:
        run()
    gate_epilogue()
    qa_epilogue(0)
    qa_epilogue(1)
    kv_epilogue()


def _inproj_call(x, mod, mod_index, wa, wt, bg, conv_w, conv_b, cos, sin, tm, seq_len):
    r = x.shape[0]
    tiles_per_seq = seq_len // tm
    rope = cos is not None
    hb = tm // HALO
    n_halo = r // HALO
    in_specs = [pl.BlockSpec((HALO, D_MODEL), lambda i: (jnp.maximum(i * hb - 1, 0), 0)),
                pl.BlockSpec((tm, D_MODEL), lambda i: (i, 0)),
                pl.BlockSpec((HALO, D_MODEL), lambda i: (jnp.minimum((i + 1) * hb, n_halo - 1), 0)),
                pl.BlockSpec((1, N_MOD, D_MODEL), lambda i: (mod_index(i), 0, 0)),
                _resident((D_MODEL, N_WA)),
                _resident((D_MODEL, N_WT)),
                _resident((1, LANES)),
                _resident((CONV_W, 2 * M_WIDTH)),
                _resident((1, 2 * M_WIDTH))]
    args = [x, x, x, mod, wa, wt, bg, conv_w, conv_b]
    if rope:
        in_specs += [pl.BlockSpec((tm, LANES), lambda i: (i % tiles_per_seq, 0)),
                     pl.BlockSpec((tm, LANES), lambda i: (i % tiles_per_seq, 0))]
        args += [cos, sin]
    assert MLSTM_CHUNK == A_BLOCK == ROW_PART
    n_seq = r // seq_len
    cps = seq_len // MLSTM_CHUNK
    cpt = tm // MLSTM_CHUNK
    row = lambda w: pl.BlockSpec((tm, w), lambda i: (i, 0))
    seq_col = lambda h: pl.BlockSpec((1, cpt, h, MLSTM_CHUNK),
                                     lambda i: (i // tiles_per_seq, i % tiles_per_seq, 0, 0))
    out_specs = [row(M_WIDTH), seq_col(M_WIDTH), row(M_WIDTH), row(M_WIDTH), seq_col(GATE_ROWS),
                 row(A_WIDTH), pl.BlockSpec((cpt, A_KV_WIDTH, A_BLOCK), lambda i: (i, 0, 0)),
                 row(2 * A_KV_WIDTH)]
    out_shape = [jax.ShapeDtypeStruct((r, M_WIDTH), BF16),
                 jax.ShapeDtypeStruct((n_seq, cps, M_WIDTH, MLSTM_CHUNK), BF16),
                 jax.ShapeDtypeStruct((r, M_WIDTH), BF16),
                 jax.ShapeDtypeStruct((r, M_WIDTH), F32),
                 jax.ShapeDtypeStruct((n_seq, cps, GATE_ROWS, MLSTM_CHUNK), F32),
                 jax.ShapeDtypeStruct((r, A_WIDTH), BF16),
                 jax.ShapeDtypeStruct((r // A_BLOCK, A_KV_WIDTH, A_BLOCK), BF16),
                 jax.ShapeDtypeStruct((r, 2 * A_KV_WIDTH), BF16)]
    return pl.pallas_call(
        functools.partial(_inproj_kernel, tiles_per_seq=tiles_per_seq, rope=rope, tm=tm),
        grid=(r // tm,),
        in_specs=in_specs,
        out_specs=out_specs,
        out_shape=out_shape,
        scratch_shapes=([pltpu.VMEM((tm + 2 * HALO, D_MODEL), BF16), pltpu.VMEM((tm, N_WT), F32)]
                        + [pltpu.VMEM((1, tm + 2 * HALO, COL_BLOCK), F32)] * (2 * M_WIDTH // COL_BLOCK)),
        compiler_params=_params(1),
        name="inproj",
    )(*args)


M_AUG = 2 * M_DH
N_STREAMS = 2 * M_HEADS


def _col_replicated(row):
    return jnp.broadcast_to(row, (LANES, row.shape[1])).T


def _mlstm_kernel(*refs, emit_h):
    if emit_h:
        (qf_ref, ktf_ref, vf_ref, grf_ref, qb_ref, ktb_ref, vb_ref, grb_ref, c0_ref, m0_ref,
         hf_ref, hb_ref, cout_ref, mout_ref, c_ref, m_ref) = refs
        q_refs, h_refs = (qf_ref, qb_ref), (hf_ref, hb_ref)
    else:
        (ktf_ref, vf_ref, grf_ref, ktb_ref, vb_ref, grb_ref, c0_ref, m0_ref,
         cout_ref, mout_ref, c_ref, m_ref) = refs
    kt_refs, v_refs, gr_refs = (ktf_ref, ktb_ref), (vf_ref, vb_ref), (grf_ref, grb_ref)
    L = MLSTM_CHUNK
    n_batch = c_ref.shape[0] // N_STREAMS
    n_sub = grf_ref.shape[1]
    c = pl.program_id(0)

    @pl.when(c == 0)
    def _():
        c_ref[...] = c0_ref[...]
        m_ref[...] = m0_ref[...]

    t_idx = lax.broadcasted_iota(jnp.int32, (L, L), 0)
    s_idx = lax.broadcasted_iota(jnp.int32, (L, L), 1)
    masks = (s_idx <= t_idx, s_idx >= t_idx)
    ones = jnp.ones((L, M_DH), BF16)
    zeros_k = jnp.zeros((M_DH, L), BF16)
    m_all = m_ref[...]
    m_cur = [m_all[r:r + 1] for r in range(n_batch * N_STREAMS)]
    for sub in range(n_sub):
        for b in range(n_batch):
            for d in range(2):
                last = (L - 1, 0)[d]
                ch = sub if d == 0 else n_sub - 1 - sub
                tok = slice(ch * L, (ch + 1) * L)
                gr = gr_refs[d][b, ch]
                for pair in range(M_HEADS // 2):
                    if emit_h:
                        rows = slice(2 * pair * M_DH, (2 * pair + 2) * M_DH)
                        kt2 = kt_refs[d][b, ch, rows, :]
                        rhs = jnp.concatenate(
                            [jnp.concatenate([kt2[:M_DH], zeros_k], axis=1),
                             jnp.concatenate([zeros_k, kt2[M_DH:]], axis=1)], axis=0)
                        s2 = jnp.dot(q_refs[d][b, tok, rows], rhs, preferred_element_type=F32)
                    for hh in range(2):
                        h = 2 * pair + hh
                        r = (b * 2 + d) * M_HEADS + h
                        sl = slice(h * M_DH, (h + 1) * M_DH)
                        a_row = gr[ROW_A[d] + h:ROW_A[d] + h + 1]
                        b_row = gr[ROW_B[d] + h:ROW_B[d] + h + 1]
                        m_row = m_cur[r]
                        ct = c_ref[r]
                        kt = kt_refs[d][b, ch, sl, :]
                        v_aug = jnp.concatenate([v_refs[d][b, tok, sl], ones], axis=1)
                        if emit_h:
                            mu_row = jnp.maximum(m_row, gr[ROW_C[d] + h:ROW_C[d] + h + 1])
                            mu = _col_replicated(mu_row)
                            mt = _col_replicated(b_row + mu_row)
                            dmat = jnp.where(masks[d], jnp.exp(a_row - mu), 0.0)
                            p = (s2[:, hh * L:(hh + 1) * L] * dmat).astype(BF16)
                            qs = (q_refs[d][b, tok, sl].astype(F32) * jnp.exp(m_row - mu)).astype(BF16)
                            tot = jnp.dot(jnp.concatenate([p, qs], axis=1),
                                          jnp.concatenate([v_aug, ct.astype(BF16)], axis=0),
                                          preferred_element_type=F32)
                            h_refs[d][b, tok, sl] = (tot[:, :M_DH]
                                                     / jnp.maximum(jnp.abs(tot[:, M_DH:]), jnp.exp(-mt)))
                        mp = jnp.maximum(m_row, jnp.max(a_row, axis=1, keepdims=True))
                        ktw = (kt.astype(F32) * jnp.exp(a_row - mp)).astype(BF16)
                        decay = jnp.exp(m_row - mp)
                        c_ref[r] = (jnp.concatenate([decay, decay], axis=1) * ct
                                    + jnp.dot(ktw, v_aug, preferred_element_type=F32))
                        m_cur[r] = b_row[:, last:last + 1] + mp
    m_ref[...] = jnp.concatenate(m_cur, axis=0)

    @pl.when(c == pl.num_programs(0) - 1)
    def _():
        cout_ref[...] = c_ref[...]
        mout_ref[...] = m_ref[...]


def _mlstm_call(q, kt, v, grow, c0, m0, n_batch, seq_len, emit_h):
    L = MLSTM_CHUNK
    n_sub = min(MLSTM_STEP_CHUNKS, seq_len // L)
    nc = seq_len // (L * n_sub)
    fwd = lambda c: c
    bwd = lambda c: nc - 1 - c
    row = lambda w, f: pl.BlockSpec((n_batch, n_sub * L, w), lambda c: (0, f(c), 0))
    col = lambda hgt, f: pl.BlockSpec((n_batch, n_sub, hgt, L), lambda c: (0, f(c), 0, 0))
    n_str = n_batch * N_STREAMS
    state_specs = [pl.BlockSpec((n_str, M_DH, M_AUG), lambda c: (0, 0, 0)),
                   pl.BlockSpec((n_str, LANES), lambda c: (0, 0))]
    state_shapes = [jax.ShapeDtypeStruct((n_str, M_DH, M_AUG), F32),
                    jax.ShapeDtypeStruct((n_str, LANES), F32)]
    if emit_h:
        in_specs = [row(M_WIDTH, fwd), col(M_WIDTH, fwd), row(M_WIDTH, fwd), col(GATE_ROWS, fwd),
                    row(M_WIDTH, bwd), col(M_WIDTH, bwd), row(M_WIDTH, bwd), col(GATE_ROWS, bwd)]
        args = [q, kt, v, grow, q, kt, v, grow]
        out_specs = [row(M_WIDTH, fwd), row(M_WIDTH, bwd)] + state_specs
        out_shape = [jax.ShapeDtypeStruct((n_batch, seq_len, M_WIDTH), F32)] * 2 + state_shapes
    else:
        in_specs = [col(M_WIDTH, fwd), row(M_WIDTH, fwd), col(GATE_ROWS, fwd),
                    col(M_WIDTH, bwd), row(M_WIDTH, bwd), col(GATE_ROWS, bwd)]
        args = [kt, v, grow, kt, v, grow]
        out_specs = state_specs
        out_shape = state_shapes
    return pl.pallas_call(
        functools.partial(_mlstm_kernel, emit_h=emit_h),
        grid=(nc,),
        in_specs=in_specs + state_specs,
        out_specs=out_specs,
        out_shape=out_shape,
        scratch_shapes=[pltpu.VMEM((n_str, M_DH, M_AUG), F32), pltpu.VMEM((n_str, LANES), F32)],
        compiler_params=_params(1),
        name="mlstm" if emit_h else "mlstm_ctx",
    )(*args, c0, m0)


ATT_Q_BLOCKS = 8


def _attn_kernel(sink_ref, q_ref, ktp_ref, ktm_ref, ktn_ref, vp_ref, vm_ref, vn_ref, ktx_ref, vx_ref, o_ref,
                 *, n_ctx):
    j = pl.program_id(1)
    nj = pl.num_programs(1)
    T = A_BLOCK
    NQ = ATT_Q_BLOCKS
    n_loc = 3 * T
    n_key = n_loc + n_ctx
    half = LANES // 2
    i_idx = lax.broadcasted_iota(jnp.int32, (T, T), 0)
    r_idx = lax.broadcasted_iota(jnp.int32, (T, T), 1)
    lane_v = lax.broadcasted_iota(jnp.int32, (1, LANES), 1)
    low = lane_v < half
    zk = jnp.zeros((A_DH, n_key), BF16)
    ones_top = jnp.broadcast_to(jnp.where(low, 1.0, 0.0).astype(BF16), (n_key, LANES))
    ones_bot = jnp.broadcast_to(jnp.where(low, 0.0, 1.0).astype(BF16), (n_key, LANES))
    for t in range(NQ):
        rows = slice(t * T, (t + 1) * T)
        prev_ok = (r_idx >= i_idx) & (j > 0) if t == 0 else r_idx >= i_idx
        next_ok = (r_idx <= i_idx) & (j < nj - 1) if t == NQ - 1 else r_idx <= i_idx
        for k in range(A_KV_HEADS):
            ks = slice(k * A_DH, (k + 1) * A_DH)
            kt_prev = ktp_ref[0, ks, :] if t == 0 else ktm_ref[t - 1, ks, :]
            kt_next = ktn_ref[0, ks, :] if t == NQ - 1 else ktm_ref[t + 1, ks, :]
            v_prev = vp_ref[...] if t == 0 else vm_ref[(t - 1) * T:t * T, :]
            v_next = vn_ref[...] if t == NQ - 1 else vm_ref[(t + 1) * T:(t + 2) * T, :]
            v_rows = (v_prev, vm_ref[rows, :], v_next, vx_ref[...])
            kt_all = jnp.concatenate([kt_prev, ktm_ref[t, ks, :], kt_next]
                                     + [ktx_ref[cc, ks, :] for cc in range(n_ctx // T)], axis=1)
            rhs = jnp.concatenate([jnp.concatenate([kt_all, zk], axis=1),
                                   jnp.concatenate([zk, kt_all], axis=1)], axis=0)
            top_sl = slice(0, LANES) if k == 0 else slice(LANES, 2 * LANES)
            bot_sl = slice(LANES, 2 * LANES) if k == 0 else slice(0, LANES)
            v_top = jnp.concatenate([jnp.where(low, vr[:, top_sl], 0) for vr in v_rows], axis=0)
            v_bot = jnp.concatenate([jnp.where(low, 0, vr[:, bot_sl]) for vr in v_rows], axis=0)
            v2 = jnp.concatenate([jnp.concatenate([v_top.astype(BF16), ones_top], axis=1),
                                  jnp.concatenate([v_bot.astype(BF16), ones_bot], axis=1)], axis=0)
            for p in range(A_HEADS // A_KV_HEADS // 2):
                g = k * (A_HEADS // A_KV_HEADS // 2) + p
                q2 = q_ref[rows, g * LANES:(g + 1) * LANES]
                s2 = jnp.dot(q2, rhs, preferred_element_type=F32)
                es = []
                sink_terms = []
                for u in range(2):
                    base = u * n_key
                    sink = sink_ref[2 * g + u]
                    s_prev = jnp.where(prev_ok, s2[:, base:base + T], -jnp.inf)
                    s_cur = s2[:, base + T:base + 2 * T]
                    s_next = jnp.where(next_ok, s2[:, base + 2 * T:base + 3 * T], -jnp.inf)
                    s_ctx = s2[:, base + n_loc:base + n_key]
                    m = jnp.maximum(jnp.maximum(s_prev, s_cur), s_next)
                    for cc in range(n_ctx // T):
                        m = jnp.maximum(m, s_ctx[:, cc * T:(cc + 1) * T])
                    m = jnp.maximum(jnp.max(m, axis=1, keepdims=True), sink)
                    es += [jnp.exp(s_prev - m), jnp.exp(s_cur - m), jnp.exp(s_next - m), jnp.exp(s_ctx - m)]
                    sink_terms.append(jnp.exp(sink - m))
                p2 = jnp.concatenate(es, axis=1).astype(BF16)
                o = jnp.dot(p2, v2, preferred_element_type=F32)
                den = o[:, LANES:] + jnp.where(low, sink_terms[0], sink_terms[1])
                o_ref[rows, g * LANES:(g + 1) * LANES] = (o[:, :LANES] / den).astype(BF16)


def _attn_call(sink, qa, kat, va, kat_c, va_c, n_batch, seq_len, n_ctx):
    T = A_BLOCK
    NQ = ATT_Q_BLOCKS
    nb = seq_len // T
    nj = nb // NQ
    r = n_batch * seq_len
    prv = lambda b, j: b * nb + jnp.maximum(j * NQ - 1, 0)
    nxt = lambda b, j: b * nb + jnp.minimum((j + 1) * NQ, nb - 1)
    kedge = lambda f: pl.BlockSpec((1, A_KV_WIDTH, T), lambda b, j: (f(b, j), 0, 0))
    vedge = lambda f: pl.BlockSpec((T, 2 * A_KV_WIDTH), lambda b, j: (f(b, j), 0))
    return pl.pallas_call(
        functools.partial(_attn_kernel, n_ctx=n_ctx),
        grid=(n_batch, nj),
        in_specs=[pl.BlockSpec(memory_space=pltpu.SMEM),
                  pl.BlockSpec((NQ * T, A_WIDTH), lambda b, j: (b * nj + j, 0)),
                  kedge(prv), pl.BlockSpec((NQ, A_KV_WIDTH, T), lambda b, j: (b * nj + j, 0, 0)), kedge(nxt),
                  vedge(prv), pl.BlockSpec((NQ * T, 2 * A_KV_WIDTH), lambda b, j: (b * nj + j, 0)), vedge(nxt),
                  pl.BlockSpec((n_ctx // T, A_KV_WIDTH, T), lambda b, j: (b, 0, 0)),
                  pl.BlockSpec((n_ctx, 2 * A_KV_WIDTH), lambda b, j: (b, 0))],
        out_specs=pl.BlockSpec((NQ * T, A_WIDTH), lambda b, j: (b * nj + j, 0)),
        out_shape=jax.ShapeDtypeStruct((r, A_WIDTH), BF16),
        compiler_params=_params(2),
        name="attn",
    )(sink, qa, kat, kat, kat, va, va, va, kat_c, va_c)


def _mix_ffn_kernel(hf_ref, hb_ref, op_ref, a_ref, x_ref, mod_ref, mg_ref, wo_ref, g1_ref, b1_ref,
                    w13_hbm, w2_hbm, g2_ref, b2_ref, o_ref,
                    w13_ref, w2_ref, st13_ref, st2_ref, sem, h_ref, acc_ref, *, layer, which):
    @pl.when(pl.program_id(0) == 0)
    def _():
        _stage_ffn_weights(w13_hbm, w2_hbm, w13_ref, w2_ref, st13_ref, st2_ref, sem, layer, which)

    h = hf_ref[...] + hb_ref[...]
    gate = _sigmoid(op_ref[...])
    parts = []
    for hd in range(M_HEADS):
        sl = slice(hd * M_DH, (hd + 1) * M_DH)
        seg = h[:, sl]
        mu = jnp.mean(seg, axis=-1, keepdims=True)
        sc = seg - mu
        var = jnp.mean(sc * sc, axis=-1, keepdims=True)
        parts.append((sc * lax.rsqrt(var + LN_EPS) * mg_ref[:, sl] * gate[:, sl]).astype(BF16))
    mixed = jnp.concatenate(parts + [a_ref[...]], axis=1)
    y = jnp.dot(mixed, wo_ref[...], preferred_element_type=F32)
    x2 = _layer_norm(ALPHA * x_ref[...] + mod_ref[0, 5:6, :] * y, g1_ref[...], b1_ref[...])
    o_ref[...] = _ffn_body(x2, mod_ref, 6, w13_ref, w2_ref, g2_ref, b2_ref, h_ref, acc_ref)


def _mix_ffn_call(hf, hb, o_pre, a_l, x1, mod, mod_index, mh_g, w_out, g1, b1, ffn_w13, ffn_w2, layer, which,
                  g2, b2, tm):
    r = x1.shape[0]
    row = lambda w: pl.BlockSpec((tm, w), lambda i: (i, 0))
    vec = lambda a: a.reshape(1, a.shape[-1])
    return pl.pallas_call(
        functools.partial(_mix_ffn_kernel, layer=layer, which=which),
        grid=(r // tm,),
        in_specs=[row(M_WIDTH), row(M_WIDTH), row(M_WIDTH), row(A_WIDTH), row(D_MODEL),
                  pl.BlockSpec((1, N_MOD, D_MODEL), lambda i: (mod_index(i), 0, 0)),
                  _resident((1, M_WIDTH)),
                  _resident((D_MODEL, D_MODEL)),
                  _resident((1, D_MODEL)),
                  _resident((1, D_MODEL)),
                  pl.BlockSpec(memory_space=pl.ANY),
                  pl.BlockSpec(memory_space=pl.ANY),
                  _resident((1, D_MODEL)),
                  _resident((1, D_MODEL))],
        out_specs=row(D_MODEL),
        out_shape=jax.ShapeDtypeStruct((r, D_MODEL), F32),
        scratch_shapes=_ffn_scratch(tm),
        compiler_params=_params(1),
        name="mix_ffn",
    )(hf, hb, o_pre, a_l, x1, mod, vec(mh_g), w_out, vec(g1), vec(b1), ffn_w13, ffn_w2, vec(g2), vec(b2))


def _rope_tables(seq_len):
    t = np.arange(seq_len)
    row = (t // GRID_W).astype(np.float64)
    col = (t % GRID_W).astype(np.float64)
    n_freq = A_DH // 4
    inv = ROPE_BASE ** (-np.arange(n_freq, dtype=np.float64) / n_freq)
    ang = np.concatenate([row[:, None] * inv, col[:, None] * inv], -1)
    cos, sin = np.cos(ang), np.sin(ang)
    return (jnp.asarray(np.tile(cos, (1, 4)), F32),
            jnp.asarray(np.tile(np.concatenate([-sin, sin], -1), (1, 2)), F32))


def _spread_gates(a):
    lead = a.shape[:-1]
    a = a.reshape(lead + (4, M_HEADS))
    a = jnp.pad(a, [(0, 0)] * len(lead) + [(0, 0), (0, GATE_GROUP - M_HEADS)])
    a = a.reshape(lead + (4 * GATE_GROUP,))
    return jnp.pad(a, [(0, 0)] * len(lead) + [(0, LANES - 4 * GATE_GROUP)])


def kernel(x, c, ctx, c_ctx, w_ada, b_ada, ln_g, ln_b, ffn_w13, ffn_w2, w_in, b_gates, conv_w, conv_b,
           mh_norm_g, attn_sink, w_out):
    n_batch, seq_len, d = x.shape
    n_ctx = ctx.shape[1]
    l = 0
    tm = 512
    tm_ctx = n_ctx
    tiles_per_batch = seq_len // tm

    cc = jnp.concatenate([c, c_ctx[None, :], jnp.zeros((SUBLANES - n_batch - 1, d), F32)], axis=0)
    mod = _mod_call(cc, w_ada[l], b_ada[l])[:n_batch + 1].reshape(n_batch + 1, N_MOD, d)
    lat_mod = lambda i: i // tiles_per_batch
    ctx_mod = lambda i: n_batch

    wi = w_in[l]
    wa = wi[:, OFF_QM:OFF_G].astype(BF16)
    w_va = wi[:, OFF_VA:N_IN]
    wt = jnp.concatenate([wi[:, OFF_QA:OFF_VA], w_va, w_va[:, A_DH:], w_va[:, :A_DH],
                          _spread_gates(wi[:, OFF_G:OFF_QA])], axis=1).astype(BF16)
    bg = _spread_gates(b_gates[l]).reshape(1, LANES)
    cb = conv_b[l].reshape(1, 2 * M_WIDTH)
    cos, sin = _rope_tables(seq_len)

    x_lat = x.reshape(n_batch * seq_len, d)
    x_ctx = ctx.reshape(n_batch * n_ctx, d)

    x1, xc1 = _ffn_call(x_lat, x_ctx, mod, tiles_per_batch, ffn_w13, ffn_w2, l, 0, ln_g[l, 0], ln_b[l, 0], tm)

    (_, kt_c, v_c, _, grow_c, _, kat_c, va_c) = _inproj_call(
        xc1, mod, ctx_mod, wa, wt, bg, conv_w[l], cb, None, None, tm_ctx, n_ctx)
    (q_l, kt_l, v_l, o_pre, grow_l, qa_l, kat_l, va_l) = _inproj_call(
        x1, mod, lat_mod, wa, wt, bg, conv_w[l], cb, cos, sin, tm, seq_len)

    c0 = jnp.zeros((n_batch * N_STREAMS, M_DH, M_AUG), F32)
    m0 = jnp.zeros((n_batch * N_STREAMS, LANES), F32)
    c_st, m_st = _mlstm_call(None, kt_c, v_c.reshape(n_batch, n_ctx, M_WIDTH), grow_c, c0, m0,
                             n_batch, n_ctx, False)
    hf, hb, _, _ = _mlstm_call(q_l.reshape(n_batch, seq_len, M_WIDTH), kt_l,
                               v_l.reshape(n_batch, seq_len, M_WIDTH), grow_l, c_st, m_st, n_batch, seq_len, True)
    hf = hf.reshape(n_batch * seq_len, M_WIDTH)
    hb = hb.reshape(n_batch * seq_len, M_WIDTH)

    a_l = _attn_call(attn_sink[l], qa_l, kat_l, va_l, kat_c, va_c, n_batch, seq_len, n_ctx)

    out = _mix_ffn_call(hf, hb, o_pre, a_l, x1, mod, lat_mod, mh_norm_g[l], w_out[l].astype(BF16),
                        ln_g[l, 1], ln_b[l, 1], ffn_w13, ffn_w2, l, 1, ln_g[l, 2], ln_b[l, 2], tm)
    return out.reshape(n_batch, seq_len, d)
```

```python
import functools

import jax
import jax.numpy as jnp
import numpy as np
from jax import lax
from jax.experimental import pallas as pl
from jax.experimental.pallas import tpu as pltpu

F32 = jnp.float32
BF16 = jnp.bfloat16

D_MODEL = 1024
GRID_W = 64
M_HEADS = 4
M_DH = 128
M_WIDTH = M_HEADS * M_DH
A_HEADS = 8
A_KV_HEADS = 2
A_DH = 64
A_WIDTH = A_HEADS * A_DH
A_KV_WIDTH = A_KV_HEADS * A_DH
A_BLOCK = 128
CONV_W = 5
D_FF = 2816
ROPE_BASE = 10000.0
N_MOD = 9
LN_EPS = 1e-5
FFN_RES = 0.5
DEPTH = 1
ALPHA = (2.0 * DEPTH) ** 0.25

OFF_QM = 0
OFF_KM = OFF_QM + M_WIDTH
OFF_VM = OFF_KM + M_WIDTH
OFF_OM = OFF_VM + M_WIDTH
OFF_G = OFF_OM + M_WIDTH
N_GATES = 4 * M_HEADS
OFF_QA = OFF_G + N_GATES
OFF_KA = OFF_QA + A_WIDTH
OFF_VA = OFF_KA + A_KV_WIDTH
N_IN = OFF_VA + A_KV_WIDTH

LANES = 128
SUBLANES = 8
HALO = SUBLANES
MLSTM_CHUNK = LANES
MLSTM_STEP_CHUNKS = 4
FF_CHUNK = 256
COL_BLOCK = 256
ROW_PART = 128
MOD_COLS = 2304
GATE_GROUP = SUBLANES
VMEM_LIMIT = 56 * 1024 * 1024


def _sigmoid(x):
    return 1.0 / (1.0 + jnp.exp(-x))


def _log_sigmoid(x):
    return jnp.minimum(x, 0.0) - jnp.log1p(jnp.exp(-jnp.abs(x)))


def _layer_norm(z, g, b):
    mu = jnp.mean(z, axis=-1, keepdims=True)
    zc = z - mu
    var = jnp.mean(zc * zc, axis=-1, keepdims=True)
    return zc * lax.rsqrt(var + LN_EPS) * g + b


def _params(n_grid, flags=None):
    return pltpu.CompilerParams(dimension_semantics=("arbitrary",) * n_grid,
                                vmem_limit_bytes=VMEM_LIMIT, flags=flags)


def _resident(shape):
    nd = len(shape)
    return pl.BlockSpec(shape, lambda *_: (0,) * nd, pipeline_mode=pl.Buffered(1))


def _mod_kernel(c_ref, w_ref, b_ref, o_ref):
    c = c_ref[...]
    s = (c * _sigmoid(c)).astype(BF16)
    o_ref[...] = jnp.dot(s, w_ref[...].astype(BF16), preferred_element_type=F32) + b_ref[...]


def _mod_call(cc, w_ada, b_ada):
    n = w_ada.shape[1]
    tn = MOD_COLS
    return pl.pallas_call(
        _mod_kernel,
        grid=(n // tn,),
        in_specs=[pl.BlockSpec((SUBLANES, D_MODEL), lambda j: (0, 0)),
                  pl.BlockSpec((D_MODEL, tn), lambda j: (0, j)),
                  pl.BlockSpec((1, tn), lambda j: (0, j))],
        out_specs=pl.BlockSpec((SUBLANES, tn), lambda j: (0, j)),
        out_shape=jax.ShapeDtypeStruct((SUBLANES, n), F32),
        compiler_params=_params(1),
        name="mod",
    )(cc, w_ada, b_ada.reshape(1, n))


def _ffn_body(x, mod_ref, k0, w13_ref, w2_ref, g_ref, b_ref, h_ref, acc_ref):
    shift = mod_ref[0, k0:k0 + 1, :]
    scale = mod_ref[0, k0 + 1:k0 + 2, :]
    gate = mod_ref[0, k0 + 2:k0 + 3, :]
    h_ref[...] = (x * (1.0 + scale) + shift).astype(BF16)
    for j in range(D_FF // FF_CHUNK):
        lo = j * FF_CHUNK
        h = h_ref[...]
        a1 = jnp.dot(h, w13_ref[:, lo:lo + FF_CHUNK], preferred_element_type=F32)
        a3 = jnp.dot(h, w13_ref[:, D_FF + lo:D_FF + lo + FF_CHUNK], preferred_element_type=F32)
        g = (a1 * _sigmoid(a1) * a3).astype(BF16)
        y = jnp.dot(g, w2_ref[lo:lo + FF_CHUNK, :], preferred_element_type=F32)
        if j == 0:
            acc_ref[...] = y
        else:
            acc_ref[...] += y
    z = ALPHA * x + (FFN_RES * gate) * acc_ref[...]
    return _layer_norm(z, g_ref[...], b_ref[...])


W13_STAGE_COLS = 256
W2_STAGE_ROWS = 128


def _stage_ffn_weights(w13_hbm, w2_hbm, w13_ref, w2_ref, st13_ref, st2_ref, sem, layer, which):
    chunks = []
    for c in range(2 * D_FF // W13_STAGE_COLS):
        lo = c * W13_STAGE_COLS
        chunks.append((w13_hbm.at[layer, which, :, lo:lo + W13_STAGE_COLS], st13_ref, w13_ref,
                       (slice(None), slice(lo, lo + W13_STAGE_COLS))))
    for c in range(D_FF // W2_STAGE_ROWS):
        lo = c * W2_STAGE_ROWS
        chunks.append((w2_hbm.at[layer, which, lo:lo + W2_STAGE_ROWS, :], st2_ref, w2_ref,
                       (slice(lo, lo + W2_STAGE_ROWS), slice(None))))

    def copy(n):
        src, stage, _, _ = chunks[n]
        return pltpu.make_async_copy(src, stage.at[n % 2], sem.at[n % 2])

    copy(0).start()
    for n, (_, stage, dst, idx) in enumerate(chunks):
        if n + 1 < len(chunks):
            copy(n + 1).start()
        copy(n).wait()
        dst[idx] = stage[n % 2].astype(BF16)


def _ffn_kernel(x_ref, xc_ref, mod_ref, w13_hbm, w2_hbm, g_ref, b_ref, o_ref, oc_ref,
                w13_ref, w2_ref, st13_ref, st2_ref, sem, h_ref, acc_ref, *, layer, which):
    i = pl.program_id(0)
    n_lat = pl.num_programs(0) - 1

    @pl.when(i == 0)
    def _():
        _stage_ffn_weights(w13_hbm, w2_hbm, w13_ref, w2_ref, st13_ref, st2_ref, sem, layer, which)

    x = jnp.where(i < n_lat, x_ref[...], xc_ref[...])
    y = _ffn_body(x, mod_ref, 0, w13_ref, w2_ref, g_ref, b_ref, h_ref, acc_ref)

    @pl.when(i < n_lat)
    def _():
        o_ref[...] = y

    @pl.when(i == n_lat)
    def _():
        oc_ref[...] = y


def _ffn_scratch(tm):
    return [pltpu.VMEM((D_MODEL, 2 * D_FF), BF16), pltpu.VMEM((D_FF, D_MODEL), BF16),
            pltpu.VMEM((2, D_MODEL, W13_STAGE_COLS), F32), pltpu.VMEM((2, W2_STAGE_ROWS, D_MODEL), F32),
            pltpu.SemaphoreType.DMA((2,)),
            pltpu.VMEM((tm, D_MODEL), BF16), pltpu.VMEM((tm, D_MODEL), F32)]


def _ffn_call(x, x_ctx, mod, tiles_per_batch, ffn_w13, ffn_w2, layer, which, ln_g, ln_b, tm):
    r = x.shape[0]
    n_lat = r // tm
    assert x_ctx.shape[0] == tm
    lat = lambda i: jnp.minimum(i, n_lat - 1)
    return pl.pallas_call(
        functools.partial(_ffn_kernel, layer=layer, which=which),
        grid=(n_lat + 1,),
        in_specs=[pl.BlockSpec((tm, D_MODEL), lambda i: (lat(i), 0)),
                  pl.BlockSpec((tm, D_MODEL), lambda i: (0, 0)),
                  pl.BlockSpec((1, N_MOD, D_MODEL), lambda i: (i // tiles_per_batch, 0, 0)),
                  pl.BlockSpec(memory_space=pl.ANY),
                  pl.BlockSpec(memory_space=pl.ANY),
                  _resident((1, D_MODEL)),
                  _resident((1, D_MODEL))],
        out_specs=[pl.BlockSpec((tm, D_MODEL), lambda i: (lat(i), 0)),
                   pl.BlockSpec((tm, D_MODEL), lambda i: (0, 0))],
        out_shape=[jax.ShapeDtypeStruct((r, D_MODEL), F32), jax.ShapeDtypeStruct((tm, D_MODEL), F32)],
        scratch_shapes=_ffn_scratch(tm),
        compiler_params=_params(1),
        name="ffn",
    )(x, x_ctx, mod, ffn_w13, ffn_w2, ln_g.reshape(1, D_MODEL), ln_b.reshape(1, D_MODEL))


N_WA = 2 * M_WIDTH + 2 * M_WIDTH
OFF_TG = A_WIDTH + A_KV_WIDTH + 2 * A_KV_WIDTH
N_WT = OFF_TG + LANES
ROW_A = (0, 3 * GATE_GROUP)
ROW_B = (GATE_GROUP, 4 * GATE_GROUP)
ROW_C = (2 * GATE_GROUP, 5 * GATE_GROUP)
GATE_ROWS = 6 * GATE_GROUP


def _scan(x, lane, op, reverse):
    fill = 0.0 if op is jnp.add else -jnp.inf
    k = 1
    while k < LANES:
        if reverse:
            other = jnp.where(lane < LANES - k, pltpu.roll(x, LANES - k, 1), fill)
        else:
            other = jnp.where(lane >= k, pltpu.roll(x, k, 1), fill)
        x = op(x, other)
        k *= 2
    return x


def _chunk_scan(x, lane, op, reverse):
    return jnp.concatenate([_scan(x[:, c * LANES:(c + 1) * LANES], lane, op, reverse)
                            for c in range(x.shape[1] // LANES)], axis=1)


def _rope(x, cos, sin_signed, first_half):
    swapped = jnp.where(first_half, pltpu.roll(x, LANES - A_DH // 2, 1), pltpu.roll(x, A_DH // 2, 1))
    return x * cos + swapped * sin_signed


def _inproj_kernel(*refs, tiles_per_seq, rope, tm):
    n_qk = 2 * M_WIDTH // COL_BLOCK
    raw_refs = refs[-n_qk:]
    refs = refs[:-n_qk]
    if rope:
        (xp_ref, x_ref, xn_ref, mod_ref, wa_ref, wt_ref, bg_ref, cw_ref, cb_ref, cos_ref, sin_ref,
         q_ref, kt_ref, v_ref, o_ref, grow_ref, qa_ref, kat_ref, va_ref, h_ref, pt_ref) = refs
    else:
        (xp_ref, x_ref, xn_ref, mod_ref, wa_ref, wt_ref, bg_ref, cw_ref, cb_ref,
         q_ref, kt_ref, v_ref, o_ref, grow_ref, qa_ref, kat_ref, va_ref, h_ref, pt_ref) = refs
    pos = pl.program_id(0) % tiles_per_seq
    slot0 = jnp.minimum(pl.program_id(0), 0)
    n_ext = tm + 2 * HALO
    shift = mod_ref[0, 3:4, :]
    scale1 = 1.0 + mod_ref[0, 4:5, :]
    h_ref[0:tm, :] = (x_ref[...] * scale1 + shift).astype(BF16)
    xh = jnp.concatenate([xp_ref[...], xn_ref[...]], axis=0)
    h_ref[tm:n_ext, :] = (xh * scale1 + shift).astype(BF16)

    def mm(w_ref, lo):
        return jnp.dot(h_ref[0:tm, :], w_ref[:, lo:lo + COL_BLOCK], preferred_element_type=F32)

    def qk_matmul(blk):
        cols = slice(blk * COL_BLOCK, (blk + 1) * COL_BLOCK)
        y = jnp.dot(h_ref[...], wa_ref[:, cols], preferred_element_type=F32)
        raw_refs[blk][0, 0:HALO, :] = jnp.where(pos == 0, 0.0, y[tm:tm + HALO])
        raw_refs[blk][0, HALO:HALO + tm, :] = y[0:tm]
        raw_refs[blk][0, HALO + tm:n_ext, :] = jnp.where(pos == tiles_per_seq - 1, 0.0, y[tm + HALO:n_ext])

    def qk_epilogue(blk, part):
        cols = slice(blk * COL_BLOCK, (blk + 1) * COL_BLOCK)
        r0 = part * ROW_PART
        base = HALO - CONV_W // 2 + r0
        acc = cb_ref[:, cols] + cw_ref[0:1, cols] * raw_refs[blk][slot0, pl.ds(base, ROW_PART), :]
        for j in range(1, CONV_W):
            acc = acc + cw_ref[j:j + 1, cols] * raw_refs[blk][slot0, pl.ds(base + j, ROW_PART), :]
        qk = acc * _sigmoid(acc)
        if blk < M_WIDTH // COL_BLOCK:
            q_ref[r0:r0 + ROW_PART, cols] = (qk * (M_DH ** -0.5)).astype(BF16)
        else:
            kt_ref[0, part, blk * COL_BLOCK - M_WIDTH:(blk + 1) * COL_BLOCK - M_WIDTH, :] = qk.T.astype(BF16)

    def v_matmul(blk):
        v_ref[:, blk * COL_BLOCK:(blk + 1) * COL_BLOCK] = mm(wa_ref, 2 * M_WIDTH + blk * COL_BLOCK).astype(BF16)

    def o_matmul(blk):
        o_ref[:, blk * COL_BLOCK:(blk + 1) * COL_BLOCK] = mm(wa_ref, 3 * M_WIDTH + blk * COL_BLOCK).astype(BF16)

    def t_matmul(blk):
        pt_ref[:, blk * COL_BLOCK:(blk + 1) * COL_BLOCK] = mm(wt_ref, blk * COL_BLOCK)

    def pt_cols(lo, width):
        return pt_ref[:, lo:lo + width]

    def gate_epilogue():
        pgt = (pt_cols(OFF_TG, LANES) + bg_ref[...]).T
        li_f = pgt[0:GATE_GROUP]
        lf_f = _log_sigmoid(pgt[GATE_GROUP:2 * GATE_GROUP])
        li_b = pgt[2 * GATE_GROUP:3 * GATE_GROUP]
        lf_b = _log_sigmoid(pgt[3 * GATE_GROUP:4 * GATE_GROUP])
        lane = lax.broadcasted_iota(jnp.int32, (GATE_GROUP, LANES), 1)
        b_f = _chunk_scan(lf_f, lane, jnp.add, False)
        e_b = _chunk_scan(lf_b, lane, jnp.add, True)
        a_f = li_f - b_f
        a_b = li_b - e_b
        rows = jnp.concatenate([a_f, b_f, _chunk_scan(a_f, lane, jnp.maximum, False),
                                a_b, e_b, _chunk_scan(a_b, lane, jnp.maximum, True)], axis=0)
        for c in range(tm // MLSTM_CHUNK):
            grow_ref[0, c] = rows[:, c * MLSTM_CHUNK:(c + 1) * MLSTM_CHUNK]
        va_ref[:, A_KV_WIDTH:] = pt_cols(A_WIDTH + 2 * A_KV_WIDTH, A_KV_WIDTH).astype(BF16)

    def rotate(lo):
        xg = pt_cols(lo, LANES)
        if not rope:
            return xg
        lane_t = lax.broadcasted_iota(jnp.int32, (tm, LANES), 1)
        return _rope(xg, cos_ref[...], sin_ref[...], (lane_t % A_DH) < (A_DH // 2))

    def qa_epilogue(blk):
        for g in range(blk * COL_BLOCK // LANES, (blk + 1) * COL_BLOCK // LANES):
            qa_ref[:, g * LANES:(g + 1) * LANES] = (rotate(g * LANES) * (A_DH ** -0.5)).astype(BF16)

    def kv_epilogue():
        kat = rotate(A_WIDTH).T.astype(BF16)
        for c in range(tm // A_BLOCK):
            kat_ref[c] = kat[:, c * A_BLOCK:(c + 1) * A_BLOCK]
        va_ref[:, :A_KV_WIDTH] = pt_cols(A_WIDTH + A_KV_WIDTH, A_KV_WIDTH).astype(BF16)

    n_part = tm // ROW_PART
    later = ([functools.partial(t_matmul, b) for b in range(N_WT // COL_BLOCK)]
             + [functools.partial(f, b) for b in range(M_WIDTH // COL_BLOCK) for f in (v_matmul, o_matmul)])
    qk_matmul(0)
    for blk in range(n_qk):
        if blk + 1 < n_qk:
            qk_matmul(blk + 1)
        else:
            later.pop(0)()
            later.pop(0)()
        for part in range(n_part):
            qk_epilogue(blk, part)
    for run in later:
        run()
    gate_epilogue()
    qa_epilogue(0)
    qa_epilogue(1)
    kv_epilogue()


def _inproj_call(x, mod, mod_index, wa, wt, bg, conv_w, conv_b, cos, sin, tm, seq_len):
    r = x.shape[0]
    tiles_per_seq = seq_len // tm
    rope = cos is not None
    hb = tm // HALO
    n_halo = r // HALO
    in_specs = [pl.BlockSpec((HALO, D_MODEL), lambda i: (jnp.maximum(i * hb - 1, 0), 0)),
                pl.BlockSpec((tm, D_MODEL), lambda i: (i, 0)),
                pl.BlockSpec((HALO, D_MODEL), lambda i: (jnp.minimum((i + 1) * hb, n_halo - 1), 0)),
                pl.BlockSpec((1, N_MOD, D_MODEL), lambda i: (mod_index(i), 0, 0)),
                _resident((D_MODEL, N_WA)),
                _resident((D_MODEL, N_WT)),
                _resident((1, LANES)),
                _resident((CONV_W, 2 * M_WIDTH)),
                _resident((1, 2 * M_WIDTH))]
    args = [x, x, x, mod, wa, wt, bg, conv_w, conv_b]
    if rope:
        in_specs += [pl.BlockSpec((tm, LANES), lambda i: (i % tiles_per_seq, 0)),
                     pl.BlockSpec((tm, LANES), lambda i: (i % tiles_per_seq, 0))]
        args += [cos, sin]
    assert MLSTM_CHUNK == A_BLOCK == ROW_PART
    n_seq = r // seq_len
    cps = seq_len // MLSTM_CHUNK
    cpt = tm // MLSTM_CHUNK
    row = lambda w: pl.BlockSpec((tm, w), lambda i: (i, 0))
    seq_col = lambda h: pl.BlockSpec((1, cpt, h, MLSTM_CHUNK),
                                     lambda i: (i // tiles_per_seq, i % tiles_per_seq, 0, 0))
    out_specs = [row(M_WIDTH), seq_col(M_WIDTH), row(M_WIDTH), row(M_WIDTH), seq_col(GATE_ROWS),
                 row(A_WIDTH), pl.BlockSpec((cpt, A_KV_WIDTH, A_BLOCK), lambda i: (i, 0, 0)),
                 row(2 * A_KV_WIDTH)]
    out_shape = [jax.ShapeDtypeStruct((r, M_WIDTH), BF16),
                 jax.ShapeDtypeStruct((n_seq, cps, M_WIDTH, MLSTM_CHUNK), BF16),
                 jax.ShapeDtypeStruct((r, M_WIDTH), BF16),
                 jax.ShapeDtypeStruct((r, M_WIDTH), BF16),
                 jax.ShapeDtypeStruct((n_seq, cps, GATE_ROWS, MLSTM_CHUNK), F32),
                 jax.ShapeDtypeStruct((r, A_WIDTH), BF16),
                 jax.ShapeDtypeStruct((r // A_BLOCK, A_KV_WIDTH, A_BLOCK), BF16),
                 jax.ShapeDtypeStruct((r, 2 * A_KV_WIDTH), BF16)]
    return pl.pallas_call(
        functools.partial(_inproj_kernel, tiles_per_seq=tiles_per_seq, rope=rope, tm=tm),
        grid=(r // tm,),
        in_specs=in_specs,
        out_specs=out_specs,
        out_shape=out_shape,
        scratch_shapes=([pltpu.VMEM((tm + 2 * HALO, D_MODEL), BF16), pltpu.VMEM((tm, N_WT), F32)]
                        + [pltpu.VMEM((1, tm + 2 * HALO, COL_BLOCK), F32)] * (2 * M_WIDTH // COL_BLOCK)),
        compiler_params=_params(1),
        name="inproj",
    )(*args)


M_AUG = 2 * M_DH
N_STREAMS = 2 * M_HEADS


def _col_replicated(row):
    return jnp.broadcast_to(row, (LANES, row.shape[1])).T


def _mlstm_kernel(*refs, emit_h):
    if emit_h:
        (qf_ref, ktf_ref, vf_ref, grf_ref, qb_ref, ktb_ref, vb_ref, grb_ref, c0_ref, m0_ref,
         hf_ref, hb_ref, cout_ref, mout_ref, c_ref, m_ref) = refs
        q_refs, h_refs = (qf_ref, qb_ref), (hf_ref, hb_ref)
    else:
        (ktf_ref, vf_ref, grf_ref, ktb_ref, vb_ref, grb_ref, c0_ref, m0_ref,
         cout_ref, mout_ref, c_ref, m_ref) = refs
    kt_refs, v_refs, gr_refs = (ktf_ref, ktb_ref), (vf_ref, vb_ref), (grf_ref, grb_ref)
    L = MLSTM_CHUNK
    n_batch = c_ref.shape[0] // N_STREAMS
    n_sub = grf_ref.shape[1]
    c = pl.program_id(0)

    @pl.when(c == 0)
    def _():
        c_ref[...] = c0_ref[...]
        m_ref[...] = m0_ref[...]

    t_idx = lax.broadcasted_iota(jnp.int32, (L, L), 0)
    s_idx = lax.broadcasted_iota(jnp.int32, (L, L), 1)
    masks = (s_idx <= t_idx, s_idx >= t_idx)
    ones = jnp.ones((L, M_DH), BF16)
    zeros_k = jnp.zeros((M_DH, L), BF16)
    m_all = m_ref[...]
    m_cur = [m_all[r:r + 1] for r in range(n_batch * N_STREAMS)]
    for sub in range(n_sub):
        for b in range(n_batch):
            for d in range(2):
                last = (L - 1, 0)[d]
                ch = sub if d == 0 else n_sub - 1 - sub
                tok = slice(ch * L, (ch + 1) * L)
                gr = gr_refs[d][b, ch]
                for pair in range(M_HEADS // 2):
                    if emit_h:
                        rows = slice(2 * pair * M_DH, (2 * pair + 2) * M_DH)
                        kt2 = kt_refs[d][b, ch, rows, :]
                        rhs = jnp.concatenate(
                            [jnp.concatenate([kt2[:M_DH], zeros_k], axis=1),
                             jnp.concatenate([zeros_k, kt2[M_DH:]], axis=1)], axis=0)
                        s2 = jnp.dot(q_refs[d][b, tok, rows], rhs, preferred_element_type=F32)
                    for hh in range(2):
                        h = 2 * pair + hh
                        r = (b * 2 + d) * M_HEADS + h
                        sl = slice(h * M_DH, (h + 1) * M_DH)
                        a_row = gr[ROW_A[d] + h:ROW_A[d] + h + 1]
                        b_row = gr[ROW_B[d] + h:ROW_B[d] + h + 1]
                        m_row = m_cur[r]
                        ct = c_ref[r]
                        kt = kt_refs[d][b, ch, sl, :]
                        v_aug = jnp.concatenate([v_refs[d][b, tok, sl], ones], axis=1)
                        if emit_h:
                            mu_row = jnp.maximum(m_row, gr[ROW_C[d] + h:ROW_C[d] + h + 1])
                            mu = _col_replicated(mu_row)
                            mt = _col_replicated(b_row + mu_row)
                            dmat = jnp.where(masks[d], jnp.exp(a_row - mu), 0.0)
                            p = (s2[:, hh * L:(hh + 1) * L] * dmat).astype(BF16)
                            qs = (q_refs[d][b, tok, sl].astype(F32) * jnp.exp(m_row - mu)).astype(BF16)
                            tot = jnp.dot(jnp.concatenate([p, qs], axis=1),
                                          jnp.concatenate([v_aug, ct.astype(BF16)], axis=0),
                                          preferred_element_type=F32)
                            h_dir = tot[:, :M_DH] / jnp.maximum(jnp.abs(tot[:, M_DH:]), jnp.exp(-mt))
                            h_refs[d][b, tok, sl] = h_dir.astype(BF16)
                        mp = jnp.maximum(m_row, jnp.max(a_row, axis=1, keepdims=True))
                        ktw = (kt.astype(F32) * jnp.exp(a_row - mp)).astype(BF16)
                        decay = jnp.exp(m_row - mp)
                        c_ref[r] = (jnp.concatenate([decay, decay], axis=1) * ct
                                    + jnp.dot(ktw, v_aug, preferred_element_type=F32))
                        m_cur[r] = b_row[:, last:last + 1] + mp
    m_ref[...] = jnp.concatenate(m_cur, axis=0)

    @pl.when(c == pl.num_programs(0) - 1)
    def _():
        cout_ref[...] = c_ref[...]
        mout_ref[...] = m_ref[...]


def _mlstm_call(q, kt, v, grow, c0, m0, n_batch, seq_len, emit_h):
    L = MLSTM_CHUNK
    n_sub = min(MLSTM_STEP_CHUNKS, seq_len // L)
    nc = seq_len // (L * n_sub)
    fwd = lambda c: c
    bwd = lambda c: nc - 1 - c
    row = lambda w, f: pl.BlockSpec((n_batch, n_sub * L, w), lambda c: (0, f(c), 0))
    col = lambda hgt, f: pl.BlockSpec((n_batch, n_sub, hgt, L), lambda c: (0, f(c), 0, 0))
    n_str = n_batch * N_STREAMS
    state_specs = [pl.BlockSpec((n_str, M_DH, M_AUG), lambda c: (0, 0, 0)),
                   pl.BlockSpec((n_str, LANES), lambda c: (0, 0))]
    state_shapes = [jax.ShapeDtypeStruct((n_str, M_DH, M_AUG), F32),
                    jax.ShapeDtypeStruct((n_str, LANES), F32)]
    if emit_h:
        in_specs = [row(M_WIDTH, fwd), col(M_WIDTH, fwd), row(M_WIDTH, fwd), col(GATE_ROWS, fwd),
                    row(M_WIDTH, bwd), col(M_WIDTH, bwd), row(M_WIDTH, bwd), col(GATE_ROWS, bwd)]
        args = [q, kt, v, grow, q, kt, v, grow]
        out_specs = [row(M_WIDTH, fwd), row(M_WIDTH, bwd)] + state_specs
        out_shape = [jax.ShapeDtypeStruct((n_batch, seq_len, M_WIDTH), BF16)] * 2 + state_shapes
    else:
        in_specs = [col(M_WIDTH, fwd), row(M_WIDTH, fwd), col(GATE_ROWS, fwd),
                    col(M_WIDTH, bwd), row(M_WIDTH, bwd), col(GATE_ROWS, bwd)]
        args = [kt, v, grow, kt, v, grow]
        out_specs = state_specs
        out_shape = state_shapes
    return pl.pallas_call(
        functools.partial(_mlstm_kernel, emit_h=emit_h),
        grid=(nc,),
        in_specs=in_specs + state_specs,
        out_specs=out_specs,
        out_shape=out_shape,
        scratch_shapes=[pltpu.VMEM((n_str, M_DH, M_AUG), F32), pltpu.VMEM((n_str, LANES), F32)],
        compiler_params=_params(1),
        name="mlstm" if emit_h else "mlstm_ctx",
    )(*args, c0, m0)


ATT_Q_BLOCKS = 16


def _attn_kernel(sink_ref, q_ref, ktp_ref, ktm_ref, ktn_ref, vp_ref, vm_ref, vn_ref, ktx_ref, vx_ref, o_ref,
                 *, n_ctx):
    j = pl.program_id(1)
    nj = pl.num_programs(1)
    T = A_BLOCK
    NQ = ATT_Q_BLOCKS
    n_loc = 3 * T
    n_key = n_loc + n_ctx
    half = LANES // 2
    i_idx = lax.broadcasted_iota(jnp.int32, (T, T), 0)
    r_idx = lax.broadcasted_iota(jnp.int32, (T, T), 1)
    lane_v = lax.broadcasted_iota(jnp.int32, (1, LANES), 1)
    low = lane_v < half
    zk = jnp.zeros((A_DH, n_key), BF16)
    ones_top = jnp.broadcast_to(jnp.where(low, 1.0, 0.0).astype(BF16), (n_key, LANES))
    ones_bot = jnp.broadcast_to(jnp.where(low, 0.0, 1.0).astype(BF16), (n_key, LANES))
    for t in range(NQ):
        rows = slice(t * T, (t + 1) * T)
        prev_ok = (r_idx >= i_idx) & (j > 0) if t == 0 else r_idx >= i_idx
        next_ok = (r_idx <= i_idx) & (j < nj - 1) if t == NQ - 1 else r_idx <= i_idx
        for k in range(A_KV_HEADS):
            ks = slice(k * A_DH, (k + 1) * A_DH)
            kt_prev = ktp_ref[0, ks, :] if t == 0 else ktm_ref[t - 1, ks, :]
            kt_next = ktn_ref[0, ks, :] if t == NQ - 1 else ktm_ref[t + 1, ks, :]
            v_prev = vp_ref[...] if t == 0 else vm_ref[(t - 1) * T:t * T, :]
            v_next = vn_ref[...] if t == NQ - 1 else vm_ref[(t + 1) * T:(t + 2) * T, :]
            v_rows = (v_prev, vm_ref[rows, :], v_next, vx_ref[...])
            kt_all = jnp.concatenate([kt_prev, ktm_ref[t, ks, :], kt_next]
                                     + [ktx_ref[cc, ks, :] for cc in range(n_ctx // T)], axis=1)
            rhs = jnp.concatenate([jnp.concatenate([kt_all, zk], axis=1),
                                   jnp.concatenate([zk, kt_all], axis=1)], axis=0)
            top_sl = slice(0, LANES) if k == 0 else slice(LANES, 2 * LANES)
            bot_sl = slice(LANES, 2 * LANES) if k == 0 else slice(0, LANES)
            v_top = jnp.concatenate([jnp.where(low, vr[:, top_sl], 0) for vr in v_rows], axis=0)
            v_bot = jnp.concatenate([jnp.where(low, 0, vr[:, bot_sl]) for vr in v_rows], axis=0)
            v2 = jnp.concatenate([jnp.concatenate([v_top.astype(BF16), ones_top], axis=1),
                                  jnp.concatenate([v_bot.astype(BF16), ones_bot], axis=1)], axis=0)
            for p in range(A_HEADS // A_KV_HEADS // 2):
                g = k * (A_HEADS // A_KV_HEADS // 2) + p
                q2 = q_ref[rows, g * LANES:(g + 1) * LANES]
                s2 = jnp.dot(q2, rhs, preferred_element_type=F32)
                es = []
                sink_terms = []
                for u in range(2):
                    base = u * n_key
                    sink = sink_ref[2 * g + u]
                    s_prev = jnp.where(prev_ok, s2[:, base:base + T], -jnp.inf)
                    s_cur = s2[:, base + T:base + 2 * T]
                    s_next = jnp.where(next_ok, s2[:, base + 2 * T:base + 3 * T], -jnp.inf)
                    s_ctx = s2[:, base + n_loc:base + n_key]
                    m = jnp.maximum(jnp.maximum(s_prev, s_cur), s_next)
                    for cc in range(n_ctx // T):
                        m = jnp.maximum(m, s_ctx[:, cc * T:(cc + 1) * T])
                    m = jnp.maximum(jnp.max(m, axis=1, keepdims=True), sink)
                    es += [jnp.exp(s_prev - m), jnp.exp(s_cur - m), jnp.exp(s_next - m), jnp.exp(s_ctx - m)]
                    sink_terms.append(jnp.exp(sink - m))
                p2 = jnp.concatenate(es, axis=1).astype(BF16)
                o = jnp.dot(p2, v2, preferred_element_type=F32)
                den = o[:, LANES:] + jnp.where(low, sink_terms[0], sink_terms[1])
                o_ref[rows, g * LANES:(g + 1) * LANES] = (o[:, :LANES] / den).astype(BF16)


def _attn_call(sink, qa, kat, va, kat_c, va_c, n_batch, seq_len, n_ctx):
    T = A_BLOCK
    NQ = ATT_Q_BLOCKS
    nb = seq_len // T
    nj = nb // NQ
    r = n_batch * seq_len
    prv = lambda b, j: b * nb + jnp.maximum(j * NQ - 1, 0)
    nxt = lambda b, j: b * nb + jnp.minimum((j + 1) * NQ, nb - 1)
    kedge = lambda f: pl.BlockSpec((1, A_KV_WIDTH, T), lambda b, j: (f(b, j), 0, 0))
    vedge = lambda f: pl.BlockSpec((T, 2 * A_KV_WIDTH), lambda b, j: (f(b, j), 0))
    return pl.pallas_call(
        functools.partial(_attn_kernel, n_ctx=n_ctx),
        grid=(n_batch, nj),
        in_specs=[pl.BlockSpec(memory_space=pltpu.SMEM),
                  pl.BlockSpec((NQ * T, A_WIDTH), lambda b, j: (b * nj + j, 0)),
                  kedge(prv), pl.BlockSpec((NQ, A_KV_WIDTH, T), lambda b, j: (b * nj + j, 0, 0)), kedge(nxt),
                  vedge(prv), pl.BlockSpec((NQ * T, 2 * A_KV_WIDTH), lambda b, j: (b * nj + j, 0)), vedge(nxt),
                  pl.BlockSpec((n_ctx // T, A_KV_WIDTH, T), lambda b, j: (b, 0, 0)),
                  pl.BlockSpec((n_ctx, 2 * A_KV_WIDTH), lambda b, j: (b, 0))],
        out_specs=pl.BlockSpec((NQ * T, A_WIDTH), lambda b, j: (b * nj + j, 0)),
        out_shape=jax.ShapeDtypeStruct((r, A_WIDTH), BF16),
        compiler_params=_params(2),
        name="attn",
    )(sink, qa, kat, kat, kat, va, va, va, kat_c, va_c)


def _mix_ffn_kernel(hf_ref, hb_ref, op_ref, a_ref, x_ref, mod_ref, mg_ref, wo_ref, g1_ref, b1_ref,
                    w13_hbm, w2_hbm, g2_ref, b2_ref, o_ref,
                    w13_ref, w2_ref, st13_ref, st2_ref, sem, h_ref, acc_ref, *, layer, which):
    @pl.when(pl.program_id(0) == 0)
    def _():
        _stage_ffn_weights(w13_hbm, w2_hbm, w13_ref, w2_ref, st13_ref, st2_ref, sem, layer, which)

    rows_per = hf_ref.shape[0] // MIX_PARTS
    x2 = []
    for p in range(MIX_PARTS):
        rs = slice(p * rows_per, (p + 1) * rows_per)
        h = hf_ref[rs, :].astype(F32) + hb_ref[rs, :].astype(F32)
        gate = _sigmoid(op_ref[rs, :].astype(F32))
        parts = []
        for hd in range(M_HEADS):
            sl = slice(hd * M_DH, (hd + 1) * M_DH)
            seg = h[:, sl]
            mu = jnp.mean(seg, axis=-1, keepdims=True)
            sc = seg - mu
            var = jnp.mean(sc * sc, axis=-1, keepdims=True)
            parts.append((sc * lax.rsqrt(var + LN_EPS) * mg_ref[:, sl] * gate[:, sl]).astype(BF16))
        mixed = jnp.concatenate(parts + [a_ref[rs, :]], axis=1)
        y = jnp.dot(mixed, wo_ref[...], preferred_element_type=F32)
        x2.append(_layer_norm(ALPHA * x_ref[rs, :] + mod_ref[0, 5:6, :] * y, g1_ref[...], b1_ref[...]))
    for p in range(MIX_PARTS):
        rs = slice(p * rows_per, (p + 1) * rows_per)
        o_ref[rs, :] = _ffn_body(x2[p], mod_ref, 6, w13_ref, w2_ref, g2_ref, b2_ref, h_ref.at[p], acc_ref.at[p])


MIX_PARTS = 2


def _mix_ffn_call(hf, hb, o_pre, a_l, x1, mod, mod_index, mh_g, w_out, g1, b1, ffn_w13, ffn_w2, layer, which,
                  g2, b2, tm):
    r = x1.shape[0]
    part = tm // MIX_PARTS
    scratch = _ffn_scratch(part)[:-2] + [pltpu.VMEM((MIX_PARTS, part, D_MODEL), BF16),
                                         pltpu.VMEM((MIX_PARTS, part, D_MODEL), F32)]
    row = lambda w: pl.BlockSpec((tm, w), lambda i: (i, 0))
    vec = lambda a: a.reshape(1, a.shape[-1])
    return pl.pallas_call(
        functools.partial(_mix_ffn_kernel, layer=layer, which=which),
        grid=(r // tm,),
        in_specs=[row(M_WIDTH), row(M_WIDTH), row(M_WIDTH), row(A_WIDTH), row(D_MODEL),
                  pl.BlockSpec((1, N_MOD, D_MODEL), lambda i: (mod_index(i), 0, 0)),
                  _resident((1, M_WIDTH)),
                  _resident((D_MODEL, D_MODEL)),
                  _resident((1, D_MODEL)),
                  _resident((1, D_MODEL)),
                  pl.BlockSpec(memory_space=pl.ANY),
                  pl.BlockSpec(memory_space=pl.ANY),
                  _resident((1, D_MODEL)),
                  _resident((1, D_MODEL))],
        out_specs=row(D_MODEL),
        out_shape=jax.ShapeDtypeStruct((r, D_MODEL), F32),
        scratch_shapes=scratch,
        compiler_params=_params(1),
        name="mix_ffn",
    )(hf, hb, o_pre, a_l, x1, mod, vec(mh_g), w_out, vec(g1), vec(b1), ffn_w13, ffn_w2, vec(g2), vec(b2))


def _rope_tables(seq_len):
    t = np.arange(seq_len)
    row = (t // GRID_W).astype(np.float64)
    col = (t % GRID_W).astype(np.float64)
    n_freq = A_DH // 4
    inv = ROPE_BASE ** (-np.arange(n_freq, dtype=np.float64) / n_freq)
    ang = np.concatenate([row[:, None] * inv, col[:, None] * inv], -1)
    cos, sin = np.cos(ang), np.sin(ang)
    return (jnp.asarray(np.tile(cos, (1, 4)), F32),
            jnp.asarray(np.tile(np.concatenate([-sin, sin], -1), (1, 2)), F32))


def _spread_gates(a):
    lead = a.shape[:-1]
    a = a.reshape(lead + (4, M_HEADS))
    a = jnp.pad(a, [(0, 0)] * len(lead) + [(0, 0), (0, GATE_GROUP - M_HEADS)])
    a = a.reshape(lead + (4 * GATE_GROUP,))
    return jnp.pad(a, [(0, 0)] * len(lead) + [(0, LANES - 4 * GATE_GROUP)])


def kernel(x, c, ctx, c_ctx, w_ada, b_ada, ln_g, ln_b, ffn_w13, ffn_w2, w_in, b_gates, conv_w, conv_b,
           mh_norm_g, attn_sink, w_out):
    n_batch, seq_len, d = x.shape
    n_ctx = ctx.shape[1]
    l = 0
    tm = 512
    tm_ctx = n_ctx
    tiles_per_batch = seq_len // tm

    cc = jnp.concatenate([c, c_ctx[None, :], jnp.zeros((SUBLANES - n_batch - 1, d), F32)], axis=0)
    mod = _mod_call(cc, w_ada[l], b_ada[l])[:n_batch + 1].reshape(n_batch + 1, N_MOD, d)
    lat_mod = lambda i: i // tiles_per_batch
    ctx_mod = lambda i: n_batch

    wi = w_in[l]
    wa = wi[:, OFF_QM:OFF_G].astype(BF16)
    w_va = wi[:, OFF_VA:N_IN]
    wt = jnp.concatenate([wi[:, OFF_QA:OFF_VA], w_va, w_va[:, A_DH:], w_va[:, :A_DH],
                          _spread_gates(wi[:, OFF_G:OFF_QA])], axis=1).astype(BF16)
    bg = _spread_gates(b_gates[l]).reshape(1, LANES)
    cb = conv_b[l].reshape(1, 2 * M_WIDTH)
    cos, sin = _rope_tables(seq_len)

    x_lat = x.reshape(n_batch * seq_len, d)
    x_ctx = ctx.reshape(n_batch * n_ctx, d)

    x1, xc1 = _ffn_call(x_lat, x_ctx, mod, tiles_per_batch, ffn_w13, ffn_w2, l, 0, ln_g[l, 0], ln_b[l, 0], tm)

    (_, kt_c, v_c, _, grow_c, _, kat_c, va_c) = _inproj_call(
        xc1, mod, ctx_mod, wa, wt, bg, conv_w[l], cb, None, None, tm_ctx, n_ctx)
    (q_l, kt_l, v_l, o_pre, grow_l, qa_l, kat_l, va_l) = _inproj_call(
        x1, mod, lat_mod, wa, wt, bg, conv_w[l], cb, cos, sin, tm, seq_len)

    c0 = jnp.zeros((n_batch * N_STREAMS, M_DH, M_AUG), F32)
    m0 = jnp.zeros((n_batch * N_STREAMS, LANES), F32)
    c_st, m_st = _mlstm_call(None, kt_c, v_c.reshape(n_batch, n_ctx, M_WIDTH), grow_c, c0, m0,
                             n_batch, n_ctx, False)
    hf, hb, _, _ = _mlstm_call(q_l.reshape(n_batch, seq_len, M_WIDTH), kt_l,
                               v_l.reshape(n_batch, seq_len, M_WIDTH), grow_l, c_st, m_st, n_batch, seq_len, True)
    hf = hf.reshape(n_batch * seq_len, M_WIDTH)
    hb = hb.reshape(n_batch * seq_len, M_WIDTH)

    a_l = _attn_call(attn_sink[l], qa_l, kat_l, va_l, kat_c, va_c, n_batch, seq_len, n_ctx)

    tm_mix = MIX_PARTS * tm
    out = _mix_ffn_call(hf, hb, o_pre, a_l, x1, mod, lambda i: i // (seq_len // tm_mix), mh_norm_g[l],
                        w_out[l].astype(BF16), ln_g[l, 1], ln_b[l, 1], ffn_w13, ffn_w2, l, 1,
                        ln_g[l, 2], ln_b[l, 2], tm_mix)
    return out.reshape(n_batch, seq_len, d)
```

```python
import functools

import jax
import jax.numpy as jnp
import numpy as np
from jax import lax
from jax.experimental import pallas as pl
from jax.experimental.pallas import tpu as pltpu

F32 = jnp.float32
BF16 = jnp.bfloat16

D_MODEL = 1024
GRID_W = 64
M_HEADS = 4
M_DH = 128
M_WIDTH = M_HEADS * M_DH
A_HEADS = 8
A_KV_HEADS = 2
A_DH = 64
A_WIDTH = A_HEADS * A_DH
A_KV_WIDTH = A_KV_HEADS * A_DH
A_BLOCK = 128
CONV_W = 5
D_FF = 2816
ROPE_BASE = 10000.0
N_MOD = 9
LN_EPS = 1e-5
FFN_RES = 0.5
DEPTH = 1
ALPHA = (2.0 * DEPTH) ** 0.25

OFF_QM = 0
OFF_KM = OFF_QM + M_WIDTH
OFF_VM = OFF_KM + M_WIDTH
OFF_OM = OFF_VM + M_WIDTH
OFF_G = OFF_OM + M_WIDTH
N_GATES = 4 * M_HEADS
OFF_QA = OFF_G + N_GATES
OFF_KA = OFF_QA + A_WIDTH
OFF_VA = OFF_KA + A_KV_WIDTH
N_IN = OFF_VA + A_KV_WIDTH

LANES = 128
SUBLANES = 8
HALO = SUBLANES
MLSTM_CHUNK = LANES
MLSTM_STEP_CHUNKS = 4
FF_CHUNK = 256
COL_BLOCK = 256
ROW_PART = 128
MOD_COLS = 2304
GATE_GROUP = SUBLANES
VMEM_LIMIT = 56 * 1024 * 1024


def _sigmoid(x):
    return 1.0 / (1.0 + jnp.exp(-x))


def _log_sigmoid(x):
    return jnp.minimum(x, 0.0) - jnp.log1p(jnp.exp(-jnp.abs(x)))


def _layer_norm(z, g, b):
    mu = jnp.mean(z, axis=-1, keepdims=True)
    zc = z - mu
    var = jnp.mean(zc * zc, axis=-1, keepdims=True)
    return zc * lax.rsqrt(var + LN_EPS) * g + b


def _params(n_grid, flags=None):
    return pltpu.CompilerParams(dimension_semantics=("arbitrary",) * n_grid,
                                vmem_limit_bytes=VMEM_LIMIT, flags=flags)


def _resident(shape):
    nd = len(shape)
    return pl.BlockSpec(shape, lambda *_: (0,) * nd, pipeline_mode=pl.Buffered(1))


def _mod_kernel(c_ref, w_ref, b_ref, o_ref):
    c = c_ref[...]
    s = (c * _sigmoid(c)).astype(BF16)
    o_ref[...] = jnp.dot(s, w_ref[...].astype(BF16), preferred_element_type=F32) + b_ref[...]


def _mod_call(cc, w_ada, b_ada):
    n = w_ada.shape[1]
    tn = MOD_COLS
    return pl.pallas_call(
        _mod_kernel,
        grid=(n // tn,),
        in_specs=[pl.BlockSpec((SUBLANES, D_MODEL), lambda j: (0, 0)),
                  pl.BlockSpec((D_MODEL, tn), lambda j: (0, j)),
                  pl.BlockSpec((1, tn), lambda j: (0, j))],
        out_specs=pl.BlockSpec((SUBLANES, tn), lambda j: (0, j)),
        out_shape=jax.ShapeDtypeStruct((SUBLANES, n), F32),
        compiler_params=_params(1),
        name="mod",
    )(cc, w_ada, b_ada.reshape(1, n))


def _ffn_body(x, mod_ref, k0, w13_ref, w2_ref, g_ref, b_ref, h_ref, acc_ref):
    shift = mod_ref[0, k0:k0 + 1, :]
    scale = mod_ref[0, k0 + 1:k0 + 2, :]
    gate = mod_ref[0, k0 + 2:k0 + 3, :]
    h_ref[...] = (x * (1.0 + scale) + shift).astype(BF16)
    for j in range(D_FF // FF_CHUNK):
        lo = j * FF_CHUNK
        h = h_ref[...]
        a1 = jnp.dot(h, w13_ref[:, lo:lo + FF_CHUNK], preferred_element_type=F32)
        a3 = jnp.dot(h, w13_ref[:, D_FF + lo:D_FF + lo + FF_CHUNK], preferred_element_type=F32)
        g = (a1 * _sigmoid(a1) * a3).astype(BF16)
        y = jnp.dot(g, w2_ref[lo:lo + FF_CHUNK, :], preferred_element_type=F32)
        if j == 0:
            acc_ref[...] = y
        else:
            acc_ref[...] += y
    z = ALPHA * x + (FFN_RES * gate) * acc_ref[...]
    return _layer_norm(z, g_ref[...], b_ref[...])


W13_STAGE_COLS = 512
W2_STAGE_ROWS = 256


def _stage_as_bf16(chunks, sem):
    def copy(n):
        src, stage, _, _ = chunks[n]
        return pltpu.make_async_copy(src, stage.at[n % 2], sem.at[n % 2])

    copy(0).start()
    for n, (_, stage, dst, idx) in enumerate(chunks):
        if n + 1 < len(chunks):
            copy(n + 1).start()
        copy(n).wait()
        dst[idx] = stage[n % 2].astype(BF16)


def _stage_ffn_weights(w13_hbm, w2_hbm, w13_ref, w2_ref, st13_ref, st2_ref, sem, layer, which):
    chunks = []
    for c in range(2 * D_FF // W13_STAGE_COLS):
        lo = c * W13_STAGE_COLS
        chunks.append((w13_hbm.at[layer, which, :, lo:lo + W13_STAGE_COLS], st13_ref, w13_ref,
                       (slice(None), slice(lo, lo + W13_STAGE_COLS))))
    for c in range(D_FF // W2_STAGE_ROWS):
        lo = c * W2_STAGE_ROWS
        chunks.append((w2_hbm.at[layer, which, lo:lo + W2_STAGE_ROWS, :], st2_ref, w2_ref,
                       (slice(lo, lo + W2_STAGE_ROWS), slice(None))))
    _stage_as_bf16(chunks, sem)


def _ffn_kernel(x_ref, xc_ref, mod_ref, w13_hbm, w2_hbm, g_ref, b_ref, o_ref, oc_ref,
                w13_ref, w2_ref, st13_ref, st2_ref, sem, h_ref, acc_ref, *, layer, which):
    i = pl.program_id(0)
    n_lat = pl.num_programs(0) - 1

    @pl.when(i == 0)
    def _():
        _stage_ffn_weights(w13_hbm, w2_hbm, w13_ref, w2_ref, st13_ref, st2_ref, sem, layer, which)

    x = jnp.where(i < n_lat, x_ref[...], xc_ref[...])
    y = _ffn_body(x, mod_ref, 0, w13_ref, w2_ref, g_ref, b_ref, h_ref, acc_ref)

    @pl.when(i < n_lat)
    def _():
        o_ref[...] = y

    @pl.when(i == n_lat)
    def _():
        oc_ref[...] = y


def _ffn_scratch(tm):
    return [pltpu.VMEM((D_MODEL, 2 * D_FF), BF16), pltpu.VMEM((D_FF, D_MODEL), BF16),
            pltpu.VMEM((2, D_MODEL, W13_STAGE_COLS), F32), pltpu.VMEM((2, W2_STAGE_ROWS, D_MODEL), F32),
            pltpu.SemaphoreType.DMA((2,)),
            pltpu.VMEM((tm, D_MODEL), BF16), pltpu.VMEM((tm, D_MODEL), F32)]


def _ffn_call(x, x_ctx, mod, tiles_per_batch, ffn_w13, ffn_w2, layer, which, ln_g, ln_b, tm):
    r = x.shape[0]
    n_lat = r // tm
    assert x_ctx.shape[0] == tm
    lat = lambda i: jnp.minimum(i, n_lat - 1)
    return pl.pallas_call(
        functools.partial(_ffn_kernel, layer=layer, which=which),
        grid=(n_lat + 1,),
        in_specs=[pl.BlockSpec((tm, D_MODEL), lambda i: (lat(i), 0)),
                  pl.BlockSpec((tm, D_MODEL), lambda i: (0, 0)),
                  pl.BlockSpec((1, N_MOD, D_MODEL), lambda i: (i // tiles_per_batch, 0, 0)),
                  pl.BlockSpec(memory_space=pl.ANY),
                  pl.BlockSpec(memory_space=pl.ANY),
                  _resident((1, D_MODEL)),
                  _resident((1, D_MODEL))],
        out_specs=[pl.BlockSpec((tm, D_MODEL), lambda i: (lat(i), 0)),
                   pl.BlockSpec((tm, D_MODEL), lambda i: (0, 0))],
        out_shape=[jax.ShapeDtypeStruct((r, D_MODEL), F32), jax.ShapeDtypeStruct((tm, D_MODEL), F32)],
        scratch_shapes=_ffn_scratch(tm),
        compiler_params=_params(1),
        name="ffn",
    )(x, x_ctx, mod, ffn_w13, ffn_w2, ln_g.reshape(1, D_MODEL), ln_b.reshape(1, D_MODEL))


N_WA = 2 * M_WIDTH + 2 * M_WIDTH
OFF_TG = A_WIDTH + A_KV_WIDTH + 2 * A_KV_WIDTH
N_WT = OFF_TG + LANES
ROW_A = (0, 3 * GATE_GROUP)
ROW_B = (GATE_GROUP, 4 * GATE_GROUP)
ROW_C = (2 * GATE_GROUP, 5 * GATE_GROUP)
GATE_ROWS = 6 * GATE_GROUP


def _scan(x, lane, op, reverse):
    fill = 0.0 if op is jnp.add else -jnp.inf
    k = 1
    while k < LANES:
        if reverse:
            other = jnp.where(lane < LANES - k, pltpu.roll(x, LANES - k, 1), fill)
        else:
            other = jnp.where(lane >= k, pltpu.roll(x, k, 1), fill)
        x = op(x, other)
        k *= 2
    return x


def _chunk_scan(x, lane, op, reverse):
    return jnp.concatenate([_scan(x[:, c * LANES:(c + 1) * LANES], lane, op, reverse)
                            for c in range(x.shape[1] // LANES)], axis=1)


def _rope(x, cos, sin_signed, first_half):
    swapped = jnp.where(first_half, pltpu.roll(x, LANES - A_DH // 2, 1), pltpu.roll(x, A_DH // 2, 1))
    return x * cos + swapped * sin_signed


def _inproj_kernel(*refs, tiles_per_seq, rope, tm):
    n_qk = 2 * M_WIDTH // COL_BLOCK
    raw_refs = refs[-n_qk:]
    refs = refs[:-n_qk]
    if rope:
        (xp_ref, x_ref, xn_ref, mod_ref, wa_ref, wt_ref, bg_ref, cw_ref, cb_ref, cos_ref, sin_ref,
         q_ref, kt_ref, v_ref, o_ref, grow_ref, qa_ref, kat_ref, va_ref, h_ref, pt_ref) = refs
    else:
        (xp_ref, x_ref, xn_ref, mod_ref, wa_ref, wt_ref, bg_ref, cw_ref, cb_ref,
         q_ref, kt_ref, v_ref, o_ref, grow_ref, qa_ref, kat_ref, va_ref, h_ref, pt_ref) = refs
    pos = pl.program_id(0) % tiles_per_seq
    slot0 = jnp.minimum(pl.program_id(0), 0)
    n_ext = tm + 2 * HALO
    shift = mod_ref[0, 3:4, :]
    scale1 = 1.0 + mod_ref[0, 4:5, :]
    h_ref[0:tm, :] = (x_ref[...] * scale1 + shift).astype(BF16)
    xh = jnp.concatenate([xp_ref[...], xn_ref[...]], axis=0)
    h_ref[tm:n_ext, :] = (xh * scale1 + shift).astype(BF16)

    def mm(w_ref, lo):
        return jnp.dot(h_ref[0:tm, :], w_ref[:, lo:lo + COL_BLOCK], preferred_element_type=F32)

    def qk_matmul(blk):
        cols = slice(blk * COL_BLOCK, (blk + 1) * COL_BLOCK)
        y = jnp.dot(h_ref[...], wa_ref[:, cols], preferred_element_type=F32)
        raw_refs[blk][0, 0:HALO, :] = jnp.where(pos == 0, 0.0, y[tm:tm + HALO])
        raw_refs[blk][0, HALO:HALO + tm, :] = y[0:tm]
        raw_refs[blk][0, HALO + tm:n_ext, :] = jnp.where(pos == tiles_per_seq - 1, 0.0, y[tm + HALO:n_ext])

    def qk_epilogue(blk, part):
        cols = slice(blk * COL_BLOCK, (blk + 1) * COL_BLOCK)
        r0 = part * ROW_PART
        base = HALO - CONV_W // 2 + r0
        acc = cb_ref[:, cols] + cw_ref[0:1, cols] * raw_refs[blk][slot0, pl.ds(base, ROW_PART), :]
        for j in range(1, CONV_W):
            acc = acc + cw_ref[j:j + 1, cols] * raw_refs[blk][slot0, pl.ds(base + j, ROW_PART), :]
        qk = acc * _sigmoid(acc)
        if blk < M_WIDTH // COL_BLOCK:
            q_ref[r0:r0 + ROW_PART, cols] = (qk * (M_DH ** -0.5)).astype(BF16)
        else:
            kt_ref[0, part, blk * COL_BLOCK - M_WIDTH:(blk + 1) * COL_BLOCK - M_WIDTH, :] = qk.T.astype(BF16)

    def v_matmul(blk):
        v_ref[:, blk * COL_BLOCK:(blk + 1) * COL_BLOCK] = mm(wa_ref, 2 * M_WIDTH + blk * COL_BLOCK).astype(BF16)

    def o_matmul(blk):
        o_ref[:, blk * COL_BLOCK:(blk + 1) * COL_BLOCK] = mm(wa_ref, 3 * M_WIDTH + blk * COL_BLOCK).astype(BF16)

    def t_matmul(blk):
        pt_ref[:, blk * COL_BLOCK:(blk + 1) * COL_BLOCK] = mm(wt_ref, blk * COL_BLOCK)

    def pt_cols(lo, width):
        return pt_ref[:, lo:lo + width]

    def gate_epilogue():
        pgt = (pt_cols(OFF_TG, LANES) + bg_ref[...]).T
        li_f = pgt[0:GATE_GROUP]
        lf_f = _log_sigmoid(pgt[GATE_GROUP:2 * GATE_GROUP])
        li_b = pgt[2 * GATE_GROUP:3 * GATE_GROUP]
        lf_b = _log_sigmoid(pgt[3 * GATE_GROUP:4 * GATE_GROUP])
        lane = lax.broadcasted_iota(jnp.int32, (GATE_GROUP, LANES), 1)
        b_f = _chunk_scan(lf_f, lane, jnp.add, False)
        e_b = _chunk_scan(lf_b, lane, jnp.add, True)
        a_f = li_f - b_f
        a_b = li_b - e_b
        rows = jnp.concatenate([a_f, b_f, _chunk_scan(a_f, lane, jnp.maximum, False),
                                a_b, e_b, _chunk_scan(a_b, lane, jnp.maximum, True)], axis=0)
        for c in range(tm // MLSTM_CHUNK):
            grow_ref[0, c] = rows[:, c * MLSTM_CHUNK:(c + 1) * MLSTM_CHUNK]
        va_ref[:, A_KV_WIDTH:] = pt_cols(A_WIDTH + 2 * A_KV_WIDTH, A_KV_WIDTH).astype(BF16)

    def rotate(lo):
        xg = pt_cols(lo, LANES)
        if not rope:
            return xg
        lane_t = lax.broadcasted_iota(jnp.int32, (tm, LANES), 1)
        return _rope(xg, cos_ref[...], sin_ref[...], (lane_t % A_DH) < (A_DH // 2))

    def qa_epilogue(blk):
        for g in range(blk * COL_BLOCK // LANES, (blk + 1) * COL_BLOCK // LANES):
            qa_ref[:, g * LANES:(g + 1) * LANES] = (rotate(g * LANES) * (A_DH ** -0.5)).astype(BF16)

    def kv_epilogue():
        kat = rotate(A_WIDTH).T.astype(BF16)
        for c in range(tm // A_BLOCK):
            kat_ref[c] = kat[:, c * A_BLOCK:(c + 1) * A_BLOCK]
        va_ref[:, :A_KV_WIDTH] = pt_cols(A_WIDTH + A_KV_WIDTH, A_KV_WIDTH).astype(BF16)

    n_part = tm // ROW_PART
    later = ([functools.partial(t_matmul, b) for b in range(N_WT // COL_BLOCK)]
             + [functools.partial(f, b) for b in range(M_WIDTH // COL_BLOCK) for f in (v_matmul, o_matmul)])
    qk_matmul(0)
    for blk in range(n_qk):
        if blk + 1 < n_qk:
            qk_matmul(blk + 1)
        else:
            later.pop(0)()
            later.pop(0)()
        for part in range(n_part):
            qk_epilogue(blk, part)
    for run in later:
        run()
    gate_epilogue()
    qa_epilogue(0)
    qa_epilogue(1)
    kv_epilogue()


def _inproj_call(x, mod, mod_index, wa, wt, bg, conv_w, conv_b, cos, sin, tm, seq_len):
    r = x.shape[0]
    tiles_per_seq = seq_len // tm
    rope = cos is not None
    hb = tm // HALO
    n_halo = r // HALO
    in_specs = [pl.BlockSpec((HALO, D_MODEL), lambda i: (jnp.maximum(i * hb - 1, 0), 0)),
                pl.BlockSpec((tm, D_MODEL), lambda i: (i, 0)),
                pl.BlockSpec((HALO, D_MODEL), lambda i: (jnp.minimum((i + 1) * hb, n_halo - 1), 0)),
                pl.BlockSpec((1, N_MOD, D_MODEL), lambda i: (mod_index(i), 0, 0)),
                _resident((D_MODEL, N_WA)),
                _resident((D_MODEL, N_WT)),
                _resident((1, LANES)),
                _resident((CONV_W, 2 * M_WIDTH)),
                _resident((1, 2 * M_WIDTH))]
    args = [x, x, x, mod, wa, wt, bg, conv_w, conv_b]
    if rope:
        in_specs += [pl.BlockSpec((tm, LANES), lambda i: (i % tiles_per_seq, 0)),
                     pl.BlockSpec((tm, LANES), lambda i: (i % tiles_per_seq, 0))]
        args += [cos, sin]
    assert MLSTM_CHUNK == A_BLOCK == ROW_PART
    n_seq = r // seq_len
    cps = seq_len // MLSTM_CHUNK
    cpt = tm // MLSTM_CHUNK
    row = lambda w: pl.BlockSpec((tm, w), lambda i: (i, 0))
    seq_col = lambda h: pl.BlockSpec((1, cpt, h, MLSTM_CHUNK),
                                     lambda i: (i // tiles_per_seq, i % tiles_per_seq, 0, 0))
    out_specs = [row(M_WIDTH), seq_col(M_WIDTH), row(M_WIDTH), row(M_WIDTH), seq_col(GATE_ROWS),
                 row(A_WIDTH), pl.BlockSpec((cpt, A_KV_WIDTH, A_BLOCK), lambda i: (i, 0, 0)),
                 row(2 * A_KV_WIDTH)]
    out_shape = [jax.ShapeDtypeStruct((r, M_WIDTH), BF16),
                 jax.ShapeDtypeStruct((n_seq, cps, M_WIDTH, MLSTM_CHUNK), BF16),
                 jax.ShapeDtypeStruct((r, M_WIDTH), BF16),
                 jax.ShapeDtypeStruct((r, M_WIDTH), BF16),
                 jax.ShapeDtypeStruct((n_seq, cps, GATE_ROWS, MLSTM_CHUNK), F32),
                 jax.ShapeDtypeStruct((r, A_WIDTH), BF16),
                 jax.ShapeDtypeStruct((r // A_BLOCK, A_KV_WIDTH, A_BLOCK), BF16),
                 jax.ShapeDtypeStruct((r, 2 * A_KV_WIDTH), BF16)]
    return pl.pallas_call(
        functools.partial(_inproj_kernel, tiles_per_seq=tiles_per_seq, rope=rope, tm=tm),
        grid=(r // tm,),
        in_specs=in_specs,
        out_specs=out_specs,
        out_shape=out_shape,
        scratch_shapes=([pltpu.VMEM((tm + 2 * HALO, D_MODEL), BF16), pltpu.VMEM((tm, N_WT), F32)]
                        + [pltpu.VMEM((1, tm + 2 * HALO, COL_BLOCK), F32)] * (2 * M_WIDTH // COL_BLOCK)),
        compiler_params=_params(1),
        name="inproj",
    )(*args)


M_AUG = 2 * M_DH
N_STREAMS = 2 * M_HEADS


def _col_replicated(row):
    return jnp.broadcast_to(row, (LANES, row.shape[1])).T


def _mlstm_kernel(*refs, emit_h):
    if emit_h:
        (qf_ref, ktf_ref, vf_ref, grf_ref, qb_ref, ktb_ref, vb_ref, grb_ref, c0_ref, m0_ref,
         hf_ref, hb_ref, cout_ref, mout_ref, c_ref, m_ref) = refs
        q_refs, h_refs = (qf_ref, qb_ref), (hf_ref, hb_ref)
    else:
        (ktf_ref, vf_ref, grf_ref, ktb_ref, vb_ref, grb_ref, c0_ref, m0_ref,
         cout_ref, mout_ref, c_ref, m_ref) = refs
    kt_refs, v_refs, gr_refs = (ktf_ref, ktb_ref), (vf_ref, vb_ref), (grf_ref, grb_ref)
    L = MLSTM_CHUNK
    n_batch = c_ref.shape[0] // N_STREAMS
    n_sub = grf_ref.shape[1]
    c = pl.program_id(0)

    @pl.when(c == 0)
    def _():
        c_ref[...] = c0_ref[...]
        m_ref[...] = m0_ref[...]

    t_idx = lax.broadcasted_iota(jnp.int32, (L, L), 0)
    s_idx = lax.broadcasted_iota(jnp.int32, (L, L), 1)
    masks = (s_idx <= t_idx, s_idx >= t_idx)
    ones = jnp.ones((L, M_DH), BF16)
    zeros_k = jnp.zeros((M_DH, L), BF16)
    m_all = m_ref[...]
    m_cur = [m_all[r:r + 1] for r in range(n_batch * N_STREAMS)]
    for sub in range(n_sub):
        for b in range(n_batch):
            for d in range(2):
                last = (L - 1, 0)[d]
                ch = sub if d == 0 else n_sub - 1 - sub
                tok = slice(ch * L, (ch + 1) * L)
                gr = gr_refs[d][b, ch]
                for pair in range(M_HEADS // 2):
                    if emit_h:
                        rows = slice(2 * pair * M_DH, (2 * pair + 2) * M_DH)
                        kt2 = kt_refs[d][b, ch, rows, :]
                        rhs = jnp.concatenate(
                            [jnp.concatenate([kt2[:M_DH], zeros_k], axis=1),
                             jnp.concatenate([zeros_k, kt2[M_DH:]], axis=1)], axis=0)
                        s2 = jnp.dot(q_refs[d][b, tok, rows], rhs, preferred_element_type=F32)
                    for hh in range(2):
                        h = 2 * pair + hh
                        r = (b * 2 + d) * M_HEADS + h
                        sl = slice(h * M_DH, (h + 1) * M_DH)
                        a_row = gr[ROW_A[d] + h:ROW_A[d] + h + 1]
                        b_row = gr[ROW_B[d] + h:ROW_B[d] + h + 1]
                        m_row = m_cur[r]
                        ct = c_ref[r]
                        kt = kt_refs[d][b, ch, sl, :]
                        v_aug = jnp.concatenate([v_refs[d][b, tok, sl], ones], axis=1)
                        if emit_h:
                            mu_row = jnp.maximum(m_row, gr[ROW_C[d] + h:ROW_C[d] + h + 1])
                            mu = _col_replicated(mu_row)
                            mt = _col_replicated(b_row + mu_row)
                            dmat = jnp.where(masks[d], jnp.exp(a_row - mu), 0.0)
                            p = (s2[:, hh * L:(hh + 1) * L] * dmat).astype(BF16)
                            qs = (q_refs[d][b, tok, sl].astype(F32) * jnp.exp(m_row - mu)).astype(BF16)
                            tot = jnp.dot(jnp.concatenate([p, qs], axis=1),
                                          jnp.concatenate([v_aug, ct.astype(BF16)], axis=0),
                                          preferred_element_type=F32)
                            h_dir = tot[:, :M_DH] / jnp.maximum(jnp.abs(tot[:, M_DH:]), jnp.exp(-mt))
                            h_refs[d][b, tok, sl] = h_dir.astype(BF16)
                        mp = jnp.maximum(m_row, jnp.max(a_row, axis=1, keepdims=True))
                        ktw = (kt.astype(F32) * jnp.exp(a_row - mp)).astype(BF16)
                        decay = jnp.exp(m_row - mp)
                        c_ref[r] = (jnp.concatenate([decay, decay], axis=1) * ct
                                    + jnp.dot(ktw, v_aug, preferred_element_type=F32))
                        m_cur[r] = b_row[:, last:last + 1] + mp
    m_ref[...] = jnp.concatenate(m_cur, axis=0)

    @pl.when(c == pl.num_programs(0) - 1)
    def _():
        cout_ref[...] = c_ref[...]
        mout_ref[...] = m_ref[...]


def _mlstm_call(q, kt, v, grow, c0, m0, n_batch, seq_len, emit_h):
    L = MLSTM_CHUNK
    n_sub = min(MLSTM_STEP_CHUNKS, seq_len // L)
    nc = seq_len // (L * n_sub)
    fwd = lambda c: c
    bwd = lambda c: nc - 1 - c
    row = lambda w, f: pl.BlockSpec((n_batch, n_sub * L, w), lambda c: (0, f(c), 0))
    col = lambda hgt, f: pl.BlockSpec((n_batch, n_sub, hgt, L), lambda c: (0, f(c), 0, 0))
    n_str = n_batch * N_STREAMS
    state_specs = [pl.BlockSpec((n_str, M_DH, M_AUG), lambda c: (0, 0, 0)),
                   pl.BlockSpec((n_str, LANES), lambda c: (0, 0))]
    state_shapes = [jax.ShapeDtypeStruct((n_str, M_DH, M_AUG), F32),
                    jax.ShapeDtypeStruct((n_str, LANES), F32)]
    if emit_h:
        in_specs = [row(M_WIDTH, fwd), col(M_WIDTH, fwd), row(M_WIDTH, fwd), col(GATE_ROWS, fwd),
                    row(M_WIDTH, bwd), col(M_WIDTH, bwd), row(M_WIDTH, bwd), col(GATE_ROWS, bwd)]
        args = [q, kt, v, grow, q, kt, v, grow]
        out_specs = [row(M_WIDTH, fwd), row(M_WIDTH, bwd)] + state_specs
        out_shape = [jax.ShapeDtypeStruct((n_batch, seq_len, M_WIDTH), BF16)] * 2 + state_shapes
    else:
        in_specs = [col(M_WIDTH, fwd), row(M_WIDTH, fwd), col(GATE_ROWS, fwd),
                    col(M_WIDTH, bwd), row(M_WIDTH, bwd), col(GATE_ROWS, bwd)]
        args = [kt, v, grow, kt, v, grow]
        out_specs = state_specs
        out_shape = state_shapes
    return pl.pallas_call(
        functools.partial(_mlstm_kernel, emit_h=emit_h),
        grid=(nc,),
        in_specs=in_specs + state_specs,
        out_specs=out_specs,
        out_shape=out_shape,
        scratch_shapes=[pltpu.VMEM((n_str, M_DH, M_AUG), F32), pltpu.VMEM((n_str, LANES), F32)],
        compiler_params=_params(1),
        name="mlstm" if emit_h else "mlstm_ctx",
    )(*args, c0, m0)


ATT_Q_BLOCKS = 16


def _attn_kernel(sink_ref, q_ref, ktp_ref, ktm_ref, ktn_ref, vp_ref, vm_ref, vn_ref, ktx_ref, vx_ref, o_ref,
                 *, n_ctx):
    j = pl.program_id(1)
    nj = pl.num_programs(1)
    T = A_BLOCK
    NQ = ATT_Q_BLOCKS
    n_loc = 3 * T
    n_key = n_loc + n_ctx
    half = LANES // 2
    i_idx = lax.broadcasted_iota(jnp.int32, (T, T), 0)
    r_idx = lax.broadcasted_iota(jnp.int32, (T, T), 1)
    lane_v = lax.broadcasted_iota(jnp.int32, (1, LANES), 1)
    low = lane_v < half
    zk = jnp.zeros((A_DH, n_key), BF16)
    ones_top = jnp.broadcast_to(jnp.where(low, 1.0, 0.0).astype(BF16), (n_key, LANES))
    ones_bot = jnp.broadcast_to(jnp.where(low, 0.0, 1.0).astype(BF16), (n_key, LANES))
    for t in range(NQ):
        rows = slice(t * T, (t + 1) * T)
        prev_ok = (r_idx >= i_idx) & (j > 0) if t == 0 else r_idx >= i_idx
        next_ok = (r_idx <= i_idx) & (j < nj - 1) if t == NQ - 1 else r_idx <= i_idx
        for k in range(A_KV_HEADS):
            ks = slice(k * A_DH, (k + 1) * A_DH)
            kt_prev = ktp_ref[0, ks, :] if t == 0 else ktm_ref[t - 1, ks, :]
            kt_next = ktn_ref[0, ks, :] if t == NQ - 1 else ktm_ref[t + 1, ks, :]
            v_prev = vp_ref[...] if t == 0 else vm_ref[(t - 1) * T:t * T, :]
            v_next = vn_ref[...] if t == NQ - 1 else vm_ref[(t + 1) * T:(t + 2) * T, :]
            v_rows = (v_prev, vm_ref[rows, :], v_next, vx_ref[...])
            kt_all = jnp.concatenate([kt_prev, ktm_ref[t, ks, :], kt_next]
                                     + [ktx_ref[cc, ks, :] for cc in range(n_ctx // T)], axis=1)
            rhs = jnp.concatenate([jnp.concatenate([kt_all, zk], axis=1),
                                   jnp.concatenate([zk, kt_all], axis=1)], axis=0)
            top_sl = slice(0, LANES) if k == 0 else slice(LANES, 2 * LANES)
            bot_sl = slice(LANES, 2 * LANES) if k == 0 else slice(0, LANES)
            v_top = jnp.concatenate([jnp.where(low, vr[:, top_sl], 0) for vr in v_rows], axis=0)
            v_bot = jnp.concatenate([jnp.where(low, 0, vr[:, bot_sl]) for vr in v_rows], axis=0)
            v2 = jnp.concatenate([jnp.concatenate([v_top.astype(BF16), ones_top], axis=1),
                                  jnp.concatenate([v_bot.astype(BF16), ones_bot], axis=1)], axis=0)
            for p in range(A_HEADS // A_KV_HEADS // 2):
                g = k * (A_HEADS // A_KV_HEADS // 2) + p
                q2 = q_ref[rows, g * LANES:(g + 1) * LANES]
                s2 = jnp.dot(q2, rhs, preferred_element_type=F32)
                es = []
                sink_terms = []
                for u in range(2):
                    base = u * n_key
                    sink = sink_ref[2 * g + u]
                    s_prev = jnp.where(prev_ok, s2[:, base:base + T], -jnp.inf)
                    s_cur = s2[:, base + T:base + 2 * T]
                    s_next = jnp.where(next_ok, s2[:, base + 2 * T:base + 3 * T], -jnp.inf)
                    s_ctx = s2[:, base + n_loc:base + n_key]
                    m = jnp.maximum(jnp.maximum(s_prev, s_cur), s_next)
                    for cc in range(n_ctx // T):
                        m = jnp.maximum(m, s_ctx[:, cc * T:(cc + 1) * T])
                    m = jnp.maximum(jnp.max(m, axis=1, keepdims=True), sink)
                    es += [jnp.exp(s_prev - m), jnp.exp(s_cur - m), jnp.exp(s_next - m), jnp.exp(s_ctx - m)]
                    sink_terms.append(jnp.exp(sink - m))
                p2 = jnp.concatenate(es, axis=1).astype(BF16)
                o = jnp.dot(p2, v2, preferred_element_type=F32)
                den = o[:, LANES:] + jnp.where(low, sink_terms[0], sink_terms[1])
                o_ref[rows, g * LANES:(g + 1) * LANES] = (o[:, :LANES] / den).astype(BF16)


def _attn_call(sink, qa, kat, va, kat_c, va_c, n_batch, seq_len, n_ctx):
    T = A_BLOCK
    NQ = ATT_Q_BLOCKS
    nb = seq_len // T
    nj = nb // NQ
    r = n_batch * seq_len
    prv = lambda b, j: b * nb + jnp.maximum(j * NQ - 1, 0)
    nxt = lambda b, j: b * nb + jnp.minimum((j + 1) * NQ, nb - 1)
    kedge = lambda f: pl.BlockSpec((1, A_KV_WIDTH, T), lambda b, j: (f(b, j), 0, 0))
    vedge = lambda f: pl.BlockSpec((T, 2 * A_KV_WIDTH), lambda b, j: (f(b, j), 0))
    return pl.pallas_call(
        functools.partial(_attn_kernel, n_ctx=n_ctx),
        grid=(n_batch, nj),
        in_specs=[pl.BlockSpec(memory_space=pltpu.SMEM),
                  pl.BlockSpec((NQ * T, A_WIDTH), lambda b, j: (b * nj + j, 0)),
                  kedge(prv), pl.BlockSpec((NQ, A_KV_WIDTH, T), lambda b, j: (b * nj + j, 0, 0)), kedge(nxt),
                  vedge(prv), pl.BlockSpec((NQ * T, 2 * A_KV_WIDTH), lambda b, j: (b * nj + j, 0)), vedge(nxt),
                  pl.BlockSpec((n_ctx // T, A_KV_WIDTH, T), lambda b, j: (b, 0, 0)),
                  pl.BlockSpec((n_ctx, 2 * A_KV_WIDTH), lambda b, j: (b, 0))],
        out_specs=pl.BlockSpec((NQ * T, A_WIDTH), lambda b, j: (b * nj + j, 0)),
        out_shape=jax.ShapeDtypeStruct((r, A_WIDTH), BF16),
        compiler_params=_params(2),
        name="attn",
    )(sink, qa, kat, kat, kat, va, va, va, kat_c, va_c)


def _mix_ffn_kernel(hf_ref, hb_ref, op_ref, a_ref, x_ref, mod_ref, mg_ref, wo_ref, g1_ref, b1_ref,
                    w13_hbm, w2_hbm, g2_ref, b2_ref, o_ref,
                    w13_ref, w2_ref, st13_ref, st2_ref, sem, h_ref, acc_ref, *, layer, which):
    @pl.when(pl.program_id(0) == 0)
    def _():
        _stage_ffn_weights(w13_hbm, w2_hbm, w13_ref, w2_ref, st13_ref, st2_ref, sem, layer, which)

    rows_per = hf_ref.shape[0] // MIX_PARTS
    x2 = []
    for p in range(MIX_PARTS):
        rs = slice(p * rows_per, (p + 1) * rows_per)
        h = hf_ref[rs, :].astype(F32) + hb_ref[rs, :].astype(F32)
        gate = _sigmoid(op_ref[rs, :].astype(F32))
        parts = []
        for hd in range(M_HEADS):
            sl = slice(hd * M_DH, (hd + 1) * M_DH)
            seg = h[:, sl]
            mu = jnp.mean(seg, axis=-1, keepdims=True)
            sc = seg - mu
            var = jnp.mean(sc * sc, axis=-1, keepdims=True)
            parts.append((sc * lax.rsqrt(var + LN_EPS) * mg_ref[:, sl] * gate[:, sl]).astype(BF16))
        mixed = jnp.concatenate(parts + [a_ref[rs, :]], axis=1)
        y = jnp.dot(mixed, wo_ref[...], preferred_element_type=F32)
        x2.append(_layer_norm(ALPHA * x_ref[rs, :] + mod_ref[0, 5:6, :] * y, g1_ref[...], b1_ref[...]))
    for p in range(MIX_PARTS):
        rs = slice(p * rows_per, (p + 1) * rows_per)
        o_ref[rs, :] = _ffn_body(x2[p], mod_ref, 6, w13_ref, w2_ref, g2_ref, b2_ref, h_ref.at[p], acc_ref.at[p])


MIX_PARTS = 2


def _mix_ffn_call(hf, hb, o_pre, a_l, x1, mod, mod_index, mh_g, w_out, g1, b1, ffn_w13, ffn_w2, layer, which,
                  g2, b2, tm):
    r = x1.shape[0]
    part = tm // MIX_PARTS
    scratch = _ffn_scratch(part)[:-2] + [pltpu.VMEM((MIX_PARTS, part, D_MODEL), BF16),
                                         pltpu.VMEM((MIX_PARTS, part, D_MODEL), F32)]
    row = lambda w: pl.BlockSpec((tm, w), lambda i: (i, 0))
    vec = lambda a: a.reshape(1, a.shape[-1])
    return pl.pallas_call(
        functools.partial(_mix_ffn_kernel, layer=layer, which=which),
        grid=(r // tm,),
        in_specs=[row(M_WIDTH), row(M_WIDTH), row(M_WIDTH), row(A_WIDTH), row(D_MODEL),
                  pl.BlockSpec((1, N_MOD, D_MODEL), lambda i: (mod_index(i), 0, 0)),
                  _resident((1, M_WIDTH)),
                  _resident((D_MODEL, D_MODEL)),
                  _resident((1, D_MODEL)),
                  _resident((1, D_MODEL)),
                  pl.BlockSpec(memory_space=pl.ANY),
                  pl.BlockSpec(memory_space=pl.ANY),
                  _resident((1, D_MODEL)),
                  _resident((1, D_MODEL))],
        out_specs=row(D_MODEL),
        out_shape=jax.ShapeDtypeStruct((r, D_MODEL), F32),
        scratch_shapes=scratch,
        compiler_params=_params(1),
        name="mix_ffn",
    )(hf, hb, o_pre, a_l, x1, mod, vec(mh_g), w_out, vec(g1), vec(b1), ffn_w13, ffn_w2, vec(g2), vec(b2))


def _rope_tables(seq_len):
    t = np.arange(seq_len)
    row = (t // GRID_W).astype(np.float64)
    col = (t % GRID_W).astype(np.float64)
    n_freq = A_DH // 4
    inv = ROPE_BASE ** (-np.arange(n_freq, dtype=np.float64) / n_freq)
    ang = np.concatenate([row[:, None] * inv, col[:, None] * inv], -1)
    cos, sin = np.cos(ang), np.sin(ang)
    return (jnp.asarray(np.tile(cos, (1, 4)), F32),
            jnp.asarray(np.tile(np.concatenate([-sin, sin], -1), (1, 2)), F32))


def _tail_selection():
    sel = np.zeros((N_IN - OFF_G, N_WT), np.float32)
    src = np.arange(OFF_QA, N_IN) - OFF_G
    sel[src, np.arange(src.size)] = 1.0
    va = np.arange(OFF_VA, N_IN) - OFF_G
    sel[np.concatenate([va[A_DH:], va[:A_DH]]), A_WIDTH + 2 * A_KV_WIDTH + np.arange(A_KV_WIDTH)] = 1.0
    g = np.arange(N_GATES)
    sel[g, OFF_TG + (g // M_HEADS) * GATE_GROUP + g % M_HEADS] = 1.0
    return jnp.asarray(sel)


def _spread_gates(a):
    lead = a.shape[:-1]
    a = a.reshape(lead + (4, M_HEADS))
    a = jnp.pad(a, [(0, 0)] * len(lead) + [(0, 0), (0, GATE_GROUP - M_HEADS)])
    a = a.reshape(lead + (4 * GATE_GROUP,))
    return jnp.pad(a, [(0, 0)] * len(lead) + [(0, LANES - 4 * GATE_GROUP)])


def kernel(x, c, ctx, c_ctx, w_ada, b_ada, ln_g, ln_b, ffn_w13, ffn_w2, w_in, b_gates, conv_w, conv_b,
           mh_norm_g, attn_sink, w_out):
    n_batch, seq_len, d = x.shape
    n_ctx = ctx.shape[1]
    l = 0
    tm = 512
    tm_ctx = n_ctx
    tiles_per_batch = seq_len // tm

    cc = jnp.concatenate([c, c_ctx[None, :], jnp.zeros((SUBLANES - n_batch - 1, d), F32)], axis=0)
    mod = _mod_call(cc, w_ada[l], b_ada[l])[:n_batch + 1].reshape(n_batch + 1, N_MOD, d)
    lat_mod = lambda i: i // tiles_per_batch
    ctx_mod = lambda i: n_batch

    wa = w_in[l, :, OFF_QM:OFF_G].astype(BF16)
    wt = jnp.dot(w_in[l, :, OFF_G:], _tail_selection(), precision=lax.Precision.HIGHEST).astype(BF16)
    bg = _spread_gates(b_gates[l]).reshape(1, LANES)
    cb = conv_b[l].reshape(1, 2 * M_WIDTH)
    cos, sin = _rope_tables(seq_len)

    x_lat = x.reshape(n_batch * seq_len, d)
    x_ctx = ctx.reshape(n_batch * n_ctx, d)

    x1, xc1 = _ffn_call(x_lat, x_ctx, mod, tiles_per_batch, ffn_w13, ffn_w2, l, 0, ln_g[l, 0], ln_b[l, 0], tm)

    (_, kt_c, v_c, _, grow_c, _, kat_c, va_c) = _inproj_call(
        xc1, mod, ctx_mod, wa, wt, bg, conv_w[l], cb, None, None, tm_ctx, n_ctx)
    (q_l, kt_l, v_l, o_pre, grow_l, qa_l, kat_l, va_l) = _inproj_call(
        x1, mod, lat_mod, wa, wt, bg, conv_w[l], cb, cos, sin, tm, seq_len)

    c0 = jnp.zeros((n_batch * N_STREAMS, M_DH, M_AUG), F32)
    m0 = jnp.zeros((n_batch * N_STREAMS, LANES), F32)
    c_st, m_st = _mlstm_call(None, kt_c, v_c.reshape(n_batch, n_ctx, M_WIDTH), grow_c, c0, m0,
                             n_batch, n_ctx, False)
    hf, hb, _, _ = _mlstm_call(q_l.reshape(n_batch, seq_len, M_WIDTH), kt_l,
                               v_l.reshape(n_batch, seq_len, M_WIDTH), grow_l, c_st, m_st, n_batch, seq_len, True)
    hf = hf.reshape(n_batch * seq_len, M_WIDTH)
    hb = hb.reshape(n_batch * seq_len, M_WIDTH)

    a_l = _attn_call(attn_sink[l], qa_l, kat_l, va_l, kat_c, va_c, n_batch, seq_len, n_ctx)

    tm_mix = MIX_PARTS * tm
    out = _mix_ffn_call(hf, hb, o_pre, a_l, x1, mod, lambda i: i // (seq_len // tm_mix), mh_norm_g[l],
                        w_out[l].astype(BF16), ln_g[l, 1], ln_b[l, 1], ffn_w13, ffn_w2, l, 1,
                        ln_g[l, 2], ln_b[l, 2], tm_mix)
    return out.reshape(n_batch, seq_len, d)
```

```python
import functools

import jax
import jax.numpy as jnp
import numpy as np
from jax import lax
from jax.experimental import pallas as pl
from jax.experimental.pallas import tpu as pltpu

F32 = jnp.float32
BF16 = jnp.bfloat16

D_MODEL = 1024
GRID_W = 64
M_HEADS = 4
M_DH = 128
M_WIDTH = M_HEADS * M_DH
A_HEADS = 8
A_KV_HEADS = 2
A_DH = 64
A_WIDTH = A_HEADS * A_DH
A_KV_WIDTH = A_KV_HEADS * A_DH
A_BLOCK = 128
CONV_W = 5
D_FF = 2816
ROPE_BASE = 10000.0
N_MOD = 9
LN_EPS = 1e-5
FFN_RES = 0.5
DEPTH = 1
ALPHA = (2.0 * DEPTH) ** 0.25

OFF_QM = 0
OFF_KM = OFF_QM + M_WIDTH
OFF_VM = OFF_KM + M_WIDTH
OFF_OM = OFF_VM + M_WIDTH
OFF_G = OFF_OM + M_WIDTH
N_GATES = 4 * M_HEADS
OFF_QA = OFF_G + N_GATES
OFF_KA = OFF_QA + A_WIDTH
OFF_VA = OFF_KA + A_KV_WIDTH
N_IN = OFF_VA + A_KV_WIDTH

LANES = 128
SUBLANES = 8
MXU_TILE = 256
VMEM_BYTES = 64 * 1024 * 1024
VMEM_LIMIT = VMEM_BYTES * 7 // 8
HALO = SUBLANES
MLSTM_CHUNK = LANES
MLSTM_STEP_CHUNKS = 4
FF_CHUNK = MXU_TILE
COL_BLOCK = MXU_TILE
ROW_PART = LANES
MIX_PARTS = 2
ATT_Q_BLOCKS = 16
MOD_COLS = 2304
GATE_GROUP = SUBLANES


def _sigmoid(x):
    return 1.0 / (1.0 + jnp.exp(-x))


def _log_sigmoid(x):
    return jnp.minimum(x, 0.0) - jnp.log1p(jnp.exp(-jnp.abs(x)))


def _layer_norm(z, g, b):
    mu = jnp.mean(z, axis=-1, keepdims=True)
    zc = z - mu
    var = jnp.mean(zc * zc, axis=-1, keepdims=True)
    return zc * lax.rsqrt(var + LN_EPS) * g + b


def _params(n_grid):
    return pltpu.CompilerParams(dimension_semantics=("arbitrary",) * n_grid, vmem_limit_bytes=VMEM_LIMIT)


def _resident(shape):
    nd = len(shape)
    return pl.BlockSpec(shape, lambda *_: (0,) * nd, pipeline_mode=pl.Buffered(1))


def _mod_kernel(c_ref, w_ref, b_ref, o_ref):
    c = c_ref[...]
    s = (c * _sigmoid(c)).astype(BF16)
    o_ref[...] = jnp.dot(s, w_ref[...].astype(BF16), preferred_element_type=F32) + b_ref[...]


def _mod_call(cc, w_ada, b_ada):
    n = w_ada.shape[1]
    tn = MOD_COLS
    return pl.pallas_call(
        _mod_kernel,
        grid=(n // tn,),
        in_specs=[pl.BlockSpec((SUBLANES, D_MODEL), lambda j: (0, 0)),
                  pl.BlockSpec((D_MODEL, tn), lambda j: (0, j)),
                  pl.BlockSpec((1, tn), lambda j: (0, j))],
        out_specs=pl.BlockSpec((SUBLANES, tn), lambda j: (0, j)),
        out_shape=jax.ShapeDtypeStruct((SUBLANES, n), F32),
        compiler_params=_params(1),
        name="mod",
    )(cc, w_ada, b_ada.reshape(1, n))


def _ffn_body(x, mod_ref, k0, w13_ref, w2_ref, g_ref, b_ref, h_ref, acc_ref):
    shift = mod_ref[0, k0:k0 + 1, :]
    scale = mod_ref[0, k0 + 1:k0 + 2, :]
    gate = mod_ref[0, k0 + 2:k0 + 3, :]
    h_ref[...] = (x * (1.0 + scale) + shift).astype(BF16)
    for j in range(D_FF // FF_CHUNK):
        lo = j * FF_CHUNK
        h = h_ref[...]
        a1 = jnp.dot(h, w13_ref[:, lo:lo + FF_CHUNK], preferred_element_type=F32)
        a3 = jnp.dot(h, w13_ref[:, D_FF + lo:D_FF + lo + FF_CHUNK], preferred_element_type=F32)
        g = (a1 * _sigmoid(a1) * a3).astype(BF16)
        y = jnp.dot(g, w2_ref[lo:lo + FF_CHUNK, :], preferred_element_type=F32)
        if j == 0:
            acc_ref[...] = y
        else:
            acc_ref[...] += y
    z = ALPHA * x + (FFN_RES * gate) * acc_ref[...]
    return _layer_norm(z, g_ref[...], b_ref[...])


W13_STAGE_COLS = 512
W2_STAGE_ROWS = 256


def _stage_as_bf16(chunks, sem):
    def copy(n):
        src, stage, _, _ = chunks[n]
        return pltpu.make_async_copy(src, stage.at[n % 2], sem.at[n % 2])

    copy(0).start()
    for n, (_, stage, dst, idx) in enumerate(chunks):
        if n + 1 < len(chunks):
            copy(n + 1).start()
        copy(n).wait()
        dst[idx] = stage[n % 2].astype(BF16)


def _stage_ffn_weights(w13_hbm, w2_hbm, w13_ref, w2_ref, st13_ref, st2_ref, sem, layer, which):
    chunks = []
    for c in range(2 * D_FF // W13_STAGE_COLS):
        lo = c * W13_STAGE_COLS
        chunks.append((w13_hbm.at[layer, which, :, lo:lo + W13_STAGE_COLS], st13_ref, w13_ref,
                       (slice(None), slice(lo, lo + W13_STAGE_COLS))))
    for c in range(D_FF // W2_STAGE_ROWS):
        lo = c * W2_STAGE_ROWS
        chunks.append((w2_hbm.at[layer, which, lo:lo + W2_STAGE_ROWS, :], st2_ref, w2_ref,
                       (slice(lo, lo + W2_STAGE_ROWS), slice(None))))
    _stage_as_bf16(chunks, sem)


def _ffn_kernel(x_ref, xc_ref, mod_ref, w13_hbm, w2_hbm, g_ref, b_ref, o_ref, oc_ref,
                w13_ref, w2_ref, st13_ref, st2_ref, sem, h_ref, acc_ref, *, layer, which):
    i = pl.program_id(0)
    n_lat = pl.num_programs(0) - 1

    @pl.when(i == 0)
    def _():
        _stage_ffn_weights(w13_hbm, w2_hbm, w13_ref, w2_ref, st13_ref, st2_ref, sem, layer, which)

    x = jnp.where(i < n_lat, x_ref[...], xc_ref[...])
    y = _ffn_body(x, mod_ref, 0, w13_ref, w2_ref, g_ref, b_ref, h_ref, acc_ref)

    @pl.when(i < n_lat)
    def _():
        o_ref[...] = y

    @pl.when(i == n_lat)
    def _():
        oc_ref[...] = y


def _ffn_scratch(rows):
    rows = rows if isinstance(rows, tuple) else (rows,)
    return [pltpu.VMEM((D_MODEL, 2 * D_FF), BF16), pltpu.VMEM((D_FF, D_MODEL), BF16),
            pltpu.VMEM((2, D_MODEL, W13_STAGE_COLS), F32), pltpu.VMEM((2, W2_STAGE_ROWS, D_MODEL), F32),
            pltpu.SemaphoreType.DMA((2,)),
            pltpu.VMEM(rows + (D_MODEL,), BF16), pltpu.VMEM(rows + (D_MODEL,), F32)]


def _ffn_call(x, x_ctx, mod, tiles_per_batch, ffn_w13, ffn_w2, layer, which, ln_g, ln_b, tm):
    r = x.shape[0]
    n_lat = r // tm
    assert x_ctx.shape[0] == tm
    lat = lambda i: jnp.minimum(i, n_lat - 1)
    return pl.pallas_call(
        functools.partial(_ffn_kernel, layer=layer, which=which),
        grid=(n_lat + 1,),
        in_specs=[pl.BlockSpec((tm, D_MODEL), lambda i: (lat(i), 0)),
                  pl.BlockSpec((tm, D_MODEL), lambda i: (0, 0)),
                  pl.BlockSpec((1, N_MOD, D_MODEL), lambda i: (i // tiles_per_batch, 0, 0)),
                  pl.BlockSpec(memory_space=pl.ANY),
                  pl.BlockSpec(memory_space=pl.ANY),
                  _resident((1, D_MODEL)),
                  _resident((1, D_MODEL))],
        out_specs=[pl.BlockSpec((tm, D_MODEL), lambda i: (lat(i), 0)),
                   pl.BlockSpec((tm, D_MODEL), lambda i: (0, 0))],
        out_shape=[jax.ShapeDtypeStruct((r, D_MODEL), F32), jax.ShapeDtypeStruct((tm, D_MODEL), F32)],
        scratch_shapes=_ffn_scratch(tm),
        compiler_params=_params(1),
        name="ffn",
    )(x, x_ctx, mod, ffn_w13, ffn_w2, ln_g.reshape(1, D_MODEL), ln_b.reshape(1, D_MODEL))


N_WA = 2 * M_WIDTH + 2 * M_WIDTH
OFF_TG = A_WIDTH + A_KV_WIDTH + 2 * A_KV_WIDTH
N_WT = OFF_TG + LANES
ROW_A = (0, 3 * GATE_GROUP)
ROW_B = (GATE_GROUP, 4 * GATE_GROUP)
ROW_C = (2 * GATE_GROUP, 5 * GATE_GROUP)
GATE_ROWS = 6 * GATE_GROUP


def _scan(x, lane, op, reverse):
    fill = 0.0 if op is jnp.add else -jnp.inf
    k = 1
    while k < LANES:
        if reverse:
            other = jnp.where(lane < LANES - k, pltpu.roll(x, LANES - k, 1), fill)
        else:
            other = jnp.where(lane >= k, pltpu.roll(x, k, 1), fill)
        x = op(x, other)
        k *= 2
    return x


def _chunk_scan(x, lane, op, reverse):
    return jnp.concatenate([_scan(x[:, c * LANES:(c + 1) * LANES], lane, op, reverse)
                            for c in range(x.shape[1] // LANES)], axis=1)


def _rope(x, cos, sin_signed, first_half):
    swapped = jnp.where(first_half, pltpu.roll(x, LANES - A_DH // 2, 1), pltpu.roll(x, A_DH // 2, 1))
    return x * cos + swapped * sin_signed


def _inproj_kernel(*refs, tiles_per_seq, rope, tm):
    n_qk = 2 * M_WIDTH // COL_BLOCK
    raw_refs = refs[-n_qk:]
    refs = refs[:-n_qk]
    if rope:
        (xp_ref, x_ref, xn_ref, mod_ref, wa_ref, wt_ref, bg_ref, cw_ref, cb_ref, cos_ref, sin_ref,
         q_ref, kt_ref, v_ref, o_ref, grow_ref, qa_ref, kat_ref, va_ref, h_ref, pt_ref) = refs
    else:
        (xp_ref, x_ref, xn_ref, mod_ref, wa_ref, wt_ref, bg_ref, cw_ref, cb_ref,
         q_ref, kt_ref, v_ref, o_ref, grow_ref, qa_ref, kat_ref, va_ref, h_ref, pt_ref) = refs
    pos = pl.program_id(0) % tiles_per_seq
    slot0 = jnp.minimum(pl.program_id(0), 0)
    n_ext = tm + 2 * HALO
    shift = mod_ref[0, 3:4, :]
    scale1 = 1.0 + mod_ref[0, 4:5, :]
    h_ref[0:tm, :] = (x_ref[...] * scale1 + shift).astype(BF16)
    xh = jnp.concatenate([xp_ref[...], xn_ref[...]], axis=0)
    h_ref[tm:n_ext, :] = (xh * scale1 + shift).astype(BF16)

    def mm(w_ref, lo):
        return jnp.dot(h_ref[0:tm, :], w_ref[:, lo:lo + COL_BLOCK], preferred_element_type=F32)

    def qk_matmul(blk):
        cols = slice(blk * COL_BLOCK, (blk + 1) * COL_BLOCK)
        y = jnp.dot(h_ref[...], wa_ref[:, cols], preferred_element_type=F32)
        raw_refs[blk][0, 0:HALO, :] = jnp.where(pos == 0, 0.0, y[tm:tm + HALO])
        raw_refs[blk][0, HALO:HALO + tm, :] = y[0:tm]
        raw_refs[blk][0, HALO + tm:n_ext, :] = jnp.where(pos == tiles_per_seq - 1, 0.0, y[tm + HALO:n_ext])

    def qk_epilogue(blk, part):
        cols = slice(blk * COL_BLOCK, (blk + 1) * COL_BLOCK)
        r0 = part * ROW_PART
        base = HALO - CONV_W // 2 + r0
        acc = cb_ref[:, cols] + cw_ref[0:1, cols] * raw_refs[blk][slot0, pl.ds(base, ROW_PART), :]
        for j in range(1, CONV_W):
            acc = acc + cw_ref[j:j + 1, cols] * raw_refs[blk][slot0, pl.ds(base + j, ROW_PART), :]
        qk = acc * _sigmoid(acc)
        if blk < M_WIDTH // COL_BLOCK:
            q_ref[r0:r0 + ROW_PART, cols] = (qk * (M_DH ** -0.5)).astype(BF16)
        else:
            kt_ref[0, part, blk * COL_BLOCK - M_WIDTH:(blk + 1) * COL_BLOCK - M_WIDTH, :] = qk.T.astype(BF16)

    def v_matmul(blk):
        v_ref[:, blk * COL_BLOCK:(blk + 1) * COL_BLOCK] = mm(wa_ref, 2 * M_WIDTH + blk * COL_BLOCK).astype(BF16)

    def o_matmul(blk):
        o_ref[:, blk * COL_BLOCK:(blk + 1) * COL_BLOCK] = mm(wa_ref, 3 * M_WIDTH + blk * COL_BLOCK).astype(BF16)

    def t_matmul(blk):
        pt_ref[:, blk * COL_BLOCK:(blk + 1) * COL_BLOCK] = mm(wt_ref, blk * COL_BLOCK)

    def pt_cols(lo, width):
        return pt_ref[:, lo:lo + width]

    def gate_epilogue():
        pgt = (pt_cols(OFF_TG, LANES) + bg_ref[...]).T
        li_f = pgt[0:GATE_GROUP]
        lf_f = _log_sigmoid(pgt[GATE_GROUP:2 * GATE_GROUP])
        li_b = pgt[2 * GATE_GROUP:3 * GATE_GROUP]
        lf_b = _log_sigmoid(pgt[3 * GATE_GROUP:4 * GATE_GROUP])
        lane = lax.broadcasted_iota(jnp.int32, (GATE_GROUP, LANES), 1)
        b_f = _chunk_scan(lf_f, lane, jnp.add, False)
        e_b = _chunk_scan(lf_b, lane, jnp.add, True)
        a_f = li_f - b_f
        a_b = li_b - e_b
        rows = jnp.concatenate([a_f, b_f, _chunk_scan(a_f, lane, jnp.maximum, False),
                                a_b, e_b, _chunk_scan(a_b, lane, jnp.maximum, True)], axis=0)
        for c in range(tm // MLSTM_CHUNK):
            grow_ref[0, c] = rows[:, c * MLSTM_CHUNK:(c + 1) * MLSTM_CHUNK]
        va_ref[:, A_KV_WIDTH:] = pt_cols(A_WIDTH + 2 * A_KV_WIDTH, A_KV_WIDTH).astype(BF16)

    def rotate(lo):
        xg = pt_cols(lo, LANES)
        if not rope:
            return xg
        lane_t = lax.broadcasted_iota(jnp.int32, (tm, LANES), 1)
        return _rope(xg, cos_ref[...], sin_ref[...], (lane_t % A_DH) < (A_DH // 2))

    def qa_epilogue(blk):
        for g in range(blk * COL_BLOCK // LANES, (blk + 1) * COL_BLOCK // LANES):
            qa_ref[:, g * LANES:(g + 1) * LANES] = (rotate(g * LANES) * (A_DH ** -0.5)).astype(BF16)

    def kv_epilogue():
        kat = rotate(A_WIDTH).T.astype(BF16)
        for c in range(tm // A_BLOCK):
            kat_ref[c] = kat[:, c * A_BLOCK:(c + 1) * A_BLOCK]
        va_ref[:, :A_KV_WIDTH] = pt_cols(A_WIDTH + A_KV_WIDTH, A_KV_WIDTH).astype(BF16)

    for blk in range(n_qk):
        qk_matmul(blk)
    for blk in range(N_WT // COL_BLOCK):
        t_matmul(blk)
    for blk in range(M_WIDTH // COL_BLOCK):
        v_matmul(blk)
        o_matmul(blk)
    for blk in range(n_qk):
        for part in range(tm // ROW_PART):
            qk_epilogue(blk, part)
    gate_epilogue()
    qa_epilogue(0)
    qa_epilogue(1)
    kv_epilogue()


def _inproj_call(x, mod, mod_index, wa, wt, bg, conv_w, conv_b, cos, sin, tm, seq_len):
    r = x.shape[0]
    tiles_per_seq = seq_len // tm
    rope = cos is not None
    hb = tm // HALO
    n_halo = r // HALO
    in_specs = [pl.BlockSpec((HALO, D_MODEL), lambda i: (jnp.maximum(i * hb - 1, 0), 0)),
                pl.BlockSpec((tm, D_MODEL), lambda i: (i, 0)),
                pl.BlockSpec((HALO, D_MODEL), lambda i: (jnp.minimum((i + 1) * hb, n_halo - 1), 0)),
                pl.BlockSpec((1, N_MOD, D_MODEL), lambda i: (mod_index(i), 0, 0)),
                _resident((D_MODEL, N_WA)),
                _resident((D_MODEL, N_WT)),
                _resident((1, LANES)),
                _resident((CONV_W, 2 * M_WIDTH)),
                _resident((1, 2 * M_WIDTH))]
    args = [x, x, x, mod, wa, wt, bg, conv_w, conv_b]
    if rope:
        in_specs += [pl.BlockSpec((tm, LANES), lambda i: (i % tiles_per_seq, 0)),
                     pl.BlockSpec((tm, LANES), lambda i: (i % tiles_per_seq, 0))]
        args += [cos, sin]
    assert MLSTM_CHUNK == A_BLOCK == ROW_PART
    n_seq = r // seq_len
    cps = seq_len // MLSTM_CHUNK
    cpt = tm // MLSTM_CHUNK
    row = lambda w: pl.BlockSpec((tm, w), lambda i: (i, 0))
    seq_col = lambda h: pl.BlockSpec((1, cpt, h, MLSTM_CHUNK),
                                     lambda i: (i // tiles_per_seq, i % tiles_per_seq, 0, 0))
    out_specs = [row(M_WIDTH), seq_col(M_WIDTH), row(M_WIDTH), row(M_WIDTH), seq_col(GATE_ROWS),
                 row(A_WIDTH), pl.BlockSpec((cpt, A_KV_WIDTH, A_BLOCK), lambda i: (i, 0, 0)),
                 row(2 * A_KV_WIDTH)]
    out_shape = [jax.ShapeDtypeStruct((r, M_WIDTH), BF16),
                 jax.ShapeDtypeStruct((n_seq, cps, M_WIDTH, MLSTM_CHUNK), BF16),
                 jax.ShapeDtypeStruct((r, M_WIDTH), BF16),
                 jax.ShapeDtypeStruct((r, M_WIDTH), BF16),
                 jax.ShapeDtypeStruct((n_seq, cps, GATE_ROWS, MLSTM_CHUNK), F32),
                 jax.ShapeDtypeStruct((r, A_WIDTH), BF16),
                 jax.ShapeDtypeStruct((r // A_BLOCK, A_KV_WIDTH, A_BLOCK), BF16),
                 jax.ShapeDtypeStruct((r, 2 * A_KV_WIDTH), BF16)]
    return pl.pallas_call(
        functools.partial(_inproj_kernel, tiles_per_seq=tiles_per_seq, rope=rope, tm=tm),
        grid=(r // tm,),
        in_specs=in_specs,
        out_specs=out_specs,
        out_shape=out_shape,
        scratch_shapes=([pltpu.VMEM((tm + 2 * HALO, D_MODEL), BF16), pltpu.VMEM((tm, N_WT), F32)]
                        + [pltpu.VMEM((1, tm + 2 * HALO, COL_BLOCK), F32)] * (2 * M_WIDTH // COL_BLOCK)),
        compiler_params=_params(1),
        name="inproj",
    )(*args)


M_AUG = 2 * M_DH
N_STREAMS = 2 * M_HEADS


def _col_replicated(row):
    return jnp.broadcast_to(row, (LANES, row.shape[1])).T


def _mlstm_kernel(*refs, emit_h):
    if emit_h:
        (qf_ref, ktf_ref, vf_ref, grf_ref, qb_ref, ktb_ref, vb_ref, grb_ref, c0_ref, m0_ref,
         hf_ref, hb_ref, cout_ref, mout_ref, c_ref, m_ref) = refs
        q_refs, h_refs = (qf_ref, qb_ref), (hf_ref, hb_ref)
    else:
        (ktf_ref, vf_ref, grf_ref, ktb_ref, vb_ref, grb_ref, c0_ref, m0_ref,
         cout_ref, mout_ref, c_ref, m_ref) = refs
    kt_refs, v_refs, gr_refs = (ktf_ref, ktb_ref), (vf_ref, vb_ref), (grf_ref, grb_ref)
    L = MLSTM_CHUNK
    n_batch = c_ref.shape[0] // N_STREAMS
    n_sub = grf_ref.shape[1]
    c = pl.program_id(0)

    @pl.when(c == 0)
    def _():
        c_ref[...] = c0_ref[...]
        m_ref[...] = m0_ref[...]

    t_idx = lax.broadcasted_iota(jnp.int32, (L, L), 0)
    s_idx = lax.broadcasted_iota(jnp.int32, (L, L), 1)
    masks = (s_idx <= t_idx, s_idx >= t_idx)
    ones = jnp.ones((L, M_DH), BF16)
    zeros_k = jnp.zeros((M_DH, L), BF16)
    m_all = m_ref[...]
    m_cur = [m_all[r:r + 1] for r in range(n_batch * N_STREAMS)]
    for sub in range(n_sub):
        for b in range(n_batch):
            for d in range(2):
                last = (L - 1, 0)[d]
                ch = sub if d == 0 else n_sub - 1 - sub
                tok = slice(ch * L, (ch + 1) * L)
                gr = gr_refs[d][b, ch]
                for pair in range(M_HEADS // 2):
                    if emit_h:
                        rows = slice(2 * pair * M_DH, (2 * pair + 2) * M_DH)
                        kt2 = kt_refs[d][b, ch, rows, :]
                        rhs = jnp.concatenate(
                            [jnp.concatenate([kt2[:M_DH], zeros_k], axis=1),
                             jnp.concatenate([zeros_k, kt2[M_DH:]], axis=1)], axis=0)
                        s2 = jnp.dot(q_refs[d][b, tok, rows], rhs, preferred_element_type=F32)
                    for hh in range(2):
                        h = 2 * pair + hh
                        r = (b * 2 + d) * M_HEADS + h
                        sl = slice(h * M_DH, (h + 1) * M_DH)
                        a_row = gr[ROW_A[d] + h:ROW_A[d] + h + 1]
                        b_row = gr[ROW_B[d] + h:ROW_B[d] + h + 1]
                        m_row = m_cur[r]
                        ct = c_ref[r]
                        kt = kt_refs[d][b, ch, sl, :]
                        v_aug = jnp.concatenate([v_refs[d][b, tok, sl], ones], axis=1)
                        if emit_h:
                            mu_row = jnp.maximum(m_row, gr[ROW_C[d] + h:ROW_C[d] + h + 1])
                            mu = _col_replicated(mu_row)
                            mt = _col_replicated(b_row + mu_row)
                            dmat = jnp.where(masks[d], jnp.exp(a_row - mu), 0.0)
                            p = (s2[:, hh * L:(hh + 1) * L] * dmat).astype(BF16)
                            qs = (q_refs[d][b, tok, sl].astype(F32) * jnp.exp(m_row - mu)).astype(BF16)
                            tot = jnp.dot(jnp.concatenate([p, qs], axis=1),
                                          jnp.concatenate([v_aug, ct.astype(BF16)], axis=0),
                                          preferred_element_type=F32)
                            h_dir = tot[:, :M_DH] / jnp.maximum(jnp.abs(tot[:, M_DH:]), jnp.exp(-mt))
                            h_refs[d][b, tok, sl] = h_dir.astype(BF16)
                        mp = jnp.maximum(m_row, jnp.max(a_row, axis=1, keepdims=True))
                        ktw = (kt.astype(F32) * jnp.exp(a_row - mp)).astype(BF16)
                        decay = jnp.exp(m_row - mp)
                        c_ref[r] = (jnp.concatenate([decay, decay], axis=1) * ct
                                    + jnp.dot(ktw, v_aug, preferred_element_type=F32))
                        m_cur[r] = b_row[:, last:last + 1] + mp
    m_ref[...] = jnp.concatenate(m_cur, axis=0)

    @pl.when(c == pl.num_programs(0) - 1)
    def _():
        cout_ref[...] = c_ref[...]
        mout_ref[...] = m_ref[...]


def _mlstm_call(q, kt, v, grow, c0, m0, n_batch, seq_len, emit_h):
    L = MLSTM_CHUNK
    n_sub = min(MLSTM_STEP_CHUNKS, seq_len // L)
    nc = seq_len // (L * n_sub)
    fwd = lambda c: c
    bwd = lambda c: nc - 1 - c
    row = lambda w, f: pl.BlockSpec((n_batch, n_sub * L, w), lambda c: (0, f(c), 0))
    col = lambda hgt, f: pl.BlockSpec((n_batch, n_sub, hgt, L), lambda c: (0, f(c), 0, 0))
    n_str = n_batch * N_STREAMS
    state_specs = [pl.BlockSpec((n_str, M_DH, M_AUG), lambda c: (0, 0, 0)),
                   pl.BlockSpec((n_str, LANES), lambda c: (0, 0))]
    state_shapes = [jax.ShapeDtypeStruct((n_str, M_DH, M_AUG), F32),
                    jax.ShapeDtypeStruct((n_str, LANES), F32)]
    if emit_h:
        in_specs = [row(M_WIDTH, fwd), col(M_WIDTH, fwd), row(M_WIDTH, fwd), col(GATE_ROWS, fwd),
                    row(M_WIDTH, bwd), col(M_WIDTH, bwd), row(M_WIDTH, bwd), col(GATE_ROWS, bwd)]
        args = [q, kt, v, grow, q, kt, v, grow]
        out_specs = [row(M_WIDTH, fwd), row(M_WIDTH, bwd)] + state_specs
        out_shape = [jax.ShapeDtypeStruct((n_batch, seq_len, M_WIDTH), BF16)] * 2 + state_shapes
    else:
        in_specs = [col(M_WIDTH, fwd), row(M_WIDTH, fwd), col(GATE_ROWS, fwd),
                    col(M_WIDTH, bwd), row(M_WIDTH, bwd), col(GATE_ROWS, bwd)]
        args = [kt, v, grow, kt, v, grow]
        out_specs = state_specs
        out_shape = state_shapes
    return pl.pallas_call(
        functools.partial(_mlstm_kernel, emit_h=emit_h),
        grid=(nc,),
        in_specs=in_specs + state_specs,
        out_specs=out_specs,
        out_shape=out_shape,
        scratch_shapes=[pltpu.VMEM((n_str, M_DH, M_AUG), F32), pltpu.VMEM((n_str, LANES), F32)],
        compiler_params=_params(1),
        name="mlstm" if emit_h else "mlstm_ctx",
    )(*args, c0, m0)


def _attn_kernel(sink_ref, q_ref, ktp_ref, ktm_ref, ktn_ref, vp_ref, vm_ref, vn_ref, ktx_ref, vx_ref, o_ref,
                 *, n_ctx):
    j = pl.program_id(1)
    nj = pl.num_programs(1)
    T = A_BLOCK
    NQ = ATT_Q_BLOCKS
    n_loc = 3 * T
    n_key = n_loc + n_ctx
    half = LANES // 2
    i_idx = lax.broadcasted_iota(jnp.int32, (T, T), 0)
    r_idx = lax.broadcasted_iota(jnp.int32, (T, T), 1)
    lane_v = lax.broadcasted_iota(jnp.int32, (1, LANES), 1)
    low = lane_v < half
    zk = jnp.zeros((A_DH, n_key), BF16)
    ones_top = jnp.broadcast_to(jnp.where(low, 1.0, 0.0).astype(BF16), (n_key, LANES))
    ones_bot = jnp.broadcast_to(jnp.where(low, 0.0, 1.0).astype(BF16), (n_key, LANES))
    for t in range(NQ):
        rows = slice(t * T, (t + 1) * T)
        prev_ok = (r_idx >= i_idx) & (j > 0) if t == 0 else r_idx >= i_idx
        next_ok = (r_idx <= i_idx) & (j < nj - 1) if t == NQ - 1 else r_idx <= i_idx
        for k in range(A_KV_HEADS):
            ks = slice(k * A_DH, (k + 1) * A_DH)
            kt_prev = ktp_ref[0, ks, :] if t == 0 else ktm_ref[t - 1, ks, :]
            kt_next = ktn_ref[0, ks, :] if t == NQ - 1 else ktm_ref[t + 1, ks, :]
            v_prev = vp_ref[...] if t == 0 else vm_ref[(t - 1) * T:t * T, :]
            v_next = vn_ref[...] if t == NQ - 1 else vm_ref[(t + 1) * T:(t + 2) * T, :]
            v_rows = (v_prev, vm_ref[rows, :], v_next, vx_ref[...])
            kt_all = jnp.concatenate([kt_prev, ktm_ref[t, ks, :], kt_next]
                                     + [ktx_ref[cc, ks, :] for cc in range(n_ctx // T)], axis=1)
            rhs = jnp.concatenate([jnp.concatenate([kt_all, zk], axis=1),
                                   jnp.concatenate([zk, kt_all], axis=1)], axis=0)
            top_sl = slice(0, LANES) if k == 0 else slice(LANES, 2 * LANES)
            bot_sl = slice(LANES, 2 * LANES) if k == 0 else slice(0, LANES)
            v_top = jnp.concatenate([jnp.where(low, vr[:, top_sl], 0) for vr in v_rows], axis=0)
            v_bot = jnp.concatenate([jnp.where(low, 0, vr[:, bot_sl]) for vr in v_rows], axis=0)
            v2 = jnp.concatenate([jnp.concatenate([v_top.astype(BF16), ones_top], axis=1),
                                  jnp.concatenate([v_bot.astype(BF16), ones_bot], axis=1)], axis=0)
            for p in range(A_HEADS // A_KV_HEADS // 2):
                g = k * (A_HEADS // A_KV_HEADS // 2) + p
                q2 = q_ref[rows, g * LANES:(g + 1) * LANES]
                s2 = jnp.dot(q2, rhs, preferred_element_type=F32)
                es = []
                sink_terms = []
                for u in range(2):
                    base = u * n_key
                    sink = sink_ref[2 * g + u]
                    s_prev = jnp.where(prev_ok, s2[:, base:base + T], -jnp.inf)
                    s_cur = s2[:, base + T:base + 2 * T]
                    s_next = jnp.where(next_ok, s2[:, base + 2 * T:base + 3 * T], -jnp.inf)
                    s_ctx = s2[:, base + n_loc:base + n_key]
                    m = jnp.maximum(jnp.maximum(s_prev, s_cur), s_next)
                    for cc in range(n_ctx // T):
                        m = jnp.maximum(m, s_ctx[:, cc * T:(cc + 1) * T])
                    m = jnp.maximum(jnp.max(m, axis=1, keepdims=True), sink)
                    es += [jnp.exp(s_prev - m), jnp.exp(s_cur - m), jnp.exp(s_next - m), jnp.exp(s_ctx - m)]
                    sink_terms.append(jnp.exp(sink - m))
                p2 = jnp.concatenate(es, axis=1).astype(BF16)
                o = jnp.dot(p2, v2, preferred_element_type=F32)
                den = o[:, LANES:] + jnp.where(low, sink_terms[0], sink_terms[1])
                o_ref[rows, g * LANES:(g + 1) * LANES] = (o[:, :LANES] / den).astype(BF16)


def _attn_call(sink, qa, kat, va, kat_c, va_c, n_batch, seq_len, n_ctx):
    T = A_BLOCK
    NQ = ATT_Q_BLOCKS
    nb = seq_len // T
    nj = nb // NQ
    r = n_batch * seq_len
    prv = lambda b, j: b * nb + jnp.maximum(j * NQ - 1, 0)
    nxt = lambda b, j: b * nb + jnp.minimum((j + 1) * NQ, nb - 1)
    kedge = lambda f: pl.BlockSpec((1, A_KV_WIDTH, T), lambda b, j: (f(b, j), 0, 0))
    vedge = lambda f: pl.BlockSpec((T, 2 * A_KV_WIDTH), lambda b, j: (f(b, j), 0))
    return pl.pallas_call(
        functools.partial(_attn_kernel, n_ctx=n_ctx),
        grid=(n_batch, nj),
        in_specs=[pl.BlockSpec(memory_space=pltpu.SMEM),
                  pl.BlockSpec((NQ * T, A_WIDTH), lambda b, j: (b * nj + j, 0)),
                  kedge(prv), pl.BlockSpec((NQ, A_KV_WIDTH, T), lambda b, j: (b * nj + j, 0, 0)), kedge(nxt),
                  vedge(prv), pl.BlockSpec((NQ * T, 2 * A_KV_WIDTH), lambda b, j: (b * nj + j, 0)), vedge(nxt),
                  pl.BlockSpec((n_ctx // T, A_KV_WIDTH, T), lambda b, j: (b, 0, 0)),
                  pl.BlockSpec((n_ctx, 2 * A_KV_WIDTH), lambda b, j: (b, 0))],
        out_specs=pl.BlockSpec((NQ * T, A_WIDTH), lambda b, j: (b * nj + j, 0)),
        out_shape=jax.ShapeDtypeStruct((r, A_WIDTH), BF16),
        compiler_params=_params(2),
        name="attn",
    )(sink, qa, kat, kat, kat, va, va, va, kat_c, va_c)


def _mix_ffn_kernel(hf_ref, hb_ref, op_ref, a_ref, x_ref, mod_ref, mg_ref, wo_ref, g1_ref, b1_ref,
                    w13_hbm, w2_hbm, g2_ref, b2_ref, o_ref,
                    w13_ref, w2_ref, st13_ref, st2_ref, sem, h_ref, acc_ref, *, layer, which):
    @pl.when(pl.program_id(0) == 0)
    def _():
        _stage_ffn_weights(w13_hbm, w2_hbm, w13_ref, w2_ref, st13_ref, st2_ref, sem, layer, which)

    rows_per = hf_ref.shape[0] // MIX_PARTS
    x2 = []
    for p in range(MIX_PARTS):
        rs = slice(p * rows_per, (p + 1) * rows_per)
        h = hf_ref[rs, :].astype(F32) + hb_ref[rs, :].astype(F32)
        gate = _sigmoid(op_ref[rs, :].astype(F32))
        parts = []
        for hd in range(M_HEADS):
            sl = slice(hd * M_DH, (hd + 1) * M_DH)
            seg = h[:, sl]
            mu = jnp.mean(seg, axis=-1, keepdims=True)
            sc = seg - mu
            var = jnp.mean(sc * sc, axis=-1, keepdims=True)
            parts.append((sc * lax.rsqrt(var + LN_EPS) * mg_ref[:, sl] * gate[:, sl]).astype(BF16))
        mixed = jnp.concatenate(parts + [a_ref[rs, :]], axis=1)
        y = jnp.dot(mixed, wo_ref[...], preferred_element_type=F32)
        x2.append(_layer_norm(ALPHA * x_ref[rs, :] + mod_ref[0, 5:6, :] * y, g1_ref[...], b1_ref[...]))
    for p in range(MIX_PARTS):
        rs = slice(p * rows_per, (p + 1) * rows_per)
        o_ref[rs, :] = _ffn_body(x2[p], mod_ref, 6, w13_ref, w2_ref, g2_ref, b2_ref, h_ref.at[p], acc_ref.at[p])


def _mix_ffn_call(hf, hb, o_pre, a_l, x1, mod, mod_index, mh_g, w_out, g1, b1, ffn_w13, ffn_w2, layer, which,
                  g2, b2, tm):
    r = x1.shape[0]
    row = lambda w: pl.BlockSpec((tm, w), lambda i: (i, 0))
    vec = lambda a: a.reshape(1, a.shape[-1])
    return pl.pallas_call(
        functools.partial(_mix_ffn_kernel, layer=layer, which=which),
        grid=(r // tm,),
        in_specs=[row(M_WIDTH), row(M_WIDTH), row(M_WIDTH), row(A_WIDTH), row(D_MODEL),
                  pl.BlockSpec((1, N_MOD, D_MODEL), lambda i: (mod_index(i), 0, 0)),
                  _resident((1, M_WIDTH)),
                  _resident((D_MODEL, D_MODEL)),
                  _resident((1, D_MODEL)),
                  _resident((1, D_MODEL)),
                  pl.BlockSpec(memory_space=pl.ANY),
                  pl.BlockSpec(memory_space=pl.ANY),
                  _resident((1, D_MODEL)),
                  _resident((1, D_MODEL))],
        out_specs=row(D_MODEL),
        out_shape=jax.ShapeDtypeStruct((r, D_MODEL), F32),
        scratch_shapes=_ffn_scratch((MIX_PARTS, tm // MIX_PARTS)),
        compiler_params=_params(1),
        name="mix_ffn",
    )(hf, hb, o_pre, a_l, x1, mod, vec(mh_g), w_out, vec(g1), vec(b1), ffn_w13, ffn_w2, vec(g2), vec(b2))


def _rope_tables(seq_len):
    t = np.arange(seq_len)
    row = (t // GRID_W).astype(np.float64)
    col = (t % GRID_W).astype(np.float64)
    n_freq = A_DH // 4
    inv = ROPE_BASE ** (-np.arange(n_freq, dtype=np.float64) / n_freq)
    ang = np.concatenate([row[:, None] * inv, col[:, None] * inv], -1)
    cos, sin = np.cos(ang), np.sin(ang)
    return (jnp.asarray(np.tile(cos, (1, 4)), F32),
            jnp.asarray(np.tile(np.concatenate([-sin, sin], -1), (1, 2)), F32))


def _spread_gates(a):
    lead = a.shape[:-1]
    a = a.reshape(lead + (4, M_HEADS))
    a = jnp.pad(a, [(0, 0)] * len(lead) + [(0, 0), (0, GATE_GROUP - M_HEADS)])
    a = a.reshape(lead + (4 * GATE_GROUP,))
    return jnp.pad(a, [(0, 0)] * len(lead) + [(0, LANES - 4 * GATE_GROUP)])


def kernel(x, c, ctx, c_ctx, w_ada, b_ada, ln_g, ln_b, ffn_w13, ffn_w2, w_in, b_gates, conv_w, conv_b,
           mh_norm_g, attn_sink, w_out):
    n_batch, seq_len, d = x.shape
    n_ctx = ctx.shape[1]
    l = 0
    tm = 512
    tm_ctx = n_ctx
    tiles_per_batch = seq_len // tm

    cc = jnp.concatenate([c, c_ctx[None, :], jnp.zeros((SUBLANES - n_batch - 1, d), F32)], axis=0)
    mod = _mod_call(cc, w_ada[l], b_ada[l])[:n_batch + 1].reshape(n_batch + 1, N_MOD, d)
    lat_mod = lambda i: i // tiles_per_batch
    ctx_mod = lambda i: n_batch

    wi = w_in[l]
    wa = wi[:, OFF_QM:OFF_G].astype(BF16)
    w_va = wi[:, OFF_VA:N_IN]
    wt = jnp.concatenate([wi[:, OFF_QA:OFF_VA], w_va, w_va[:, A_DH:], w_va[:, :A_DH],
                          _spread_gates(wi[:, OFF_G:OFF_QA])], axis=1).astype(BF16)
    bg = _spread_gates(b_gates[l]).reshape(1, LANES)
    cb = conv_b[l].reshape(1, 2 * M_WIDTH)
    cos, sin = _rope_tables(seq_len)

    x_lat = x.reshape(n_batch * seq_len, d)
    x_ctx = ctx.reshape(n_batch * n_ctx, d)

    x1, xc1 = _ffn_call(x_lat, x_ctx, mod, tiles_per_batch, ffn_w13, ffn_w2, l, 0, ln_g[l, 0], ln_b[l, 0], tm)

    (_, kt_c, v_c, _, grow_c, _, kat_c, va_c) = _inproj_call(
        xc1, mod, ctx_mod, wa, wt, bg, conv_w[l], cb, None, None, tm_ctx, n_ctx)
    (q_l, kt_l, v_l, o_pre, grow_l, qa_l, kat_l, va_l) = _inproj_call(
        x1, mod, lat_mod, wa, wt, bg, conv_w[l], cb, cos, sin, tm, seq_len)

    c0 = jnp.zeros((n_batch * N_STREAMS, M_DH, M_AUG), F32)
    m0 = jnp.zeros((n_batch * N_STREAMS, LANES), F32)
    c_st, m_st = _mlstm_call(None, kt_c, v_c.reshape(n_batch, n_ctx, M_WIDTH), grow_c, c0, m0,
                             n_batch, n_ctx, False)
    hf, hb, _, _ = _mlstm_call(q_l.reshape(n_batch, seq_len, M_WIDTH), kt_l,
                               v_l.reshape(n_batch, seq_len, M_WIDTH), grow_l, c_st, m_st, n_batch, seq_len, True)
    hf = hf.reshape(n_batch * seq_len, M_WIDTH)
    hb = hb.reshape(n_batch * seq_len, M_WIDTH)

    a_l = _attn_call(attn_sink[l], qa_l, kat_l, va_l, kat_c, va_c, n_batch, seq_len, n_ctx)

    tm_mix = MIX_PARTS * tm
    out = _mix_ffn_call(hf, hb, o_pre, a_l, x1, mod, lambda i: i // (seq_len // tm_mix), mh_norm_g[l],
                        w_out[l].astype(BF16), ln_g[l, 1], ln_b[l, 1], ffn_w13, ffn_w2, l, 1,
                        ln_g[l, 2], ln_b[l, 2], tm_mix)
    return out.reshape(n_batch, seq_len, d)
```

```python
import functools

import jax
import jax.numpy as jnp
import numpy as np
from jax import lax
from jax.experimental import pallas as pl
from jax.experimental.pallas import tpu as pltpu

F32 = jnp.float32
BF16 = jnp.bfloat16

D_MODEL = 1024
GRID_W = 64
M_HEADS = 4
M_DH = 128
M_WIDTH = M_HEADS * M_DH
A_HEADS = 8
A_KV_HEADS = 2
A_DH = 64
A_WIDTH = A_HEADS * A_DH
A_KV_WIDTH = A_KV_HEADS * A_DH
A_BLOCK = 128
CONV_W = 5
D_FF = 2816
ROPE_BASE = 10000.0
N_MOD = 9
LN_EPS = 1e-5
FFN_RES = 0.5
DEPTH = 1
ALPHA = (2.0 * DEPTH) ** 0.25

OFF_QM = 0
OFF_KM = OFF_QM + M_WIDTH
OFF_VM = OFF_KM + M_WIDTH
OFF_OM = OFF_VM + M_WIDTH
OFF_G = OFF_OM + M_WIDTH
N_GATES = 4 * M_HEADS
OFF_QA = OFF_G + N_GATES
OFF_KA = OFF_QA + A_WIDTH
OFF_VA = OFF_KA + A_KV_WIDTH
N_IN = OFF_VA + A_KV_WIDTH

LANES = 128
SUBLANES = 8
MXU_TILE = 256
VMEM_BYTES = 64 * 1024 * 1024
VMEM_LIMIT = VMEM_BYTES * 7 // 8
HALO = SUBLANES
MLSTM_CHUNK = LANES
MLSTM_STEP_CHUNKS = 4
FF_CHUNK = MXU_TILE
COL_BLOCK = MXU_TILE
ROW_PART = LANES
MIX_PARTS = 2
ATT_Q_BLOCKS = 16
MOD_COLS = 2304
GATE_GROUP = SUBLANES


def _sigmoid(x):
    return 1.0 / (1.0 + jnp.exp(-x))


def _log_sigmoid(x):
    return jnp.minimum(x, 0.0) - jnp.log1p(jnp.exp(-jnp.abs(x)))


def _layer_norm(z, g, b):
    mu = jnp.mean(z, axis=-1, keepdims=True)
    zc = z - mu
    var = jnp.mean(zc * zc, axis=-1, keepdims=True)
    return zc * lax.rsqrt(var + LN_EPS) * g + b


def _params(n_grid):
    return pltpu.CompilerParams(dimension_semantics=("arbitrary",) * n_grid, vmem_limit_bytes=VMEM_LIMIT)


def _resident(shape):
    nd = len(shape)
    return pl.BlockSpec(shape, lambda *_: (0,) * nd, pipeline_mode=pl.Buffered(1))


def _mod_kernel(c_ref, w_ref, b_ref, o_ref):
    c = c_ref[...]
    s = (c * _sigmoid(c)).astype(BF16)
    o_ref[...] = jnp.dot(s, w_ref[...].astype(BF16), preferred_element_type=F32) + b_ref[...]


def _mod_call(cc, w_ada, b_ada):
    n = w_ada.shape[1]
    tn = MOD_COLS
    return pl.pallas_call(
        _mod_kernel,
        grid=(n // tn,),
        in_specs=[pl.BlockSpec((SUBLANES, D_MODEL), lambda j: (0, 0)),
                  pl.BlockSpec((D_MODEL, tn), lambda j: (0, j)),
                  pl.BlockSpec((1, tn), lambda j: (0, j))],
        out_specs=pl.BlockSpec((SUBLANES, tn), lambda j: (0, j)),
        out_shape=jax.ShapeDtypeStruct((SUBLANES, n), F32),
        compiler_params=_params(1),
        name="mod",
    )(cc, w_ada, b_ada.reshape(1, n))


def _ffn_body(x, mod_ref, k0, w13_ref, w2_ref, g_ref, b_ref, h_ref, acc_ref, before_chunk=None):
    shift = mod_ref[0, k0:k0 + 1, :]
    scale = mod_ref[0, k0 + 1:k0 + 2, :]
    gate = mod_ref[0, k0 + 2:k0 + 3, :]
    h_ref[...] = (x * (1.0 + scale) + shift).astype(BF16)
    for j in range(D_FF // FF_CHUNK):
        lo = j * FF_CHUNK
        if before_chunk is not None:
            before_chunk(j)
        h = h_ref[...]
        a1 = jnp.dot(h, w13_ref[:, lo:lo + FF_CHUNK], preferred_element_type=F32)
        a3 = jnp.dot(h, w13_ref[:, D_FF + lo:D_FF + lo + FF_CHUNK], preferred_element_type=F32)
        g = (a1 * _sigmoid(a1) * a3).astype(BF16)
        y = jnp.dot(g, w2_ref[lo:lo + FF_CHUNK, :], preferred_element_type=F32)
        if j == 0:
            acc_ref[...] = y
        else:
            acc_ref[...] += y
    z = ALPHA * x + (FFN_RES * gate) * acc_ref[...]
    return _layer_norm(z, g_ref[...], b_ref[...])


def _weight_stager(w13_hbm, w2_hbm, w13_ref, w2_ref, st13_ref, st2_ref, sem, layer, which):
    n_chunks = D_FF // FF_CHUNK

    def copies(j):
        lo, slot = j * FF_CHUNK, j % 2
        return (pltpu.make_async_copy(w13_hbm.at[layer, which, :, lo:lo + FF_CHUNK],
                                      st13_ref.at[slot, 0], sem.at[slot, 0]),
                pltpu.make_async_copy(w13_hbm.at[layer, which, :, D_FF + lo:D_FF + lo + FF_CHUNK],
                                      st13_ref.at[slot, 1], sem.at[slot, 1]),
                pltpu.make_async_copy(w2_hbm.at[layer, which, lo:lo + FF_CHUNK, :],
                                      st2_ref.at[slot], sem.at[slot, 2]))

    def before_chunk(j):
        if j == 0:
            for cp in copies(0):
                cp.start()
        if j + 1 < n_chunks:
            for cp in copies(j + 1):
                cp.start()
        for cp in copies(j):
            cp.wait()
        lo, slot = j * FF_CHUNK, j % 2
        w13_ref[:, lo:lo + FF_CHUNK] = st13_ref[slot, 0].astype(BF16)
        w13_ref[:, D_FF + lo:D_FF + lo + FF_CHUNK] = st13_ref[slot, 1].astype(BF16)
        w2_ref[lo:lo + FF_CHUNK, :] = st2_ref[slot].astype(BF16)

    return before_chunk


def _ffn_kernel(x_ref, xc_ref, mod_ref, w13_hbm, w2_hbm, g_ref, b_ref, o_ref, oc_ref,
                w13_ref, w2_ref, st13_ref, st2_ref, sem, h_ref, acc_ref, *, layer, which):
    i = pl.program_id(0)
    n_lat = pl.num_programs(0) - 1
    ffn = functools.partial(_ffn_body, mod_ref=mod_ref, k0=0, w13_ref=w13_ref, w2_ref=w2_ref, g_ref=g_ref,
                            b_ref=b_ref, h_ref=h_ref, acc_ref=acc_ref)

    @pl.when(i == 0)
    def _():
        stage = _weight_stager(w13_hbm, w2_hbm, w13_ref, w2_ref, st13_ref, st2_ref, sem, layer, which)
        o_ref[...] = ffn(x_ref[...], before_chunk=stage)

    @pl.when(i > 0)
    def _():
        y = ffn(jnp.where(i < n_lat, x_ref[...], xc_ref[...]))

        @pl.when(i < n_lat)
        def _():
            o_ref[...] = y

        @pl.when(i == n_lat)
        def _():
            oc_ref[...] = y


def _ffn_scratch(rows):
    rows = rows if isinstance(rows, tuple) else (rows,)
    return [pltpu.VMEM((D_MODEL, 2 * D_FF), BF16), pltpu.VMEM((D_FF, D_MODEL), BF16),
            pltpu.VMEM((2, 2, D_MODEL, FF_CHUNK), F32), pltpu.VMEM((2, FF_CHUNK, D_MODEL), F32),
            pltpu.SemaphoreType.DMA((2, 3)),
            pltpu.VMEM(rows + (D_MODEL,), BF16), pltpu.VMEM(rows + (D_MODEL,), F32)]


def _ffn_call(x, x_ctx, mod, tiles_per_batch, ffn_w13, ffn_w2, layer, which, ln_g, ln_b, tm):
    r = x.shape[0]
    n_lat = r // tm
    assert x_ctx.shape[0] == tm
    lat = lambda i: jnp.minimum(i, n_lat - 1)
    return pl.pallas_call(
        functools.partial(_ffn_kernel, layer=layer, which=which),
        grid=(n_lat + 1,),
        in_specs=[pl.BlockSpec((tm, D_MODEL), lambda i: (lat(i), 0)),
                  pl.BlockSpec((tm, D_MODEL), lambda i: (0, 0)),
                  pl.BlockSpec((1, N_MOD, D_MODEL), lambda i: (i // tiles_per_batch, 0, 0)),
                  pl.BlockSpec(memory_space=pl.ANY),
                  pl.BlockSpec(memory_space=pl.ANY),
                  _resident((1, D_MODEL)),
                  _resident((1, D_MODEL))],
        out_specs=[pl.BlockSpec((tm, D_MODEL), lambda i: (lat(i), 0)),
                   pl.BlockSpec((tm, D_MODEL), lambda i: (0, 0))],
        out_shape=[jax.ShapeDtypeStruct((r, D_MODEL), F32), jax.ShapeDtypeStruct((tm, D_MODEL), F32)],
        scratch_shapes=_ffn_scratch(tm),
        compiler_params=_params(1),
        name="ffn",
    )(x, x_ctx, mod, ffn_w13, ffn_w2, ln_g.reshape(1, D_MODEL), ln_b.reshape(1, D_MODEL))


N_WA = 2 * M_WIDTH + 2 * M_WIDTH
OFF_TG = A_WIDTH + A_KV_WIDTH + 2 * A_KV_WIDTH
N_WT = OFF_TG + LANES
ROW_A = (0, 3 * GATE_GROUP)
ROW_B = (GATE_GROUP, 4 * GATE_GROUP)
ROW_C = (2 * GATE_GROUP, 5 * GATE_GROUP)
GATE_ROWS = 6 * GATE_GROUP


def _scan(x, lane, op, reverse):
    fill = 0.0 if op is jnp.add else -jnp.inf
    k = 1
    while k < LANES:
        if reverse:
            other = jnp.where(lane < LANES - k, pltpu.roll(x, LANES - k, 1), fill)
        else:
            other = jnp.where(lane >= k, pltpu.roll(x, k, 1), fill)
        x = op(x, other)
        k *= 2
    return x


def _chunk_scan(x, lane, op, reverse):
    return jnp.concatenate([_scan(x[:, c * LANES:(c + 1) * LANES], lane, op, reverse)
                            for c in range(x.shape[1] // LANES)], axis=1)


def _rope(x, cos, sin_signed, first_half):
    swapped = jnp.where(first_half, pltpu.roll(x, LANES - A_DH // 2, 1), pltpu.roll(x, A_DH // 2, 1))
    return x * cos + swapped * sin_signed


def _inproj_kernel(*refs, tiles_per_seq, rope, tm):
    n_qk = 2 * M_WIDTH // COL_BLOCK
    raw_refs = refs[-n_qk:]
    refs = refs[:-n_qk]
    if rope:
        (xp_ref, x_ref, xn_ref, mod_ref, wa_ref, wt_ref, bg_ref, cw_ref, cb_ref, cos_ref, sin_ref,
         q_ref, kt_ref, v_ref, o_ref, grow_ref, qa_ref, kat_ref, va_ref, h_ref, pt_ref) = refs
    else:
        (xp_ref, x_ref, xn_ref, mod_ref, wa_ref, wt_ref, bg_ref, cw_ref, cb_ref,
         q_ref, kt_ref, v_ref, o_ref, grow_ref, qa_ref, kat_ref, va_ref, h_ref, pt_ref) = refs
    pos = pl.program_id(0) % tiles_per_seq
    slot0 = jnp.minimum(pl.program_id(0), 0)
    n_ext = tm + 2 * HALO
    shift = mod_ref[0, 3:4, :]
    scale1 = 1.0 + mod_ref[0, 4:5, :]
    h_ref[0:tm, :] = (x_ref[...] * scale1 + shift).astype(BF16)
    xh = jnp.concatenate([xp_ref[...], xn_ref[...]], axis=0)
    h_ref[tm:n_ext, :] = (xh * scale1 + shift).astype(BF16)

    def mm(w_ref, lo):
        return jnp.dot(h_ref[0:tm, :], w_ref[:, lo:lo + COL_BLOCK], preferred_element_type=F32)

    def qk_matmul(blk):
        cols = slice(blk * COL_BLOCK, (blk + 1) * COL_BLOCK)
        y = jnp.dot(h_ref[...], wa_ref[:, cols], preferred_element_type=F32)
        raw_refs[blk][0, 0:HALO, :] = jnp.where(pos == 0, 0.0, y[tm:tm + HALO])
        raw_refs[blk][0, HALO:HALO + tm, :] = y[0:tm]
        raw_refs[blk][0, HALO + tm:n_ext, :] = jnp.where(pos == tiles_per_seq - 1, 0.0, y[tm + HALO:n_ext])

    def qk_epilogue(blk, part):
        cols = slice(blk * COL_BLOCK, (blk + 1) * COL_BLOCK)
        r0 = part * ROW_PART
        base = HALO - CONV_W // 2 + r0
        acc = cb_ref[:, cols] + cw_ref[0:1, cols] * raw_refs[blk][slot0, pl.ds(base, ROW_PART), :]
        for j in range(1, CONV_W):
            acc = acc + cw_ref[j:j + 1, cols] * raw_refs[blk][slot0, pl.ds(base + j, ROW_PART), :]
        qk = acc * _sigmoid(acc)
        if blk < M_WIDTH // COL_BLOCK:
            q_ref[r0:r0 + ROW_PART, cols] = (qk * (M_DH ** -0.5)).astype(BF16)
        else:
            kt_ref[0, part, blk * COL_BLOCK - M_WIDTH:(blk + 1) * COL_BLOCK - M_WIDTH, :] = qk.T.astype(BF16)

    def v_matmul(blk):
        v_ref[:, blk * COL_BLOCK:(blk + 1) * COL_BLOCK] = mm(wa_ref, 2 * M_WIDTH + blk * COL_BLOCK).astype(BF16)

    def o_matmul(blk):
        o_ref[:, blk * COL_BLOCK:(blk + 1) * COL_BLOCK] = mm(wa_ref, 3 * M_WIDTH + blk * COL_BLOCK).astype(BF16)

    def t_matmul(blk):
        pt_ref[:, blk * COL_BLOCK:(blk + 1) * COL_BLOCK] = mm(wt_ref, blk * COL_BLOCK)

    def pt_cols(lo, width):
        return pt_ref[:, lo:lo + width]

    def gate_epilogue():
        pgt = (pt_cols(OFF_TG, LANES) + bg_ref[...]).T
        li_f = pgt[0:GATE_GROUP]
        lf_f = _log_sigmoid(pgt[GATE_GROUP:2 * GATE_GROUP])
        li_b = pgt[2 * GATE_GROUP:3 * GATE_GROUP]
        lf_b = _log_sigmoid(pgt[3 * GATE_GROUP:4 * GATE_GROUP])
        lane = lax.broadcasted_iota(jnp.int32, (GATE_GROUP, LANES), 1)
        b_f = _chunk_scan(lf_f, lane, jnp.add, False)
        e_b = _chunk_scan(lf_b, lane, jnp.add, True)
        a_f = li_f - b_f
        a_b = li_b - e_b
        rows = jnp.concatenate([a_f, b_f, _chunk_scan(a_f, lane, jnp.maximum, False),
                                a_b, e_b, _chunk_scan(a_b, lane, jnp.maximum, True)], axis=0)
        for c in range(tm // MLSTM_CHUNK):
            grow_ref[0, c] = rows[:, c * MLSTM_CHUNK:(c + 1) * MLSTM_CHUNK]
        va_ref[:, A_KV_WIDTH:] = pt_cols(A_WIDTH + 2 * A_KV_WIDTH, A_KV_WIDTH).astype(BF16)

    def rotate(lo):
        xg = pt_cols(lo, LANES)
        if not rope:
            return xg
        lane_t = lax.broadcasted_iota(jnp.int32, (tm, LANES), 1)
        return _rope(xg, cos_ref[...], sin_ref[...], (lane_t % A_DH) < (A_DH // 2))

    def qa_epilogue(blk):
        for g in range(blk * COL_BLOCK // LANES, (blk + 1) * COL_BLOCK // LANES):
            qa_ref[:, g * LANES:(g + 1) * LANES] = (rotate(g * LANES) * (A_DH ** -0.5)).astype(BF16)

    def kv_epilogue():
        kat = rotate(A_WIDTH).T.astype(BF16)
        for c in range(tm // A_BLOCK):
            kat_ref[c] = kat[:, c * A_BLOCK:(c + 1) * A_BLOCK]
        va_ref[:, :A_KV_WIDTH] = pt_cols(A_WIDTH + A_KV_WIDTH, A_KV_WIDTH).astype(BF16)

    for blk in range(n_qk):
        qk_matmul(blk)
    for blk in range(N_WT // COL_BLOCK):
        t_matmul(blk)
    for blk in range(M_WIDTH // COL_BLOCK):
        v_matmul(blk)
        o_matmul(blk)
    for blk in range(n_qk):
        for part in range(tm // ROW_PART):
            qk_epilogue(blk, part)
    gate_epilogue()
    qa_epilogue(0)
    qa_epilogue(1)
    kv_epilogue()


def _inproj_call(x, mod, mod_index, wa, wt, bg, conv_w, conv_b, cos, sin, tm, seq_len):
    r = x.shape[0]
    tiles_per_seq = seq_len // tm
    rope = cos is not None
    hb = tm // HALO
    n_halo = r // HALO
    in_specs = [pl.BlockSpec((HALO, D_MODEL), lambda i: (jnp.maximum(i * hb - 1, 0), 0)),
                pl.BlockSpec((tm, D_MODEL), lambda i: (i, 0)),
                pl.BlockSpec((HALO, D_MODEL), lambda i: (jnp.minimum((i + 1) * hb, n_halo - 1), 0)),
                pl.BlockSpec((1, N_MOD, D_MODEL), lambda i: (mod_index(i), 0, 0)),
                _resident((D_MODEL, N_WA)),
                _resident((D_MODEL, N_WT)),
                _resident((1, LANES)),
                _resident((CONV_W, 2 * M_WIDTH)),
                _resident((1, 2 * M_WIDTH))]
    args = [x, x, x, mod, wa, wt, bg, conv_w, conv_b]
    if rope:
        in_specs += [pl.BlockSpec((tm, LANES), lambda i: (i % tiles_per_seq, 0)),
                     pl.BlockSpec((tm, LANES), lambda i: (i % tiles_per_seq, 0))]
        args += [cos, sin]
    assert MLSTM_CHUNK == A_BLOCK == ROW_PART
    n_seq = r // seq_len
    cps = seq_len // MLSTM_CHUNK
    cpt = tm // MLSTM_CHUNK
    row = lambda w: pl.BlockSpec((tm, w), lambda i: (i, 0))
    seq_col = lambda h: pl.BlockSpec((1, cpt, h, MLSTM_CHUNK),
                                     lambda i: (i // tiles_per_seq, i % tiles_per_seq, 0, 0))
    out_specs = [row(M_WIDTH), seq_col(M_WIDTH), row(M_WIDTH), row(M_WIDTH), seq_col(GATE_ROWS),
                 row(A_WIDTH), pl.BlockSpec((cpt, A_KV_WIDTH, A_BLOCK), lambda i: (i, 0, 0)),
                 row(2 * A_KV_WIDTH)]
    out_shape = [jax.ShapeDtypeStruct((r, M_WIDTH), BF16),
                 jax.ShapeDtypeStruct((n_seq, cps, M_WIDTH, MLSTM_CHUNK), BF16),
                 jax.ShapeDtypeStruct((r, M_WIDTH), BF16),
                 jax.ShapeDtypeStruct((r, M_WIDTH), BF16),
                 jax.ShapeDtypeStruct((n_seq, cps, GATE_ROWS, MLSTM_CHUNK), F32),
                 jax.ShapeDtypeStruct((r, A_WIDTH), BF16),
                 jax.ShapeDtypeStruct((r // A_BLOCK, A_KV_WIDTH, A_BLOCK), BF16),
                 jax.ShapeDtypeStruct((r, 2 * A_KV_WIDTH), BF16)]
    return pl.pallas_call(
        functools.partial(_inproj_kernel, tiles_per_seq=tiles_per_seq, rope=rope, tm=tm),
        grid=(r // tm,),
        in_specs=in_specs,
        out_specs=out_specs,
        out_shape=out_shape,
        scratch_shapes=([pltpu.VMEM((tm + 2 * HALO, D_MODEL), BF16), pltpu.VMEM((tm, N_WT), F32)]
                        + [pltpu.VMEM((1, tm + 2 * HALO, COL_BLOCK), F32)] * (2 * M_WIDTH // COL_BLOCK)),
        compiler_params=_params(1),
        name="inproj",
    )(*args)


M_AUG = 2 * M_DH
N_STREAMS = 2 * M_HEADS


def _col_replicated(row):
    return jnp.broadcast_to(row, (LANES, row.shape[1])).T


def _mlstm_kernel(*refs, emit_h):
    if emit_h:
        (qf_ref, ktf_ref, vf_ref, grf_ref, qb_ref, ktb_ref, vb_ref, grb_ref, c0_ref, m0_ref,
         hf_ref, hb_ref, cout_ref, mout_ref, c_ref, m_ref) = refs
        q_refs, h_refs = (qf_ref, qb_ref), (hf_ref, hb_ref)
    else:
        (ktf_ref, vf_ref, grf_ref, ktb_ref, vb_ref, grb_ref, c0_ref, m0_ref,
         cout_ref, mout_ref, c_ref, m_ref) = refs
    kt_refs, v_refs, gr_refs = (ktf_ref, ktb_ref), (vf_ref, vb_ref), (grf_ref, grb_ref)
    L = MLSTM_CHUNK
    n_batch = c_ref.shape[0] // N_STREAMS
    n_sub = grf_ref.shape[1]
    c = pl.program_id(0)

    @pl.when(c == 0)
    def _():
        c_ref[...] = c0_ref[...]
        m_ref[...] = m0_ref[...]

    t_idx = lax.broadcasted_iota(jnp.int32, (L, L), 0)
    s_idx = lax.broadcasted_iota(jnp.int32, (L, L), 1)
    masks = (s_idx <= t_idx, s_idx >= t_idx)
    ones = jnp.ones((L, M_DH), BF16)
    zeros_k = jnp.zeros((M_DH, L), BF16)
    m_all = m_ref[...]
    m_cur = [m_all[r:r + 1] for r in range(n_batch * N_STREAMS)]
    for sub in range(n_sub):
        for b in range(n_batch):
            for d in range(2):
                last = (L - 1, 0)[d]
                ch = sub if d == 0 else n_sub - 1 - sub
                tok = slice(ch * L, (ch + 1) * L)
                gr = gr_refs[d][b, ch]
                for pair in range(M_HEADS // 2):
                    if emit_h:
                        rows = slice(2 * pair * M_DH, (2 * pair + 2) * M_DH)
                        kt2 = kt_refs[d][b, ch, rows, :]
                        rhs = jnp.concatenate(
                            [jnp.concatenate([kt2[:M_DH], zeros_k], axis=1),
                             jnp.concatenate([zeros_k, kt2[M_DH:]], axis=1)], axis=0)
                        s2 = jnp.dot(q_refs[d][b, tok, rows], rhs, preferred_element_type=F32)
                    for hh in range(2):
                        h = 2 * pair + hh
                        r = (b * 2 + d) * M_HEADS + h
                        sl = slice(h * M_DH, (h + 1) * M_DH)
                        a_row = gr[ROW_A[d] + h:ROW_A[d] + h + 1]
                        b_row = gr[ROW_B[d] + h:ROW_B[d] + h + 1]
                        m_row = m_cur[r]
                        ct = c_ref[r]
                        kt = kt_refs[d][b, ch, sl, :]
                        v_aug = jnp.concatenate([v_refs[d][b, tok, sl], ones], axis=1)
                        if emit_h:
                            mu_row = jnp.maximum(m_row, gr[ROW_C[d] + h:ROW_C[d] + h + 1])
                            mu = _col_replicated(mu_row)
                            mt = _col_replicated(b_row + mu_row)
                            dmat = jnp.where(masks[d], jnp.exp(a_row - mu), 0.0)
                            p = (s2[:, hh * L:(hh + 1) * L] * dmat).astype(BF16)
                            qs = (q_refs[d][b, tok, sl].astype(F32) * jnp.exp(m_row - mu)).astype(BF16)
                            tot = jnp.dot(jnp.concatenate([p, qs], axis=1),
                                          jnp.concatenate([v_aug, ct.astype(BF16)], axis=0),
                                          preferred_element_type=F32)
                            h_dir = tot[:, :M_DH] / jnp.maximum(jnp.abs(tot[:, M_DH:]), jnp.exp(-mt))
                            h_refs[d][b, tok, sl] = h_dir.astype(BF16)
                        mp = jnp.maximum(m_row, jnp.max(a_row, axis=1, keepdims=True))
                        ktw = (kt.astype(F32) * jnp.exp(a_row - mp)).astype(BF16)
                        decay = jnp.exp(m_row - mp)
                        c_ref[r] = (jnp.concatenate([decay, decay], axis=1) * ct
                                    + jnp.dot(ktw, v_aug, preferred_element_type=F32))
                        m_cur[r] = b_row[:, last:last + 1] + mp
    m_ref[...] = jnp.concatenate(m_cur, axis=0)

    @pl.when(c == pl.num_programs(0) - 1)
    def _():
        cout_ref[...] = c_ref[...]
        mout_ref[...] = m_ref[...]


def _mlstm_call(q, kt, v, grow, c0, m0, n_batch, seq_len, emit_h):
    L = MLSTM_CHUNK
    n_sub = min(MLSTM_STEP_CHUNKS, seq_len // L)
    nc = seq_len // (L * n_sub)
    fwd = lambda c: c
    bwd = lambda c: nc - 1 - c
    row = lambda w, f: pl.BlockSpec((n_batch, n_sub * L, w), lambda c: (0, f(c), 0))
    col = lambda hgt, f: pl.BlockSpec((n_batch, n_sub, hgt, L), lambda c: (0, f(c), 0, 0))
    n_str = n_batch * N_STREAMS
    state_specs = [pl.BlockSpec((n_str, M_DH, M_AUG), lambda c: (0, 0, 0)),
                   pl.BlockSpec((n_str, LANES), lambda c: (0, 0))]
    state_shapes = [jax.ShapeDtypeStruct((n_str, M_DH, M_AUG), F32),
                    jax.ShapeDtypeStruct((n_str, LANES), F32)]
    if emit_h:
        in_specs = [row(M_WIDTH, fwd), col(M_WIDTH, fwd), row(M_WIDTH, fwd), col(GATE_ROWS, fwd),
                    row(M_WIDTH, bwd), col(M_WIDTH, bwd), row(M_WIDTH, bwd), col(GATE_ROWS, bwd)]
        args = [q, kt, v, grow, q, kt, v, grow]
        out_specs = [row(M_WIDTH, fwd), row(M_WIDTH, bwd)] + state_specs
        out_shape = [jax.ShapeDtypeStruct((n_batch, seq_len, M_WIDTH), BF16)] * 2 + state_shapes
    else:
        in_specs = [col(M_WIDTH, fwd), row(M_WIDTH, fwd), col(GATE_ROWS, fwd),
                    col(M_WIDTH, bwd), row(M_WIDTH, bwd), col(GATE_ROWS, bwd)]
        args = [kt, v, grow, kt, v, grow]
        out_specs = state_specs
        out_shape = state_shapes
    return pl.pallas_call(
        functools.partial(_mlstm_kernel, emit_h=emit_h),
        grid=(nc,),
        in_specs=in_specs + state_specs,
        out_specs=out_specs,
        out_shape=out_shape,
        scratch_shapes=[pltpu.VMEM((n_str, M_DH, M_AUG), F32), pltpu.VMEM((n_str, LANES), F32)],
        compiler_params=_params(1),
        name="mlstm" if emit_h else "mlstm_ctx",
    )(*args, c0, m0)


def _attn_kernel(sink_ref, q_ref, ktp_ref, ktm_ref, ktn_ref, vp_ref, vm_ref, vn_ref, ktx_ref, vx_ref, o_ref,
                 *, n_ctx):
    j = pl.program_id(1)
    nj = pl.num_programs(1)
    T = A_BLOCK
    NQ = ATT_Q_BLOCKS
    n_loc = 3 * T
    n_key = n_loc + n_ctx
    half = LANES // 2
    i_idx = lax.broadcasted_iota(jnp.int32, (T, T), 0)
    r_idx = lax.broadcasted_iota(jnp.int32, (T, T), 1)
    lane_v = lax.broadcasted_iota(jnp.int32, (1, LANES), 1)
    low = lane_v < half
    zk = jnp.zeros((A_DH, n_key), BF16)
    ones_top = jnp.broadcast_to(jnp.where(low, 1.0, 0.0).astype(BF16), (n_key, LANES))
    ones_bot = jnp.broadcast_to(jnp.where(low, 0.0, 1.0).astype(BF16), (n_key, LANES))
    for t in range(NQ):
        rows = slice(t * T, (t + 1) * T)
        prev_ok = (r_idx >= i_idx) & (j > 0) if t == 0 else r_idx >= i_idx
        next_ok = (r_idx <= i_idx) & (j < nj - 1) if t == NQ - 1 else r_idx <= i_idx
        for k in range(A_KV_HEADS):
            ks = slice(k * A_DH, (k + 1) * A_DH)
            kt_prev = ktp_ref[0, ks, :] if t == 0 else ktm_ref[t - 1, ks, :]
            kt_next = ktn_ref[0, ks, :] if t == NQ - 1 else ktm_ref[t + 1, ks, :]
            v_prev = vp_ref[...] if t == 0 else vm_ref[(t - 1) * T:t * T, :]
            v_next = vn_ref[...] if t == NQ - 1 else vm_ref[(t + 1) * T:(t + 2) * T, :]
            v_rows = (v_prev, vm_ref[rows, :], v_next, vx_ref[...])
            kt_all = jnp.concatenate([kt_prev, ktm_ref[t, ks, :], kt_next]
                                     + [ktx_ref[cc, ks, :] for cc in range(n_ctx // T)], axis=1)
            rhs = jnp.concatenate([jnp.concatenate([kt_all, zk], axis=1),
                                   jnp.concatenate([zk, kt_all], axis=1)], axis=0)
            top_sl = slice(0, LANES) if k == 0 else slice(LANES, 2 * LANES)
            bot_sl = slice(LANES, 2 * LANES) if k == 0 else slice(0, LANES)
            v_top = jnp.concatenate([jnp.where(low, vr[:, top_sl], 0) for vr in v_rows], axis=0)
            v_bot = jnp.concatenate([jnp.where(low, 0, vr[:, bot_sl]) for vr in v_rows], axis=0)
            v2 = jnp.concatenate([jnp.concatenate([v_top.astype(BF16), ones_top], axis=1),
                                  jnp.concatenate([v_bot.astype(BF16), ones_bot], axis=1)], axis=0)
            for p in range(A_HEADS // A_KV_HEADS // 2):
                g = k * (A_HEADS // A_KV_HEADS // 2) + p
                q2 = q_ref[rows, g * LANES:(g + 1) * LANES]
                s2 = jnp.dot(q2, rhs, preferred_element_type=F32)
                es = []
                sink_terms = []
                for u in range(2):
                    base = u * n_key
                    sink = sink_ref[2 * g + u]
                    s_prev = jnp.where(prev_ok, s2[:, base:base + T], -jnp.inf)
                    s_cur = s2[:, base + T:base + 2 * T]
                    s_next = jnp.where(next_ok, s2[:, base + 2 * T:base + 3 * T], -jnp.inf)
                    s_ctx = s2[:, base + n_loc:base + n_key]
                    m = jnp.maximum(jnp.maximum(s_prev, s_cur), s_next)
                    for cc in range(n_ctx // T):
                        m = jnp.maximum(m, s_ctx[:, cc * T:(cc + 1) * T])
                    m = jnp.maximum(jnp.max(m, axis=1, keepdims=True), sink)
                    es += [jnp.exp(s_prev - m), jnp.exp(s_cur - m), jnp.exp(s_next - m), jnp.exp(s_ctx - m)]
                    sink_terms.append(jnp.exp(sink - m))
                p2 = jnp.concatenate(es, axis=1).astype(BF16)
                o = jnp.dot(p2, v2, preferred_element_type=F32)
                den = o[:, LANES:] + jnp.where(low, sink_terms[0], sink_terms[1])
                o_ref[rows, g * LANES:(g + 1) * LANES] = (o[:, :LANES] / den).astype(BF16)


def _attn_call(sink, qa, kat, va, kat_c, va_c, n_batch, seq_len, n_ctx):
    T = A_BLOCK
    NQ = ATT_Q_BLOCKS
    nb = seq_len // T
    nj = nb // NQ
    r = n_batch * seq_len
    prv = lambda b, j: b * nb + jnp.maximum(j * NQ - 1, 0)
    nxt = lambda b, j: b * nb + jnp.minimum((j + 1) * NQ, nb - 1)
    kedge = lambda f: pl.BlockSpec((1, A_KV_WIDTH, T), lambda b, j: (f(b, j), 0, 0))
    vedge = lambda f: pl.BlockSpec((T, 2 * A_KV_WIDTH), lambda b, j: (f(b, j), 0))
    return pl.pallas_call(
        functools.partial(_attn_kernel, n_ctx=n_ctx),
        grid=(n_batch, nj),
        in_specs=[pl.BlockSpec(memory_space=pltpu.SMEM),
                  pl.BlockSpec((NQ * T, A_WIDTH), lambda b, j: (b * nj + j, 0)),
                  kedge(prv), pl.BlockSpec((NQ, A_KV_WIDTH, T), lambda b, j: (b * nj + j, 0, 0)), kedge(nxt),
                  vedge(prv), pl.BlockSpec((NQ * T, 2 * A_KV_WIDTH), lambda b, j: (b * nj + j, 0)), vedge(nxt),
                  pl.BlockSpec((n_ctx // T, A_KV_WIDTH, T), lambda b, j: (b, 0, 0)),
                  pl.BlockSpec((n_ctx, 2 * A_KV_WIDTH), lambda b, j: (b, 0))],
        out_specs=pl.BlockSpec((NQ * T, A_WIDTH), lambda b, j: (b * nj + j, 0)),
        out_shape=jax.ShapeDtypeStruct((r, A_WIDTH), BF16),
        compiler_params=_params(2),
        name="attn",
    )(sink, qa, kat, kat, kat, va, va, va, kat_c, va_c)


def _mix_ffn_kernel(hf_ref, hb_ref, op_ref, a_ref, x_ref, mod_ref, mg_ref, wo_ref, g1_ref, b1_ref,
                    w13_hbm, w2_hbm, g2_ref, b2_ref, o_ref,
                    w13_ref, w2_ref, st13_ref, st2_ref, sem, h_ref, acc_ref, *, layer, which):
    @pl.when(pl.program_id(0) == 0)
    def _():
        stage = _weight_stager(w13_hbm, w2_hbm, w13_ref, w2_ref, st13_ref, st2_ref, sem, layer, which)
        for j in range(D_FF // FF_CHUNK):
            stage(j)

    rows_per = hf_ref.shape[0] // MIX_PARTS
    x2 = []
    for p in range(MIX_PARTS):
        rs = slice(p * rows_per, (p + 1) * rows_per)
        h = hf_ref[rs, :].astype(F32) + hb_ref[rs, :].astype(F32)
        gate = _sigmoid(op_ref[rs, :].astype(F32))
        parts = []
        for hd in range(M_HEADS):
            sl = slice(hd * M_DH, (hd + 1) * M_DH)
            seg = h[:, sl]
            mu = jnp.mean(seg, axis=-1, keepdims=True)
            sc = seg - mu
            var = jnp.mean(sc * sc, axis=-1, keepdims=True)
            parts.append((sc * lax.rsqrt(var + LN_EPS) * mg_ref[:, sl] * gate[:, sl]).astype(BF16))
        mixed = jnp.concatenate(parts + [a_ref[rs, :]], axis=1)
        y = jnp.dot(mixed, wo_ref[...], preferred_element_type=F32)
        x2.append(_layer_norm(ALPHA * x_ref[rs, :] + mod_ref[0, 5:6, :] * y, g1_ref[...], b1_ref[...]))
    for p in range(MIX_PARTS):
        rs = slice(p * rows_per, (p + 1) * rows_per)
        o_ref[rs, :] = _ffn_body(x2[p], mod_ref, 6, w13_ref, w2_ref, g2_ref, b2_ref, h_ref.at[p], acc_ref.at[p])


def _mix_ffn_call(hf, hb, o_pre, a_l, x1, mod, mod_index, mh_g, w_out, g1, b1, ffn_w13, ffn_w2, layer, which,
                  g2, b2, tm):
    r = x1.shape[0]
    row = lambda w: pl.BlockSpec((tm, w), lambda i: (i, 0))
    vec = lambda a: a.reshape(1, a.shape[-1])
    return pl.pallas_call(
        functools.partial(_mix_ffn_kernel, layer=layer, which=which),
        grid=(r // tm,),
        in_specs=[row(M_WIDTH), row(M_WIDTH), row(M_WIDTH), row(A_WIDTH), row(D_MODEL),
                  pl.BlockSpec((1, N_MOD, D_MODEL), lambda i: (mod_index(i), 0, 0)),
                  _resident((1, M_WIDTH)),
                  _resident((D_MODEL, D_MODEL)),
                  _resident((1, D_MODEL)),
                  _resident((1, D_MODEL)),
                  pl.BlockSpec(memory_space=pl.ANY),
                  pl.BlockSpec(memory_space=pl.ANY),
                  _resident((1, D_MODEL)),
                  _resident((1, D_MODEL))],
        out_specs=row(D_MODEL),
        out_shape=jax.ShapeDtypeStruct((r, D_MODEL), F32),
        scratch_shapes=_ffn_scratch((MIX_PARTS, tm // MIX_PARTS)),
        compiler_params=_params(1),
        name="mix_ffn",
    )(hf, hb, o_pre, a_l, x1, mod, vec(mh_g), w_out, vec(g1), vec(b1), ffn_w13, ffn_w2, vec(g2), vec(b2))


def _rope_tables(seq_len):
    t = np.arange(seq_len)
    row = (t // GRID_W).astype(np.float64)
    col = (t % GRID_W).astype(np.float64)
    n_freq = A_DH // 4
    inv = ROPE_BASE ** (-np.arange(n_freq, dtype=np.float64) / n_freq)
    ang = np.concatenate([row[:, None] * inv, col[:, None] * inv], -1)
    cos, sin = np.cos(ang), np.sin(ang)
    return (jnp.asarray(np.tile(cos, (1, 4)), F32),
            jnp.asarray(np.tile(np.concatenate([-sin, sin], -1), (1, 2)), F32))


def _spread_gates(a):
    lead = a.shape[:-1]
    a = a.reshape(lead + (4, M_HEADS))
    a = jnp.pad(a, [(0, 0)] * len(lead) + [(0, 0), (0, GATE_GROUP - M_HEADS)])
    a = a.reshape(lead + (4 * GATE_GROUP,))
    return jnp.pad(a, [(0, 0)] * len(lead) + [(0, LANES - 4 * GATE_GROUP)])


def kernel(x, c, ctx, c_ctx, w_ada, b_ada, ln_g, ln_b, ffn_w13, ffn_w2, w_in, b_gates, conv_w, conv_b,
           mh_norm_g, attn_sink, w_out):
    n_batch, seq_len, d = x.shape
    n_ctx = ctx.shape[1]
    l = 0
    tm = 512
    tm_ctx = n_ctx
    tiles_per_batch = seq_len // tm

    cc = jnp.concatenate([c, c_ctx[None, :], jnp.zeros((SUBLANES - n_batch - 1, d), F32)], axis=0)
    mod = _mod_call(cc, w_ada[l], b_ada[l])[:n_batch + 1].reshape(n_batch + 1, N_MOD, d)
    lat_mod = lambda i: i // tiles_per_batch
    ctx_mod = lambda i: n_batch

    wi = w_in[l]
    wa = wi[:, OFF_QM:OFF_G].astype(BF16)
    w_va = wi[:, OFF_VA:N_IN]
    wt = jnp.concatenate([wi[:, OFF_QA:OFF_VA], w_va, w_va[:, A_DH:], w_va[:, :A_DH],
                          _spread_gates(wi[:, OFF_G:OFF_QA])], axis=1).astype(BF16)
    bg = _spread_gates(b_gates[l]).reshape(1, LANES)
    cb = conv_b[l].reshape(1, 2 * M_WIDTH)
    cos, sin = _rope_tables(seq_len)

    x_lat = x.reshape(n_batch * seq_len, d)
    x_ctx = ctx.reshape(n_batch * n_ctx, d)

    x1, xc1 = _ffn_call(x_lat, x_ctx, mod, tiles_per_batch, ffn_w13, ffn_w2, l, 0, ln_g[l, 0], ln_b[l, 0], tm)

    (_, kt_c, v_c, _, grow_c, _, kat_c, va_c) = _inproj_call(
        xc1, mod, ctx_mod, wa, wt, bg, conv_w[l], cb, None, None, tm_ctx, n_ctx)
    (q_l, kt_l, v_l, o_pre, grow_l, qa_l, kat_l, va_l) = _inproj_call(
        x1, mod, lat_mod, wa, wt, bg, conv_w[l], cb, cos, sin, tm, seq_len)

    c0 = jnp.zeros((n_batch * N_STREAMS, M_DH, M_AUG), F32)
    m0 = jnp.zeros((n_batch * N_STREAMS, LANES), F32)
    c_st, m_st = _mlstm_call(None, kt_c, v_c.reshape(n_batch, n_ctx, M_WIDTH), grow_c, c0, m0,
                             n_batch, n_ctx, False)
    hf, hb, _, _ = _mlstm_call(q_l.reshape(n_batch, seq_len, M_WIDTH), kt_l,
                               v_l.reshape(n_batch, seq_len, M_WIDTH), grow_l, c_st, m_st, n_batch, seq_len, True)
    hf = hf.reshape(n_batch * seq_len, M_WIDTH)
    hb = hb.reshape(n_batch * seq_len, M_WIDTH)

    a_l = _attn_call(attn_sink[l], qa_l, kat_l, va_l, kat_c, va_c, n_batch, seq_len, n_ctx)

    tm_mix = MIX_PARTS * tm
    out = _mix_ffn_call(hf, hb, o_pre, a_l, x1, mod, lambda i: i // (seq_len // tm_mix), mh_norm_g[l],
                        w_out[l].astype(BF16), ln_g[l, 1], ln_b[l, 1], ffn_w13, ffn_w2, l, 1,
                        ln_g[l, 2], ln_b[l, 2], tm_mix)
    return out.reshape(n_batch, seq_len, d)
```

```python
import functools

import jax
import jax.numpy as jnp
import numpy as np
from jax import lax
from jax.experimental import pallas as pl
from jax.experimental.pallas import tpu as pltpu

F32 = jnp.float32
BF16 = jnp.bfloat16

D_MODEL = 1024
GRID_W = 64
M_HEADS = 4
M_DH = 128
M_WIDTH = M_HEADS * M_DH
A_HEADS = 8
A_KV_HEADS = 2
A_DH = 64
A_WIDTH = A_HEADS * A_DH
A_KV_WIDTH = A_KV_HEADS * A_DH
A_BLOCK = 128
CONV_W = 5
D_FF = 2816
ROPE_BASE = 10000.0
N_MOD = 9
LN_EPS = 1e-5
FFN_RES = 0.5
DEPTH = 1
ALPHA = (2.0 * DEPTH) ** 0.25

OFF_QM = 0
OFF_KM = OFF_QM + M_WIDTH
OFF_VM = OFF_KM + M_WIDTH
OFF_OM = OFF_VM + M_WIDTH
OFF_G = OFF_OM + M_WIDTH
N_GATES = 4 * M_HEADS
OFF_QA = OFF_G + N_GATES
OFF_KA = OFF_QA + A_WIDTH
OFF_VA = OFF_KA + A_KV_WIDTH
N_IN = OFF_VA + A_KV_WIDTH

LANES = 128
SUBLANES = 8
MXU_TILE = 256
VMEM_BYTES = 64 * 1024 * 1024
VMEM_LIMIT = VMEM_BYTES * 7 // 8
HALO = SUBLANES
MLSTM_CHUNK = LANES
MLSTM_STEP_CHUNKS = 4
FF_CHUNK = MXU_TILE
COL_BLOCK = MXU_TILE
ROW_PART = LANES
FFN_PARTS = 2
ATT_Q_BLOCKS = 16
MOD_COLS = 2304
GATE_GROUP = SUBLANES


def _sigmoid(x):
    return 1.0 / (1.0 + jnp.exp(-x))


def _log_sigmoid(x):
    return jnp.minimum(x, 0.0) - jnp.log1p(jnp.exp(-jnp.abs(x)))


def _layer_norm(z, g, b):
    mu = jnp.mean(z, axis=-1, keepdims=True)
    zc = z - mu
    var = jnp.mean(zc * zc, axis=-1, keepdims=True)
    return zc * lax.rsqrt(var + LN_EPS) * g + b


def _params(n_grid):
    return pltpu.CompilerParams(dimension_semantics=("arbitrary",) * n_grid, vmem_limit_bytes=VMEM_LIMIT)


def _resident(shape):
    nd = len(shape)
    return pl.BlockSpec(shape, lambda *_: (0,) * nd, pipeline_mode=pl.Buffered(1))


def _mod_kernel(c_ref, w_ref, b_ref, o_ref):
    c = c_ref[...]
    s = (c * _sigmoid(c)).astype(BF16)
    o_ref[...] = jnp.dot(s, w_ref[...].astype(BF16), preferred_element_type=F32) + b_ref[...]


def _mod_call(cc, w_ada, b_ada):
    n = w_ada.shape[1]
    tn = MOD_COLS
    return pl.pallas_call(
        _mod_kernel,
        grid=(n // tn,),
        in_specs=[pl.BlockSpec((SUBLANES, D_MODEL), lambda j: (0, 0)),
                  pl.BlockSpec((D_MODEL, tn), lambda j: (0, j)),
                  pl.BlockSpec((1, tn), lambda j: (0, j))],
        out_specs=pl.BlockSpec((SUBLANES, tn), lambda j: (0, j)),
        out_shape=jax.ShapeDtypeStruct((SUBLANES, n), F32),
        compiler_params=_params(1),
        name="mod",
    )(cc, w_ada, b_ada.reshape(1, n))


def _ffn_body(x, mod_ref, k0, w13_ref, w2_ref, g_ref, b_ref, h_ref, acc_ref, before_chunk=None):
    shift = mod_ref[0, k0:k0 + 1, :]
    scale = mod_ref[0, k0 + 1:k0 + 2, :]
    gate = mod_ref[0, k0 + 2:k0 + 3, :]
    h_ref[...] = (x * (1.0 + scale) + shift).astype(BF16)
    for j in range(D_FF // FF_CHUNK):
        lo = j * FF_CHUNK
        if before_chunk is not None:
            before_chunk(j)
        h = h_ref[...]
        a1 = jnp.dot(h, w13_ref[:, lo:lo + FF_CHUNK], preferred_element_type=F32)
        a3 = jnp.dot(h, w13_ref[:, D_FF + lo:D_FF + lo + FF_CHUNK], preferred_element_type=F32)
        g = (a1 * _sigmoid(a1) * a3).astype(BF16)
        y = jnp.dot(g, w2_ref[lo:lo + FF_CHUNK, :], preferred_element_type=F32)
        if j == 0:
            acc_ref[...] = y
        else:
            acc_ref[...] += y
    z = ALPHA * x + (FFN_RES * gate) * acc_ref[...]
    return _layer_norm(z, g_ref[...], b_ref[...])


def _weight_stager(w13_hbm, w2_hbm, w13_ref, w2_ref, st13_ref, st2_ref, sem, layer, which):
    n_chunks = D_FF // FF_CHUNK

    def copies(j):
        lo, slot = j * FF_CHUNK, j % 2
        return (pltpu.make_async_copy(w13_hbm.at[layer, which, :, lo:lo + FF_CHUNK],
                                      st13_ref.at[slot, 0], sem.at[slot, 0]),
                pltpu.make_async_copy(w13_hbm.at[layer, which, :, D_FF + lo:D_FF + lo + FF_CHUNK],
                                      st13_ref.at[slot, 1], sem.at[slot, 1]),
                pltpu.make_async_copy(w2_hbm.at[layer, which, lo:lo + FF_CHUNK, :],
                                      st2_ref.at[slot], sem.at[slot, 2]))

    def before_chunk(j):
        if j == 0:
            for cp in copies(0):
                cp.start()
        if j + 1 < n_chunks:
            for cp in copies(j + 1):
                cp.start()
        for cp in copies(j):
            cp.wait()
        lo, slot = j * FF_CHUNK, j % 2
        w13_ref[:, lo:lo + FF_CHUNK] = st13_ref[slot, 0].astype(BF16)
        w13_ref[:, D_FF + lo:D_FF + lo + FF_CHUNK] = st13_ref[slot, 1].astype(BF16)
        w2_ref[lo:lo + FF_CHUNK, :] = st2_ref[slot].astype(BF16)

    return before_chunk


def _ffn_kernel(x_ref, xc_ref, mod_ref, w13_hbm, w2_hbm, g_ref, b_ref, o_ref, oc_ref,
                w13_ref, w2_ref, st13_ref, st2_ref, sem, h_ref, acc_ref, *, layer, which):
    i = pl.program_id(0)
    ffn = functools.partial(_ffn_body, mod_ref=mod_ref, k0=0, w13_ref=w13_ref, w2_ref=w2_ref, g_ref=g_ref,
                            b_ref=b_ref)

    @pl.when(i == 0)
    def _():
        stage = _weight_stager(w13_hbm, w2_hbm, w13_ref, w2_ref, st13_ref, st2_ref, sem, layer, which)
        oc_ref[...] = ffn(xc_ref[...], h_ref=h_ref.at[0], acc_ref=acc_ref.at[0], before_chunk=stage)

    @pl.when(i > 0)
    def _():
        rows = o_ref.shape[0] // FFN_PARTS
        for p in range(FFN_PARTS):
            part = slice(p * rows, (p + 1) * rows)
            o_ref[part, :] = ffn(x_ref[part, :], h_ref=h_ref.at[p], acc_ref=acc_ref.at[p])


def _ffn_scratch(rows):
    rows = rows if isinstance(rows, tuple) else (rows,)
    return [pltpu.VMEM((D_MODEL, 2 * D_FF), BF16), pltpu.VMEM((D_FF, D_MODEL), BF16),
            pltpu.VMEM((2, 2, D_MODEL, FF_CHUNK), F32), pltpu.VMEM((2, FF_CHUNK, D_MODEL), F32),
            pltpu.SemaphoreType.DMA((2, 3)),
            pltpu.VMEM(rows + (D_MODEL,), BF16), pltpu.VMEM(rows + (D_MODEL,), F32)]


def _ffn_call(x, x_ctx, mod, tiles_per_batch, ffn_w13, ffn_w2, layer, which, ln_g, ln_b, tm):
    r = x.shape[0]
    n_lat = r // tm
    rows = tm // FFN_PARTS
    assert x_ctx.shape[0] == rows
    n_batch = mod.shape[0] - 1
    lat = lambda i: jnp.maximum(i - 1, 0)
    return pl.pallas_call(
        functools.partial(_ffn_kernel, layer=layer, which=which),
        grid=(n_lat + 1,),
        in_specs=[pl.BlockSpec((tm, D_MODEL), lambda i: (lat(i), 0)),
                  _resident((rows, D_MODEL)),
                  pl.BlockSpec((1, N_MOD, D_MODEL),
                               lambda i: (jnp.where(i == 0, n_batch, lat(i) // tiles_per_batch), 0, 0)),
                  pl.BlockSpec(memory_space=pl.ANY),
                  pl.BlockSpec(memory_space=pl.ANY),
                  _resident((1, D_MODEL)),
                  _resident((1, D_MODEL))],
        out_specs=[pl.BlockSpec((tm, D_MODEL), lambda i: (lat(i), 0)),
                   pl.BlockSpec((rows, D_MODEL), lambda i: (0, 0))],
        out_shape=[jax.ShapeDtypeStruct((r, D_MODEL), F32), jax.ShapeDtypeStruct((rows, D_MODEL), F32)],
        scratch_shapes=_ffn_scratch((FFN_PARTS, rows)),
        compiler_params=_params(1),
        name="ffn",
    )(x, x_ctx, mod, ffn_w13, ffn_w2, ln_g.reshape(1, D_MODEL), ln_b.reshape(1, D_MODEL))


N_WA = 2 * M_WIDTH + 2 * M_WIDTH
OFF_TG = A_WIDTH + A_KV_WIDTH + 2 * A_KV_WIDTH
N_WT = OFF_TG + LANES
ROW_A = (0, 3 * GATE_GROUP)
ROW_B = (GATE_GROUP, 4 * GATE_GROUP)
ROW_C = (2 * GATE_GROUP, 5 * GATE_GROUP)
GATE_ROWS = 6 * GATE_GROUP


def _scan(x, lane, op, reverse):
    fill = 0.0 if op is jnp.add else -jnp.inf
    k = 1
    while k < LANES:
        if reverse:
            other = jnp.where(lane < LANES - k, pltpu.roll(x, LANES - k, 1), fill)
        else:
            other = jnp.where(lane >= k, pltpu.roll(x, k, 1), fill)
        x = op(x, other)
        k *= 2
    return x


def _chunk_scan(x, lane, op, reverse):
    return jnp.concatenate([_scan(x[:, c * LANES:(c + 1) * LANES], lane, op, reverse)
                            for c in range(x.shape[1] // LANES)], axis=1)


def _rope(x, cos, sin_signed, first_half):
    swapped = jnp.where(first_half, pltpu.roll(x, LANES - A_DH // 2, 1), pltpu.roll(x, A_DH // 2, 1))
    return x * cos + swapped * sin_signed


def _inproj_kernel(*refs, tiles_per_seq, rope, tm):
    n_qk = 2 * M_WIDTH // COL_BLOCK
    raw_refs = refs[-n_qk:]
    refs = refs[:-n_qk]
    if rope:
        (xp_ref, x_ref, xn_ref, mod_ref, wa_ref, wt_ref, bg_ref, cw_ref, cb_ref, cos_ref, sin_ref,
         q_ref, kt_ref, v_ref, o_ref, grow_ref, qa_ref, kat_ref, va_ref, h_ref, pt_ref) = refs
    else:
        (xp_ref, x_ref, xn_ref, mod_ref, wa_ref, wt_ref, bg_ref, cw_ref, cb_ref,
         q_ref, kt_ref, v_ref, o_ref, grow_ref, qa_ref, kat_ref, va_ref, h_ref, pt_ref) = refs
    pos = pl.program_id(0) % tiles_per_seq
    slot0 = jnp.minimum(pl.program_id(0), 0)
    n_ext = tm + 2 * HALO
    shift = mod_ref[0, 3:4, :]
    scale1 = 1.0 + mod_ref[0, 4:5, :]
    h_ref[0:tm, :] = (x_ref[...] * scale1 + shift).astype(BF16)
    xh = jnp.concatenate([xp_ref[...], xn_ref[...]], axis=0)
    h_ref[tm:n_ext, :] = (xh * scale1 + shift).astype(BF16)

    def mm(w_ref, lo):
        return jnp.dot(h_ref[0:tm, :], w_ref[:, lo:lo + COL_BLOCK], preferred_element_type=F32)

    def qk_matmul(blk):
        cols = slice(blk * COL_BLOCK, (blk + 1) * COL_BLOCK)
        y = jnp.dot(h_ref[...], wa_ref[:, cols], preferred_element_type=F32)
        raw_refs[blk][0, 0:HALO, :] = jnp.where(pos == 0, 0.0, y[tm:tm + HALO])
        raw_refs[blk][0, HALO:HALO + tm, :] = y[0:tm]
        raw_refs[blk][0, HALO + tm:n_ext, :] = jnp.where(pos == tiles_per_seq - 1, 0.0, y[tm + HALO:n_ext])

    def qk_epilogue(blk, part):
        cols = slice(blk * COL_BLOCK, (blk + 1) * COL_BLOCK)
        r0 = part * ROW_PART
        base = HALO - CONV_W // 2 + r0
        acc = cb_ref[:, cols] + cw_ref[0:1, cols] * raw_refs[blk][slot0, pl.ds(base, ROW_PART), :]
        for j in range(1, CONV_W):
            acc = acc + cw_ref[j:j + 1, cols] * raw_refs[blk][slot0, pl.ds(base + j, ROW_PART), :]
        qk = acc * _sigmoid(acc)
        if blk < M_WIDTH // COL_BLOCK:
            q_ref[r0:r0 + ROW_PART, cols] = (qk * (M_DH ** -0.5)).astype(BF16)
        else:
            kt_ref[0, part, blk * COL_BLOCK - M_WIDTH:(blk + 1) * COL_BLOCK - M_WIDTH, :] = qk.T.astype(BF16)

    def v_matmul(blk):
        v_ref[:, blk * COL_BLOCK:(blk + 1) * COL_BLOCK] = mm(wa_ref, 2 * M_WIDTH + blk * COL_BLOCK).astype(BF16)

    def o_matmul(blk):
        o_ref[:, blk * COL_BLOCK:(blk + 1) * COL_BLOCK] = mm(wa_ref, 3 * M_WIDTH + blk * COL_BLOCK).astype(BF16)

    def t_matmul(blk):
        pt_ref[:, blk * COL_BLOCK:(blk + 1) * COL_BLOCK] = mm(wt_ref, blk * COL_BLOCK)

    def pt_cols(lo, width):
        return pt_ref[:, lo:lo + width]

    def gate_epilogue():
        pgt = (pt_cols(OFF_TG, LANES) + bg_ref[...]).T
        li_f = pgt[0:GATE_GROUP]
        lf_f = _log_sigmoid(pgt[GATE_GROUP:2 * GATE_GROUP])
        li_b = pgt[2 * GATE_GROUP:3 * GATE_GROUP]
        lf_b = _log_sigmoid(pgt[3 * GATE_GROUP:4 * GATE_GROUP])
        lane = lax.broadcasted_iota(jnp.int32, (GATE_GROUP, LANES), 1)
        b_f = _chunk_scan(lf_f, lane, jnp.add, False)
        e_b = _chunk_scan(lf_b, lane, jnp.add, True)
        a_f = li_f - b_f
        a_b = li_b - e_b
        rows = jnp.concatenate([a_f, b_f, _chunk_scan(a_f, lane, jnp.maximum, False),
                                a_b, e_b, _chunk_scan(a_b, lane, jnp.maximum, True)], axis=0)
        for c in range(tm // MLSTM_CHUNK):
            grow_ref[0, c] = rows[:, c * MLSTM_CHUNK:(c + 1) * MLSTM_CHUNK]
        va_ref[:, A_KV_WIDTH:] = pt_cols(A_WIDTH + 2 * A_KV_WIDTH, A_KV_WIDTH).astype(BF16)

    def rotate(lo):
        xg = pt_cols(lo, LANES)
        if not rope:
            return xg
        lane_t = lax.broadcasted_iota(jnp.int32, (tm, LANES), 1)
        return _rope(xg, cos_ref[...], sin_ref[...], (lane_t % A_DH) < (A_DH // 2))

    def qa_epilogue(blk):
        for g in range(blk * COL_BLOCK // LANES, (blk + 1) * COL_BLOCK // LANES):
            qa_ref[:, g * LANES:(g + 1) * LANES] = (rotate(g * LANES) * (A_DH ** -0.5)).astype(BF16)

    def kv_epilogue():
        kat = rotate(A_WIDTH).T.astype(BF16)
        for c in range(tm // A_BLOCK):
            kat_ref[c] = kat[:, c * A_BLOCK:(c + 1) * A_BLOCK]
        va_ref[:, :A_KV_WIDTH] = pt_cols(A_WIDTH + A_KV_WIDTH, A_KV_WIDTH).astype(BF16)

    for blk in range(n_qk):
        qk_matmul(blk)
    for blk in range(N_WT // COL_BLOCK):
        t_matmul(blk)
    for blk in range(M_WIDTH // COL_BLOCK):
        v_matmul(blk)
        o_matmul(blk)
    for blk in range(n_qk):
        for part in range(tm // ROW_PART):
            qk_epilogue(blk, part)
    gate_epilogue()
    qa_epilogue(0)
    qa_epilogue(1)
    kv_epilogue()


def _inproj_call(x, mod, mod_index, wa, wt, bg, conv_w, conv_b, cos, sin, tm, seq_len):
    r = x.shape[0]
    tiles_per_seq = seq_len // tm
    rope = cos is not None
    hb = tm // HALO
    n_halo = r // HALO
    in_specs = [pl.BlockSpec((HALO, D_MODEL), lambda i: (jnp.maximum(i * hb - 1, 0), 0)),
                pl.BlockSpec((tm, D_MODEL), lambda i: (i, 0)),
                pl.BlockSpec((HALO, D_MODEL), lambda i: (jnp.minimum((i + 1) * hb, n_halo - 1), 0)),
                pl.BlockSpec((1, N_MOD, D_MODEL), lambda i: (mod_index(i), 0, 0)),
                _resident((D_MODEL, N_WA)),
                _resident((D_MODEL, N_WT)),
                _resident((1, LANES)),
                _resident((CONV_W, 2 * M_WIDTH)),
                _resident((1, 2 * M_WIDTH))]
    args = [x, x, x, mod, wa, wt, bg, conv_w, conv_b]
    if rope:
        in_specs += [pl.BlockSpec((tm, LANES), lambda i: (i % tiles_per_seq, 0)),
                     pl.BlockSpec((tm, LANES), lambda i: (i % tiles_per_seq, 0))]
        args += [cos, sin]
    assert MLSTM_CHUNK == A_BLOCK == ROW_PART
    n_seq = r // seq_len
    cps = seq_len // MLSTM_CHUNK
    cpt = tm // MLSTM_CHUNK
    row = lambda w: pl.BlockSpec((tm, w), lambda i: (i, 0))
    seq_col = lambda h: pl.BlockSpec((1, cpt, h, MLSTM_CHUNK),
                                     lambda i: (i // tiles_per_seq, i % tiles_per_seq, 0, 0))
    out_specs = [row(M_WIDTH), seq_col(M_WIDTH), row(M_WIDTH), row(M_WIDTH), seq_col(GATE_ROWS),
                 row(A_WIDTH), pl.BlockSpec((cpt, A_KV_WIDTH, A_BLOCK), lambda i: (i, 0, 0)),
                 row(2 * A_KV_WIDTH)]
    out_shape = [jax.ShapeDtypeStruct((r, M_WIDTH), BF16),
                 jax.ShapeDtypeStruct((n_seq, cps, M_WIDTH, MLSTM_CHUNK), BF16),
                 jax.ShapeDtypeStruct((r, M_WIDTH), BF16),
                 jax.ShapeDtypeStruct((r, M_WIDTH), BF16),
                 jax.ShapeDtypeStruct((n_seq, cps, GATE_ROWS, MLSTM_CHUNK), F32),
                 jax.ShapeDtypeStruct((r, A_WIDTH), BF16),
                 jax.ShapeDtypeStruct((r // A_BLOCK, A_KV_WIDTH, A_BLOCK), BF16),
                 jax.ShapeDtypeStruct((r, 2 * A_KV_WIDTH), BF16)]
    return pl.pallas_call(
        functools.partial(_inproj_kernel, tiles_per_seq=tiles_per_seq, rope=rope, tm=tm),
        grid=(r // tm,),
        in_specs=in_specs,
        out_specs=out_specs,
        out_shape=out_shape,
        scratch_shapes=([pltpu.VMEM((tm + 2 * HALO, D_MODEL), BF16), pltpu.VMEM((tm, N_WT), F32)]
                        + [pltpu.VMEM((1, tm + 2 * HALO, COL_BLOCK), F32)] * (2 * M_WIDTH // COL_BLOCK)),
        compiler_params=_params(1),
        name="inproj",
    )(*args)


M_AUG = 2 * M_DH
N_STREAMS = 2 * M_HEADS


def _col_replicated(row):
    return jnp.broadcast_to(row, (LANES, row.shape[1])).T


def _mlstm_kernel(*refs, emit_h):
    if emit_h:
        (qf_ref, ktf_ref, vf_ref, grf_ref, qb_ref, ktb_ref, vb_ref, grb_ref, c0_ref, m0_ref,
         hf_ref, hb_ref, cout_ref, mout_ref, c_ref, m_ref) = refs
        q_refs, h_refs = (qf_ref, qb_ref), (hf_ref, hb_ref)
    else:
        (ktf_ref, vf_ref, grf_ref, ktb_ref, vb_ref, grb_ref, c0_ref, m0_ref,
         cout_ref, mout_ref, c_ref, m_ref) = refs
    kt_refs, v_refs, gr_refs = (ktf_ref, ktb_ref), (vf_ref, vb_ref), (grf_ref, grb_ref)
    L = MLSTM_CHUNK
    n_batch = c_ref.shape[0] // N_STREAMS
    n_sub = grf_ref.shape[1]
    c = pl.program_id(0)

    @pl.when(c == 0)
    def _():
        c_ref[...] = c0_ref[...]
        m_ref[...] = m0_ref[...]

    t_idx = lax.broadcasted_iota(jnp.int32, (L, L), 0)
    s_idx = lax.broadcasted_iota(jnp.int32, (L, L), 1)
    masks = (s_idx <= t_idx, s_idx >= t_idx)
    ones = jnp.ones((L, M_DH), BF16)
    zeros_k = jnp.zeros((M_DH, L), BF16)
    m_all = m_ref[...]
    m_cur = [m_all[r:r + 1] for r in range(n_batch * N_STREAMS)]
    for sub in range(n_sub):
        for b in range(n_batch):
            for d in range(2):
                last = (L - 1, 0)[d]
                ch = sub if d == 0 else n_sub - 1 - sub
                tok = slice(ch * L, (ch + 1) * L)
                gr = gr_refs[d][b, ch]
                for pair in range(M_HEADS // 2):
                    if emit_h:
                        rows = slice(2 * pair * M_DH, (2 * pair + 2) * M_DH)
                        kt2 = kt_refs[d][b, ch, rows, :]
                        rhs = jnp.concatenate(
                            [jnp.concatenate([kt2[:M_DH], zeros_k], axis=1),
                             jnp.concatenate([zeros_k, kt2[M_DH:]], axis=1)], axis=0)
                        s2 = jnp.dot(q_refs[d][b, tok, rows], rhs, preferred_element_type=F32)
                    for hh in range(2):
                        h = 2 * pair + hh
                        r = (b * 2 + d) * M_HEADS + h
                        sl = slice(h * M_DH, (h + 1) * M_DH)
                        a_row = gr[ROW_A[d] + h:ROW_A[d] + h + 1]
                        b_row = gr[ROW_B[d] + h:ROW_B[d] + h + 1]
                        m_row = m_cur[r]
                        ct = c_ref[r]
                        kt = kt_refs[d][b, ch, sl, :]
                        v_aug = jnp.concatenate([v_refs[d][b, tok, sl], ones], axis=1)
                        if emit_h:
                            mu_row = jnp.maximum(m_row, gr[ROW_C[d] + h:ROW_C[d] + h + 1])
                            mu = _col_replicated(mu_row)
                            mt = _col_replicated(b_row + mu_row)
                            dmat = jnp.where(masks[d], jnp.exp(a_row - mu), 0.0)
                            p = (s2[:, hh * L:(hh + 1) * L] * dmat).astype(BF16)
                            qs = (q_refs[d][b, tok, sl].astype(F32) * jnp.exp(m_row - mu)).astype(BF16)
                            tot = jnp.dot(jnp.concatenate([p, qs], axis=1),
                                          jnp.concatenate([v_aug, ct.astype(BF16)], axis=0),
                                          preferred_element_type=F32)
                            h_dir = tot[:, :M_DH] / jnp.maximum(jnp.abs(tot[:, M_DH:]), jnp.exp(-mt))
                            h_refs[d][b, tok, sl] = h_dir.astype(BF16)
                        mp = jnp.maximum(m_row, jnp.max(a_row, axis=1, keepdims=True))
                        ktw = (kt.astype(F32) * jnp.exp(a_row - mp)).astype(BF16)
                        decay = jnp.exp(m_row - mp)
                        c_ref[r] = (jnp.concatenate([decay, decay], axis=1) * ct
                                    + jnp.dot(ktw, v_aug, preferred_element_type=F32))
                        m_cur[r] = b_row[:, last:last + 1] + mp
    m_ref[...] = jnp.concatenate(m_cur, axis=0)

    @pl.when(c == pl.num_programs(0) - 1)
    def _():
        cout_ref[...] = c_ref[...]
        mout_ref[...] = m_ref[...]


def _mlstm_call(q, kt, v, grow, c0, m0, n_batch, seq_len, emit_h):
    L = MLSTM_CHUNK
    n_sub = min(MLSTM_STEP_CHUNKS, seq_len // L)
    nc = seq_len // (L * n_sub)
    fwd = lambda c: c
    bwd = lambda c: nc - 1 - c
    row = lambda w, f: pl.BlockSpec((n_batch, n_sub * L, w), lambda c: (0, f(c), 0))
    col = lambda hgt, f: pl.BlockSpec((n_batch, n_sub, hgt, L), lambda c: (0, f(c), 0, 0))
    n_str = n_batch * N_STREAMS
    state_specs = [pl.BlockSpec((n_str, M_DH, M_AUG), lambda c: (0, 0, 0)),
                   pl.BlockSpec((n_str, LANES), lambda c: (0, 0))]
    state_shapes = [jax.ShapeDtypeStruct((n_str, M_DH, M_AUG), F32),
                    jax.ShapeDtypeStruct((n_str, LANES), F32)]
    if emit_h:
        in_specs = [row(M_WIDTH, fwd), col(M_WIDTH, fwd), row(M_WIDTH, fwd), col(GATE_ROWS, fwd),
                    row(M_WIDTH, bwd), col(M_WIDTH, bwd), row(M_WIDTH, bwd), col(GATE_ROWS, bwd)]
        args = [q, kt, v, grow, q, kt, v, grow]
        out_specs = [row(M_WIDTH, fwd), row(M_WIDTH, bwd)] + state_specs
        out_shape = [jax.ShapeDtypeStruct((n_batch, seq_len, M_WIDTH), BF16)] * 2 + state_shapes
    else:
        in_specs = [col(M_WIDTH, fwd), row(M_WIDTH, fwd), col(GATE_ROWS, fwd),
                    col(M_WIDTH, bwd), row(M_WIDTH, bwd), col(GATE_ROWS, bwd)]
        args = [kt, v, grow, kt, v, grow]
        out_specs = state_specs
        out_shape = state_shapes
    return pl.pallas_call(
        functools.partial(_mlstm_kernel, emit_h=emit_h),
        grid=(nc,),
        in_specs=in_specs + state_specs,
        out_specs=out_specs,
        out_shape=out_shape,
        scratch_shapes=[pltpu.VMEM((n_str, M_DH, M_AUG), F32), pltpu.VMEM((n_str, LANES), F32)],
        compiler_params=_params(1),
        name="mlstm" if emit_h else "mlstm_ctx",
    )(*args, c0, m0)


def _attn_kernel(sink_ref, q_ref, ktp_ref, ktm_ref, ktn_ref, vp_ref, vm_ref, vn_ref, ktx_ref, vx_ref, o_ref,
                 *, n_ctx):
    j = pl.program_id(1)
    nj = pl.num_programs(1)
    T = A_BLOCK
    NQ = ATT_Q_BLOCKS
    n_loc = 3 * T
    n_key = n_loc + n_ctx
    half = LANES // 2
    i_idx = lax.broadcasted_iota(jnp.int32, (T, T), 0)
    r_idx = lax.broadcasted_iota(jnp.int32, (T, T), 1)
    lane_v = lax.broadcasted_iota(jnp.int32, (1, LANES), 1)
    low = lane_v < half
    zk = jnp.zeros((A_DH, n_key), BF16)
    ones_top = jnp.broadcast_to(jnp.where(low, 1.0, 0.0).astype(BF16), (n_key, LANES))
    ones_bot = jnp.broadcast_to(jnp.where(low, 0.0, 1.0).astype(BF16), (n_key, LANES))
    for t in range(NQ):
        rows = slice(t * T, (t + 1) * T)
        prev_ok = (r_idx >= i_idx) & (j > 0) if t == 0 else r_idx >= i_idx
        next_ok = (r_idx <= i_idx) & (j < nj - 1) if t == NQ - 1 else r_idx <= i_idx
        for k in range(A_KV_HEADS):
            ks = slice(k * A_DH, (k + 1) * A_DH)
            kt_prev = ktp_ref[0, ks, :] if t == 0 else ktm_ref[t - 1, ks, :]
            kt_next = ktn_ref[0, ks, :] if t == NQ - 1 else ktm_ref[t + 1, ks, :]
            v_prev = vp_ref[...] if t == 0 else vm_ref[(t - 1) * T:t * T, :]
            v_next = vn_ref[...] if t == NQ - 1 else vm_ref[(t + 1) * T:(t + 2) * T, :]
            v_rows = (v_prev, vm_ref[rows, :], v_next, vx_ref[...])
            kt_all = jnp.concatenate([kt_prev, ktm_ref[t, ks, :], kt_next]
                                     + [ktx_ref[cc, ks, :] for cc in range(n_ctx // T)], axis=1)
            rhs = jnp.concatenate([jnp.concatenate([kt_all, zk], axis=1),
                                   jnp.concatenate([zk, kt_all], axis=1)], axis=0)
            top_sl = slice(0, LANES) if k == 0 else slice(LANES, 2 * LANES)
            bot_sl = slice(LANES, 2 * LANES) if k == 0 else slice(0, LANES)
            v_top = jnp.concatenate([jnp.where(low, vr[:, top_sl], 0) for vr in v_rows], axis=0)
            v_bot = jnp.concatenate([jnp.where(low, 0, vr[:, bot_sl]) for vr in v_rows], axis=0)
            v2 = jnp.concatenate([jnp.concatenate([v_top.astype(BF16), ones_top], axis=1),
                                  jnp.concatenate([v_bot.astype(BF16), ones_bot], axis=1)], axis=0)
            for p in range(A_HEADS // A_KV_HEADS // 2):
                g = k * (A_HEADS // A_KV_HEADS // 2) + p
                q2 = q_ref[rows, g * LANES:(g + 1) * LANES]
                s2 = jnp.dot(q2, rhs, preferred_element_type=F32)
                es = []
                sink_terms = []
                for u in range(2):
                    base = u * n_key
                    sink = sink_ref[2 * g + u]
                    s_prev = jnp.where(prev_ok, s2[:, base:base + T], -jnp.inf)
                    s_cur = s2[:, base + T:base + 2 * T]
                    s_next = jnp.where(next_ok, s2[:, base + 2 * T:base + 3 * T], -jnp.inf)
                    s_ctx = s2[:, base + n_loc:base + n_key]
                    m = jnp.maximum(jnp.maximum(s_prev, s_cur), s_next)
                    for cc in range(n_ctx // T):
                        m = jnp.maximum(m, s_ctx[:, cc * T:(cc + 1) * T])
                    m = jnp.maximum(jnp.max(m, axis=1, keepdims=True), sink)
                    es += [jnp.exp(s_prev - m), jnp.exp(s_cur - m), jnp.exp(s_next - m), jnp.exp(s_ctx - m)]
                    sink_terms.append(jnp.exp(sink - m))
                p2 = jnp.concatenate(es, axis=1).astype(BF16)
                o = jnp.dot(p2, v2, preferred_element_type=F32)
                den = o[:, LANES:] + jnp.where(low, sink_terms[0], sink_terms[1])
                o_ref[rows, g * LANES:(g + 1) * LANES] = (o[:, :LANES] / den).astype(BF16)


def _attn_call(sink, qa, kat, va, kat_c, va_c, n_batch, seq_len, n_ctx):
    T = A_BLOCK
    NQ = ATT_Q_BLOCKS
    nb = seq_len // T
    nj = nb // NQ
    r = n_batch * seq_len
    prv = lambda b, j: b * nb + jnp.maximum(j * NQ - 1, 0)
    nxt = lambda b, j: b * nb + jnp.minimum((j + 1) * NQ, nb - 1)
    kedge = lambda f: pl.BlockSpec((1, A_KV_WIDTH, T), lambda b, j: (f(b, j), 0, 0))
    vedge = lambda f: pl.BlockSpec((T, 2 * A_KV_WIDTH), lambda b, j: (f(b, j), 0))
    return pl.pallas_call(
        functools.partial(_attn_kernel, n_ctx=n_ctx),
        grid=(n_batch, nj),
        in_specs=[pl.BlockSpec(memory_space=pltpu.SMEM),
                  pl.BlockSpec((NQ * T, A_WIDTH), lambda b, j: (b * nj + j, 0)),
                  kedge(prv), pl.BlockSpec((NQ, A_KV_WIDTH, T), lambda b, j: (b * nj + j, 0, 0)), kedge(nxt),
                  vedge(prv), pl.BlockSpec((NQ * T, 2 * A_KV_WIDTH), lambda b, j: (b * nj + j, 0)), vedge(nxt),
                  pl.BlockSpec((n_ctx // T, A_KV_WIDTH, T), lambda b, j: (b, 0, 0)),
                  pl.BlockSpec((n_ctx, 2 * A_KV_WIDTH), lambda b, j: (b, 0))],
        out_specs=pl.BlockSpec((NQ * T, A_WIDTH), lambda b, j: (b * nj + j, 0)),
        out_shape=jax.ShapeDtypeStruct((r, A_WIDTH), BF16),
        compiler_params=_params(2),
        name="attn",
    )(sink, qa, kat, kat, kat, va, va, va, kat_c, va_c)


def _mix_ffn_kernel(hf_ref, hb_ref, op_ref, a_ref, x_ref, mod_ref, mg_ref, wo_ref, g1_ref, b1_ref,
                    w13_hbm, w2_hbm, g2_ref, b2_ref, o_ref,
                    w13_ref, w2_ref, st13_ref, st2_ref, sem, h_ref, acc_ref, *, layer, which):
    @pl.when(pl.program_id(0) == 0)
    def _():
        stage = _weight_stager(w13_hbm, w2_hbm, w13_ref, w2_ref, st13_ref, st2_ref, sem, layer, which)
        for j in range(D_FF // FF_CHUNK):
            stage(j)

    rows_per = hf_ref.shape[0] // FFN_PARTS
    x2 = []
    for p in range(FFN_PARTS):
        rs = slice(p * rows_per, (p + 1) * rows_per)
        h = hf_ref[rs, :].astype(F32) + hb_ref[rs, :].astype(F32)
        gate = _sigmoid(op_ref[rs, :].astype(F32))
        parts = []
        for hd in range(M_HEADS):
            sl = slice(hd * M_DH, (hd + 1) * M_DH)
            seg = h[:, sl]
            mu = jnp.mean(seg, axis=-1, keepdims=True)
            sc = seg - mu
            var = jnp.mean(sc * sc, axis=-1, keepdims=True)
            parts.append((sc * lax.rsqrt(var + LN_EPS) * mg_ref[:, sl] * gate[:, sl]).astype(BF16))
        mixed = jnp.concatenate(parts + [a_ref[rs, :]], axis=1)
        y = jnp.dot(mixed, wo_ref[...], preferred_element_type=F32)
        x2.append(_layer_norm(ALPHA * x_ref[rs, :] + mod_ref[0, 5:6, :] * y, g1_ref[...], b1_ref[...]))
    for p in range(FFN_PARTS):
        rs = slice(p * rows_per, (p + 1) * rows_per)
        o_ref[rs, :] = _ffn_body(x2[p], mod_ref, 6, w13_ref, w2_ref, g2_ref, b2_ref, h_ref.at[p], acc_ref.at[p])


def _mix_ffn_call(hf, hb, o_pre, a_l, x1, mod, mod_index, mh_g, w_out, g1, b1, ffn_w13, ffn_w2, layer, which,
                  g2, b2, tm):
    r = x1.shape[0]
    row = lambda w: pl.BlockSpec((tm, w), lambda i: (i, 0))
    vec = lambda a: a.reshape(1, a.shape[-1])
    return pl.pallas_call(
        functools.partial(_mix_ffn_kernel, layer=layer, which=which),
        grid=(r // tm,),
        in_specs=[row(M_WIDTH), row(M_WIDTH), row(M_WIDTH), row(A_WIDTH), row(D_MODEL),
                  pl.BlockSpec((1, N_MOD, D_MODEL), lambda i: (mod_index(i), 0, 0)),
                  _resident((1, M_WIDTH)),
                  _resident((D_MODEL, D_MODEL)),
                  _resident((1, D_MODEL)),
                  _resident((1, D_MODEL)),
                  pl.BlockSpec(memory_space=pl.ANY),
                  pl.BlockSpec(memory_space=pl.ANY),
                  _resident((1, D_MODEL)),
                  _resident((1, D_MODEL))],
        out_specs=row(D_MODEL),
        out_shape=jax.ShapeDtypeStruct((r, D_MODEL), F32),
        scratch_shapes=_ffn_scratch((FFN_PARTS, tm // FFN_PARTS)),
        compiler_params=_params(1),
        name="mix_ffn",
    )(hf, hb, o_pre, a_l, x1, mod, vec(mh_g), w_out, vec(g1), vec(b1), ffn_w13, ffn_w2, vec(g2), vec(b2))


def _rope_tables(seq_len):
    t = np.arange(seq_len)
    row = (t // GRID_W).astype(np.float64)
    col = (t % GRID_W).astype(np.float64)
    n_freq = A_DH // 4
    inv = ROPE_BASE ** (-np.arange(n_freq, dtype=np.float64) / n_freq)
    ang = np.concatenate([row[:, None] * inv, col[:, None] * inv], -1)
    cos, sin = np.cos(ang), np.sin(ang)
    return (jnp.asarray(np.tile(cos, (1, 4)), F32),
            jnp.asarray(np.tile(np.concatenate([-sin, sin], -1), (1, 2)), F32))


def _spread_gates(a):
    lead = a.shape[:-1]
    a = a.reshape(lead + (4, M_HEADS))
    a = jnp.pad(a, [(0, 0)] * len(lead) + [(0, 0), (0, GATE_GROUP - M_HEADS)])
    a = a.reshape(lead + (4 * GATE_GROUP,))
    return jnp.pad(a, [(0, 0)] * len(lead) + [(0, LANES - 4 * GATE_GROUP)])


def kernel(x, c, ctx, c_ctx, w_ada, b_ada, ln_g, ln_b, ffn_w13, ffn_w2, w_in, b_gates, conv_w, conv_b,
           mh_norm_g, attn_sink, w_out):
    n_batch, seq_len, d = x.shape
    n_ctx = ctx.shape[1]
    l = 0
    tm = 512
    tm_ctx = n_ctx
    tm_ffn = FFN_PARTS * tm
    tiles_per_batch = seq_len // tm

    cc = jnp.concatenate([c, c_ctx[None, :], jnp.zeros((SUBLANES - n_batch - 1, d), F32)], axis=0)
    mod = _mod_call(cc, w_ada[l], b_ada[l])[:n_batch + 1].reshape(n_batch + 1, N_MOD, d)
    lat_mod = lambda i: i // tiles_per_batch
    ctx_mod = lambda i: n_batch

    wi = w_in[l]
    wa = wi[:, OFF_QM:OFF_G].astype(BF16)
    w_va = wi[:, OFF_VA:N_IN]
    wt = jnp.concatenate([wi[:, OFF_QA:OFF_VA], w_va, w_va[:, A_DH:], w_va[:, :A_DH],
                          _spread_gates(wi[:, OFF_G:OFF_QA])], axis=1).astype(BF16)
    bg = _spread_gates(b_gates[l]).reshape(1, LANES)
    cb = conv_b[l].reshape(1, 2 * M_WIDTH)
    cos, sin = _rope_tables(seq_len)

    x_lat = x.reshape(n_batch * seq_len, d)
    x_ctx = ctx.reshape(n_batch * n_ctx, d)

    x1, xc1 = _ffn_call(x_lat, x_ctx, mod, seq_len // tm_ffn, ffn_w13, ffn_w2, l, 0, ln_g[l, 0], ln_b[l, 0],
                        tm_ffn)

    (_, kt_c, v_c, _, grow_c, _, kat_c, va_c) = _inproj_call(
        xc1, mod, ctx_mod, wa, wt, bg, conv_w[l], cb, None, None, tm_ctx, n_ctx)
    (q_l, kt_l, v_l, o_pre, grow_l, qa_l, kat_l, va_l) = _inproj_call(
        x1, mod, lat_mod, wa, wt, bg, conv_w[l], cb, cos, sin, tm, seq_len)

    c0 = jnp.zeros((n_batch * N_STREAMS, M_DH, M_AUG), F32)
    m0 = jnp.zeros((n_batch * N_STREAMS, LANES), F32)
    c_st, m_st = _mlstm_call(None, kt_c, v_c.reshape(n_batch, n_ctx, M_WIDTH), grow_c, c0, m0,
                             n_batch, n_ctx, False)
    hf, hb, _, _ = _mlstm_call(q_l.reshape(n_batch, seq_len, M_WIDTH), kt_l,
                               v_l.reshape(n_batch, seq_len, M_WIDTH), grow_l, c_st, m_st, n_batch, seq_len, True)
    hf = hf.reshape(n_batch * seq_len, M_WIDTH)
    hb = hb.reshape(n_batch * seq_len, M_WIDTH)

    a_l = _attn_call(attn_sink[l], qa_l, kat_l, va_l, kat_c, va_c, n_batch, seq_len, n_ctx)

    out = _mix_ffn_call(hf, hb, o_pre, a_l, x1, mod, lambda i: i // (seq_len // tm_ffn), mh_norm_g[l],
                        w_out[l].astype(BF16), ln_g[l, 1], ln_b[l, 1], ffn_w13, ffn_w2, l, 1,
                        ln_g[l, 2], ln_b[l, 2], tm_ffn)
    return out.reshape(n_batch, seq_len, d)
```

```python
import functools

import jax
import jax.numpy as jnp
import numpy as np
from jax import lax
from jax.experimental import pallas as pl
from jax.experimental.pallas import tpu as pltpu

F32 = jnp.float32
BF16 = jnp.bfloat16

D_MODEL = 1024
GRID_W = 64
M_HEADS = 4
M_DH = 128
M_WIDTH = M_HEADS * M_DH
A_HEADS = 8
A_KV_HEADS = 2
A_DH = 64
A_WIDTH = A_HEADS * A_DH
A_KV_WIDTH = A_KV_HEADS * A_DH
A_BLOCK = 128
CONV_W = 5
D_FF = 2816
ROPE_BASE = 10000.0
N_MOD = 9
LN_EPS = 1e-5
FFN_RES = 0.5
DEPTH = 1
ALPHA = (2.0 * DEPTH) ** 0.25

OFF_QM = 0
OFF_KM = OFF_QM + M_WIDTH
OFF_VM = OFF_KM + M_WIDTH
OFF_OM = OFF_VM + M_WIDTH
OFF_G = OFF_OM + M_WIDTH
N_GATES = 4 * M_HEADS
OFF_QA = OFF_G + N_GATES
OFF_KA = OFF_QA + A_WIDTH
OFF_VA = OFF_KA + A_KV_WIDTH
N_IN = OFF_VA + A_KV_WIDTH

LANES = 128
SUBLANES = 8
MXU_TILE = 256
VMEM_BYTES = 64 * 1024 * 1024
VMEM_LIMIT = VMEM_BYTES * 7 // 8
HALO = SUBLANES
MLSTM_CHUNK = LANES
MLSTM_STEP_CHUNKS = 4
FF_CHUNK = MXU_TILE
COL_BLOCK = MXU_TILE
ROW_PART = LANES
FFN_PARTS = 2
ATT_Q_BLOCKS = 16
MOD_COLS = 2304
MOD_STREAMS = 2
GATE_GROUP = SUBLANES


def _sigmoid(x):
    return 1.0 / (1.0 + jnp.exp(-x))


def _log_sigmoid(x):
    return jnp.minimum(x, 0.0) - jnp.log1p(jnp.exp(-jnp.abs(x)))


def _layer_norm(z, g, b):
    mu = jnp.mean(z, axis=-1, keepdims=True)
    zc = z - mu
    var = jnp.mean(zc * zc, axis=-1, keepdims=True)
    return zc * lax.rsqrt(var + LN_EPS) * g + b


def _params(n_grid):
    return pltpu.CompilerParams(dimension_semantics=("arbitrary",) * n_grid, vmem_limit_bytes=VMEM_LIMIT)


def _resident(shape):
    nd = len(shape)
    return pl.BlockSpec(shape, lambda *_: (0,) * nd, pipeline_mode=pl.Buffered(1))


def _mod_kernel(c_ref, *refs):
    *w_refs, b_ref, o_ref = refs
    c = c_ref[...]
    s = (c * _sigmoid(c)).astype(BF16)
    tn = w_refs[0].shape[1]
    for k, w_ref in enumerate(w_refs):
        cols = slice(k * tn, (k + 1) * tn)
        o_ref[:, cols] = jnp.dot(s, w_ref[...].astype(BF16), preferred_element_type=F32) + b_ref[:, cols]


def _mod_call(cc, w_ada, b_ada):
    n = w_ada.shape[1]
    tn = MOD_COLS
    w_specs = [pl.BlockSpec((D_MODEL, tn // MOD_STREAMS), lambda j, k=k: (0, MOD_STREAMS * j + k))
               for k in range(MOD_STREAMS)]
    return pl.pallas_call(
        _mod_kernel,
        grid=(n // tn,),
        in_specs=[pl.BlockSpec((SUBLANES, D_MODEL), lambda j: (0, 0))] + w_specs
                 + [pl.BlockSpec((1, tn), lambda j: (0, j))],
        out_specs=pl.BlockSpec((SUBLANES, tn), lambda j: (0, j)),
        out_shape=jax.ShapeDtypeStruct((SUBLANES, n), F32),
        compiler_params=_params(1),
        name="mod",
    )(cc, *([w_ada] * MOD_STREAMS), b_ada.reshape(1, n))


def _ffn_body(x, mod_ref, k0, w13_ref, w2_ref, g_ref, b_ref, h_ref, acc_ref, before_chunk=None):
    shift = mod_ref[0, k0:k0 + 1, :]
    scale = mod_ref[0, k0 + 1:k0 + 2, :]
    gate = mod_ref[0, k0 + 2:k0 + 3, :]
    h_ref[...] = (x * (1.0 + scale) + shift).astype(BF16)
    for j in range(D_FF // FF_CHUNK):
        lo = j * FF_CHUNK
        if before_chunk is not None:
            before_chunk(j)
        h = h_ref[...]
        a1 = jnp.dot(h, w13_ref[:, lo:lo + FF_CHUNK], preferred_element_type=F32)
        a3 = jnp.dot(h, w13_ref[:, D_FF + lo:D_FF + lo + FF_CHUNK], preferred_element_type=F32)
        g = (a1 * _sigmoid(a1) * a3).astype(BF16)
        y = jnp.dot(g, w2_ref[lo:lo + FF_CHUNK, :], preferred_element_type=F32)
        if j == 0:
            acc_ref[...] = y
        else:
            acc_ref[...] += y
    z = ALPHA * x + (FFN_RES * gate) * acc_ref[...]
    return _layer_norm(z, g_ref[...], b_ref[...])


def _weight_stager(w13_hbm, w2_hbm, w13_ref, w2_ref, st13_ref, st2_ref, sem, layer, which):
    n_chunks = D_FF // FF_CHUNK

    def copies(j):
        lo, slot = j * FF_CHUNK, j % 2
        return (pltpu.make_async_copy(w13_hbm.at[layer, which, :, lo:lo + FF_CHUNK],
                                      st13_ref.at[slot, 0], sem.at[slot, 0]),
                pltpu.make_async_copy(w13_hbm.at[layer, which, :, D_FF + lo:D_FF + lo + FF_CHUNK],
                                      st13_ref.at[slot, 1], sem.at[slot, 1]),
                pltpu.make_async_copy(w2_hbm.at[layer, which, lo:lo + FF_CHUNK, :],
                                      st2_ref.at[slot], sem.at[slot, 2]))

    def before_chunk(j):
        if j == 0:
            for cp in copies(0):
                cp.start()
        if j + 1 < n_chunks:
            for cp in copies(j + 1):
                cp.start()
        for cp in copies(j):
            cp.wait()
        lo, slot = j * FF_CHUNK, j % 2
        w13_ref[:, lo:lo + FF_CHUNK] = st13_ref[slot, 0].astype(BF16)
        w13_ref[:, D_FF + lo:D_FF + lo + FF_CHUNK] = st13_ref[slot, 1].astype(BF16)
        w2_ref[lo:lo + FF_CHUNK, :] = st2_ref[slot].astype(BF16)

    return before_chunk


def _ffn_kernel(x_ref, xc_ref, mod_ref, w13_hbm, w2_hbm, g_ref, b_ref, o_ref, oc_ref,
                w13_ref, w2_ref, st13_ref, st2_ref, sem, h_ref, acc_ref, *, layer, which):
    i = pl.program_id(0)
    ffn = functools.partial(_ffn_body, mod_ref=mod_ref, k0=0, w13_ref=w13_ref, w2_ref=w2_ref, g_ref=g_ref,
                            b_ref=b_ref)

    @pl.when(i == 0)
    def _():
        stage = _weight_stager(w13_hbm, w2_hbm, w13_ref, w2_ref, st13_ref, st2_ref, sem, layer, which)
        oc_ref[...] = ffn(xc_ref[...], h_ref=h_ref.at[0], acc_ref=acc_ref.at[0], before_chunk=stage)

    @pl.when(i > 0)
    def _():
        rows = o_ref.shape[0] // FFN_PARTS
        for p in range(FFN_PARTS):
            part = slice(p * rows, (p + 1) * rows)
            o_ref[part, :] = ffn(x_ref[part, :], h_ref=h_ref.at[p], acc_ref=acc_ref.at[p])


def _ffn_scratch(rows):
    rows = rows if isinstance(rows, tuple) else (rows,)
    return [pltpu.VMEM((D_MODEL, 2 * D_FF), BF16), pltpu.VMEM((D_FF, D_MODEL), BF16),
            pltpu.VMEM((2, 2, D_MODEL, FF_CHUNK), F32), pltpu.VMEM((2, FF_CHUNK, D_MODEL), F32),
            pltpu.SemaphoreType.DMA((2, 3)),
            pltpu.VMEM(rows + (D_MODEL,), BF16), pltpu.VMEM(rows + (D_MODEL,), F32)]


def _ffn_call(x, x_ctx, mod, tiles_per_batch, ffn_w13, ffn_w2, layer, which, ln_g, ln_b, tm):
    r = x.shape[0]
    n_lat = r // tm
    rows = tm // FFN_PARTS
    assert x_ctx.shape[0] == rows
    n_batch = mod.shape[0] - 1
    lat = lambda i: jnp.maximum(i - 1, 0)
    return pl.pallas_call(
        functools.partial(_ffn_kernel, layer=layer, which=which),
        grid=(n_lat + 1,),
        in_specs=[pl.BlockSpec((tm, D_MODEL), lambda i: (lat(i), 0)),
                  _resident((rows, D_MODEL)),
                  pl.BlockSpec((1, N_MOD, D_MODEL),
                               lambda i: (jnp.where(i == 0, n_batch, lat(i) // tiles_per_batch), 0, 0)),
                  pl.BlockSpec(memory_space=pl.ANY),
                  pl.BlockSpec(memory_space=pl.ANY),
                  _resident((1, D_MODEL)),
                  _resident((1, D_MODEL))],
        out_specs=[pl.BlockSpec((tm, D_MODEL), lambda i: (lat(i), 0)),
                   pl.BlockSpec((rows, D_MODEL), lambda i: (0, 0))],
        out_shape=[jax.ShapeDtypeStruct((r, D_MODEL), F32), jax.ShapeDtypeStruct((rows, D_MODEL), F32)],
        scratch_shapes=_ffn_scratch((FFN_PARTS, rows)),
        compiler_params=_params(1),
        name="ffn",
    )(x, x_ctx, mod, ffn_w13, ffn_w2, ln_g.reshape(1, D_MODEL), ln_b.reshape(1, D_MODEL))


N_WA = 2 * M_WIDTH + 2 * M_WIDTH
OFF_TG = A_WIDTH + A_KV_WIDTH + 2 * A_KV_WIDTH
N_WT = OFF_TG + LANES
ROW_A = (0, 3 * GATE_GROUP)
ROW_B = (GATE_GROUP, 4 * GATE_GROUP)
ROW_C = (2 * GATE_GROUP, 5 * GATE_GROUP)
GATE_ROWS = 6 * GATE_GROUP


def _scan(x, lane, op, reverse):
    fill = 0.0 if op is jnp.add else -jnp.inf
    k = 1
    while k < LANES:
        if reverse:
            other = jnp.where(lane < LANES - k, pltpu.roll(x, LANES - k, 1), fill)
        else:
            other = jnp.where(lane >= k, pltpu.roll(x, k, 1), fill)
        x = op(x, other)
        k *= 2
    return x


def _chunk_scan(x, lane, op, reverse):
    return jnp.concatenate([_scan(x[:, c * LANES:(c + 1) * LANES], lane, op, reverse)
                            for c in range(x.shape[1] // LANES)], axis=1)


def _rope(x, cos, sin_signed, first_half):
    swapped = jnp.where(first_half, pltpu.roll(x, LANES - A_DH // 2, 1), pltpu.roll(x, A_DH // 2, 1))
    return x * cos + swapped * sin_signed


def _inproj_kernel(*refs, tiles_per_seq, rope, tm):
    n_qk = 2 * M_WIDTH // COL_BLOCK
    raw_refs = refs[-n_qk:]
    refs = refs[:-n_qk]
    if rope:
        (xp_ref, x_ref, xn_ref, mod_ref, wa_ref, wt_ref, bg_ref, cw_ref, cb_ref, cos_ref, sin_ref,
         q_ref, kt_ref, v_ref, o_ref, grow_ref, qa_ref, kat_ref, va_ref, h_ref, pt_ref) = refs
    else:
        (xp_ref, x_ref, xn_ref, mod_ref, wa_ref, wt_ref, bg_ref, cw_ref, cb_ref,
         q_ref, kt_ref, v_ref, o_ref, grow_ref, qa_ref, kat_ref, va_ref, h_ref, pt_ref) = refs
    pos = pl.program_id(0) % tiles_per_seq
    slot0 = jnp.minimum(pl.program_id(0), 0)
    n_ext = tm + 2 * HALO
    shift = mod_ref[0, 3:4, :]
    scale1 = 1.0 + mod_ref[0, 4:5, :]
    h_ref[0:tm, :] = (x_ref[...] * scale1 + shift).astype(BF16)
    xh = jnp.concatenate([xp_ref[...], xn_ref[...]], axis=0)
    h_ref[tm:n_ext, :] = (xh * scale1 + shift).astype(BF16)

    def mm(w_ref, lo):
        return jnp.dot(h_ref[0:tm, :], w_ref[:, lo:lo + COL_BLOCK], preferred_element_type=F32)

    def qk_matmul(blk):
        cols = slice(blk * COL_BLOCK, (blk + 1) * COL_BLOCK)
        y = jnp.dot(h_ref[...], wa_ref[:, cols], preferred_element_type=F32)
        raw_refs[blk][0, 0:HALO, :] = jnp.where(pos == 0, 0.0, y[tm:tm + HALO])
        raw_refs[blk][0, HALO:HALO + tm, :] = y[0:tm]
        raw_refs[blk][0, HALO + tm:n_ext, :] = jnp.where(pos == tiles_per_seq - 1, 0.0, y[tm + HALO:n_ext])

    def qk_epilogue(blk, part):
        cols = slice(blk * COL_BLOCK, (blk + 1) * COL_BLOCK)
        r0 = part * ROW_PART
        base = HALO - CONV_W // 2 + r0
        acc = cb_ref[:, cols] + cw_ref[0:1, cols] * raw_refs[blk][slot0, pl.ds(base, ROW_PART), :]
        for j in range(1, CONV_W):
            acc = acc + cw_ref[j:j + 1, cols] * raw_refs[blk][slot0, pl.ds(base + j, ROW_PART), :]
        qk = acc * _sigmoid(acc)
        if blk < M_WIDTH // COL_BLOCK:
            q_ref[r0:r0 + ROW_PART, cols] = (qk * (M_DH ** -0.5)).astype(BF16)
        else:
            kt_ref[0, part, blk * COL_BLOCK - M_WIDTH:(blk + 1) * COL_BLOCK - M_WIDTH, :] = qk.T.astype(BF16)

    def v_matmul(blk):
        v_ref[:, blk * COL_BLOCK:(blk + 1) * COL_BLOCK] = mm(wa_ref, 2 * M_WIDTH + blk * COL_BLOCK).astype(BF16)

    def o_matmul(blk):
        o_ref[:, blk * COL_BLOCK:(blk + 1) * COL_BLOCK] = mm(wa_ref, 3 * M_WIDTH + blk * COL_BLOCK).astype(BF16)

    def t_matmul(blk):
        pt_ref[:, blk * COL_BLOCK:(blk + 1) * COL_BLOCK] = mm(wt_ref, blk * COL_BLOCK)

    def pt_cols(lo, width):
        return pt_ref[:, lo:lo + width]

    def gate_epilogue():
        pgt = (pt_cols(OFF_TG, LANES) + bg_ref[...]).T
        li_f = pgt[0:GATE_GROUP]
        lf_f = _log_sigmoid(pgt[GATE_GROUP:2 * GATE_GROUP])
        li_b = pgt[2 * GATE_GROUP:3 * GATE_GROUP]
        lf_b = _log_sigmoid(pgt[3 * GATE_GROUP:4 * GATE_GROUP])
        lane = lax.broadcasted_iota(jnp.int32, (GATE_GROUP, LANES), 1)
        b_f = _chunk_scan(lf_f, lane, jnp.add, False)
        e_b = _chunk_scan(lf_b, lane, jnp.add, True)
        a_f = li_f - b_f
        a_b = li_b - e_b
        rows = jnp.concatenate([a_f, b_f, _chunk_scan(a_f, lane, jnp.maximum, False),
                                a_b, e_b, _chunk_scan(a_b, lane, jnp.maximum, True)], axis=0)
        for c in range(tm // MLSTM_CHUNK):
            grow_ref[0, c] = rows[:, c * MLSTM_CHUNK:(c + 1) * MLSTM_CHUNK]
        va_ref[:, A_KV_WIDTH:] = pt_cols(A_WIDTH + 2 * A_KV_WIDTH, A_KV_WIDTH).astype(BF16)

    def rotate(lo):
        xg = pt_cols(lo, LANES)
        if not rope:
            return xg
        lane_t = lax.broadcasted_iota(jnp.int32, (tm, LANES), 1)
        return _rope(xg, cos_ref[...], sin_ref[...], (lane_t % A_DH) < (A_DH // 2))

    def qa_epilogue(blk):
        for g in range(blk * COL_BLOCK // LANES, (blk + 1) * COL_BLOCK // LANES):
            qa_ref[:, g * LANES:(g + 1) * LANES] = (rotate(g * LANES) * (A_DH ** -0.5)).astype(BF16)

    def kv_epilogue():
        kat = rotate(A_WIDTH).T.astype(BF16)
        for c in range(tm // A_BLOCK):
            kat_ref[c] = kat[:, c * A_BLOCK:(c + 1) * A_BLOCK]
        va_ref[:, :A_KV_WIDTH] = pt_cols(A_WIDTH + A_KV_WIDTH, A_KV_WIDTH).astype(BF16)

    for blk in range(n_qk):
        qk_matmul(blk)
    for blk in range(N_WT // COL_BLOCK):
        t_matmul(blk)
    for blk in range(M_WIDTH // COL_BLOCK):
        v_matmul(blk)
        o_matmul(blk)
    for blk in range(n_qk):
        for part in range(tm // ROW_PART):
            qk_epilogue(blk, part)
    gate_epilogue()
    qa_epilogue(0)
    qa_epilogue(1)
    kv_epilogue()


def _inproj_call(x, mod, mod_index, wa, wt, bg, conv_w, conv_b, cos, sin, tm, seq_len):
    r = x.shape[0]
    tiles_per_seq = seq_len // tm
    rope = cos is not None
    hb = tm // HALO
    n_halo = r // HALO
    in_specs = [pl.BlockSpec((HALO, D_MODEL), lambda i: (jnp.maximum(i * hb - 1, 0), 0)),
                pl.BlockSpec((tm, D_MODEL), lambda i: (i, 0)),
                pl.BlockSpec((HALO, D_MODEL), lambda i: (jnp.minimum((i + 1) * hb, n_halo - 1), 0)),
                pl.BlockSpec((1, N_MOD, D_MODEL), lambda i: (mod_index(i), 0, 0)),
                _resident((D_MODEL, N_WA)),
                _resident((D_MODEL, N_WT)),
                _resident((1, LANES)),
                _resident((CONV_W, 2 * M_WIDTH)),
                _resident((1, 2 * M_WIDTH))]
    args = [x, x, x, mod, wa, wt, bg, conv_w, conv_b]
    if rope:
        in_specs += [pl.BlockSpec((tm, LANES), lambda i: (i % tiles_per_seq, 0)),
                     pl.BlockSpec((tm, LANES), lambda i: (i % tiles_per_seq, 0))]
        args += [cos, sin]
    assert MLSTM_CHUNK == A_BLOCK == ROW_PART
    n_seq = r // seq_len
    cps = seq_len // MLSTM_CHUNK
    cpt = tm // MLSTM_CHUNK
    row = lambda w: pl.BlockSpec((tm, w), lambda i: (i, 0))
    seq_col = lambda h: pl.BlockSpec((1, cpt, h, MLSTM_CHUNK),
                                     lambda i: (i // tiles_per_seq, i % tiles_per_seq, 0, 0))
    out_specs = [row(M_WIDTH), seq_col(M_WIDTH), row(M_WIDTH), row(M_WIDTH), seq_col(GATE_ROWS),
                 row(A_WIDTH), pl.BlockSpec((cpt, A_KV_WIDTH, A_BLOCK), lambda i: (i, 0, 0)),
                 row(2 * A_KV_WIDTH)]
    out_shape = [jax.ShapeDtypeStruct((r, M_WIDTH), BF16),
                 jax.ShapeDtypeStruct((n_seq, cps, M_WIDTH, MLSTM_CHUNK), BF16),
                 jax.ShapeDtypeStruct((r, M_WIDTH), BF16),
                 jax.ShapeDtypeStruct((r, M_WIDTH), BF16),
                 jax.ShapeDtypeStruct((n_seq, cps, GATE_ROWS, MLSTM_CHUNK), F32),
                 jax.ShapeDtypeStruct((r, A_WIDTH), BF16),
                 jax.ShapeDtypeStruct((r // A_BLOCK, A_KV_WIDTH, A_BLOCK), BF16),
                 jax.ShapeDtypeStruct((r, 2 * A_KV_WIDTH), BF16)]
    return pl.pallas_call(
        functools.partial(_inproj_kernel, tiles_per_seq=tiles_per_seq, rope=rope, tm=tm),
        grid=(r // tm,),
        in_specs=in_specs,
        out_specs=out_specs,
        out_shape=out_shape,
        scratch_shapes=([pltpu.VMEM((tm + 2 * HALO, D_MODEL), BF16), pltpu.VMEM((tm, N_WT), F32)]
                        + [pltpu.VMEM((1, tm + 2 * HALO, COL_BLOCK), F32)] * (2 * M_WIDTH // COL_BLOCK)),
        compiler_params=_params(1),
        name="inproj",
    )(*args)


M_AUG = 2 * M_DH
N_STREAMS = 2 * M_HEADS


def _col_replicated(row):
    return jnp.broadcast_to(row, (LANES, row.shape[1])).T


def _mlstm_kernel(*refs, emit_h):
    if emit_h:
        (qf_ref, ktf_ref, vf_ref, grf_ref, qb_ref, ktb_ref, vb_ref, grb_ref, c0_ref, m0_ref,
         hf_ref, hb_ref, cout_ref, mout_ref, c_ref, m_ref) = refs
        q_refs, h_refs = (qf_ref, qb_ref), (hf_ref, hb_ref)
    else:
        (ktf_ref, vf_ref, grf_ref, ktb_ref, vb_ref, grb_ref, c0_ref, m0_ref,
         cout_ref, mout_ref, c_ref, m_ref) = refs
    kt_refs, v_refs, gr_refs = (ktf_ref, ktb_ref), (vf_ref, vb_ref), (grf_ref, grb_ref)
    L = MLSTM_CHUNK
    n_batch = c_ref.shape[0] // N_STREAMS
    n_sub = grf_ref.shape[1]
    c = pl.program_id(0)

    @pl.when(c == 0)
    def _():
        c_ref[...] = c0_ref[...]
        m_ref[...] = m0_ref[...]

    t_idx = lax.broadcasted_iota(jnp.int32, (L, L), 0)
    s_idx = lax.broadcasted_iota(jnp.int32, (L, L), 1)
    masks = (s_idx <= t_idx, s_idx >= t_idx)
    ones = jnp.ones((L, M_DH), BF16)
    zeros_k = jnp.zeros((M_DH, L), BF16)
    m_all = m_ref[...]
    m_cur = [m_all[r:r + 1] for r in range(n_batch * N_STREAMS)]
    for sub in range(n_sub):
        for b in range(n_batch):
            for d in range(2):
                last = (L - 1, 0)[d]
                ch = sub if d == 0 else n_sub - 1 - sub
                tok = slice(ch * L, (ch + 1) * L)
                gr = gr_refs[d][b, ch]
                for pair in range(M_HEADS // 2):
                    if emit_h:
                        rows = slice(2 * pair * M_DH, (2 * pair + 2) * M_DH)
                        kt2 = kt_refs[d][b, ch, rows, :]
                        rhs = jnp.concatenate(
                            [jnp.concatenate([kt2[:M_DH], zeros_k], axis=1),
                             jnp.concatenate([zeros_k, kt2[M_DH:]], axis=1)], axis=0)
                        s2 = jnp.dot(q_refs[d][b, tok, rows], rhs, preferred_element_type=F32)
                    for hh in range(2):
                        h = 2 * pair + hh
                        r = (b * 2 + d) * M_HEADS + h
                        sl = slice(h * M_DH, (h + 1) * M_DH)
                        a_row = gr[ROW_A[d] + h:ROW_A[d] + h + 1]
                        b_row = gr[ROW_B[d] + h:ROW_B[d] + h + 1]
                        m_row = m_cur[r]
                        ct = c_ref[r]
                        kt = kt_refs[d][b, ch, sl, :]
                        v_aug = jnp.concatenate([v_refs[d][b, tok, sl], ones], axis=1)
                        if emit_h:
                            mu_row = jnp.maximum(m_row, gr[ROW_C[d] + h:ROW_C[d] + h + 1])
                            mu = _col_replicated(mu_row)
                            mt = _col_replicated(b_row + mu_row)
                            dmat = jnp.where(masks[d], jnp.exp(a_row - mu), 0.0)
                            p = (s2[:, hh * L:(hh + 1) * L] * dmat).astype(BF16)
                            qs = (q_refs[d][b, tok, sl].astype(F32) * jnp.exp(m_row - mu)).astype(BF16)
                            tot = jnp.dot(jnp.concatenate([p, qs], axis=1),
                                          jnp.concatenate([v_aug, ct.astype(BF16)], axis=0),
                                          preferred_element_type=F32)
                            h_dir = tot[:, :M_DH] / jnp.maximum(jnp.abs(tot[:, M_DH:]), jnp.exp(-mt))
                            h_refs[d][b, tok, sl] = h_dir.astype(BF16)
                        mp = jnp.maximum(m_row, jnp.max(a_row, axis=1, keepdims=True))
                        ktw = (kt.astype(F32) * jnp.exp(a_row - mp)).astype(BF16)
                        decay = jnp.exp(m_row - mp)
                        c_ref[r] = (jnp.concatenate([decay, decay], axis=1) * ct
                                    + jnp.dot(ktw, v_aug, preferred_element_type=F32))
                        m_cur[r] = b_row[:, last:last + 1] + mp
    m_ref[...] = jnp.concatenate(m_cur, axis=0)

    @pl.when(c == pl.num_programs(0) - 1)
    def _():
        cout_ref[...] = c_ref[...]
        mout_ref[...] = m_ref[...]


def _mlstm_call(q, kt, v, grow, c0, m0, n_batch, seq_len, emit_h):
    L = MLSTM_CHUNK
    n_sub = min(MLSTM_STEP_CHUNKS, seq_len // L)
    nc = seq_len // (L * n_sub)
    fwd = lambda c: c
    bwd = lambda c: nc - 1 - c
    row = lambda w, f: pl.BlockSpec((n_batch, n_sub * L, w), lambda c: (0, f(c), 0))
    col = lambda hgt, f: pl.BlockSpec((n_batch, n_sub, hgt, L), lambda c: (0, f(c), 0, 0))
    n_str = n_batch * N_STREAMS
    state_specs = [pl.BlockSpec((n_str, M_DH, M_AUG), lambda c: (0, 0, 0)),
                   pl.BlockSpec((n_str, LANES), lambda c: (0, 0))]
    state_shapes = [jax.ShapeDtypeStruct((n_str, M_DH, M_AUG), F32),
                    jax.ShapeDtypeStruct((n_str, LANES), F32)]
    if emit_h:
        in_specs = [row(M_WIDTH, fwd), col(M_WIDTH, fwd), row(M_WIDTH, fwd), col(GATE_ROWS, fwd),
                    row(M_WIDTH, bwd), col(M_WIDTH, bwd), row(M_WIDTH, bwd), col(GATE_ROWS, bwd)]
        args = [q, kt, v, grow, q, kt, v, grow]
        out_specs = [row(M_WIDTH, fwd), row(M_WIDTH, bwd)] + state_specs
        out_shape = [jax.ShapeDtypeStruct((n_batch, seq_len, M_WIDTH), BF16)] * 2 + state_shapes
    else:
        in_specs = [col(M_WIDTH, fwd), row(M_WIDTH, fwd), col(GATE_ROWS, fwd),
                    col(M_WIDTH, bwd), row(M_WIDTH, bwd), col(GATE_ROWS, bwd)]
        args = [kt, v, grow, kt, v, grow]
        out_specs = state_specs
        out_shape = state_shapes
    return pl.pallas_call(
        functools.partial(_mlstm_kernel, emit_h=emit_h),
        grid=(nc,),
        in_specs=in_specs + state_specs,
        out_specs=out_specs,
        out_shape=out_shape,
        scratch_shapes=[pltpu.VMEM((n_str, M_DH, M_AUG), F32), pltpu.VMEM((n_str, LANES), F32)],
        compiler_params=_params(1),
        name="mlstm" if emit_h else "mlstm_ctx",
    )(*args, c0, m0)


def _attn_kernel(sink_ref, q_ref, ktp_ref, ktm_ref, ktn_ref, vp_ref, vm_ref, vn_ref, ktx_ref, vx_ref, o_ref,
                 *, n_ctx):
    j = pl.program_id(1)
    nj = pl.num_programs(1)
    T = A_BLOCK
    NQ = ATT_Q_BLOCKS
    n_loc = 3 * T
    n_key = n_loc + n_ctx
    half = LANES // 2
    i_idx = lax.broadcasted_iota(jnp.int32, (T, T), 0)
    r_idx = lax.broadcasted_iota(jnp.int32, (T, T), 1)
    lane_v = lax.broadcasted_iota(jnp.int32, (1, LANES), 1)
    low = lane_v < half
    zk = jnp.zeros((A_DH, n_key), BF16)
    ones_top = jnp.broadcast_to(jnp.where(low, 1.0, 0.0).astype(BF16), (n_key, LANES))
    ones_bot = jnp.broadcast_to(jnp.where(low, 0.0, 1.0).astype(BF16), (n_key, LANES))
    for t in range(NQ):
        rows = slice(t * T, (t + 1) * T)
        prev_ok = (r_idx >= i_idx) & (j > 0) if t == 0 else r_idx >= i_idx
        next_ok = (r_idx <= i_idx) & (j < nj - 1) if t == NQ - 1 else r_idx <= i_idx
        for k in range(A_KV_HEADS):
            ks = slice(k * A_DH, (k + 1) * A_DH)
            kt_prev = ktp_ref[0, ks, :] if t == 0 else ktm_ref[t - 1, ks, :]
            kt_next = ktn_ref[0, ks, :] if t == NQ - 1 else ktm_ref[t + 1, ks, :]
            v_prev = vp_ref[...] if t == 0 else vm_ref[(t - 1) * T:t * T, :]
            v_next = vn_ref[...] if t == NQ - 1 else vm_ref[(t + 1) * T:(t + 2) * T, :]
            v_rows = (v_prev, vm_ref[rows, :], v_next, vx_ref[...])
            kt_all = jnp.concatenate([kt_prev, ktm_ref[t, ks, :], kt_next]
                                     + [ktx_ref[cc, ks, :] for cc in range(n_ctx // T)], axis=1)
            rhs = jnp.concatenate([jnp.concatenate([kt_all, zk], axis=1),
                                   jnp.concatenate([zk, kt_all], axis=1)], axis=0)
            top_sl = slice(0, LANES) if k == 0 else slice(LANES, 2 * LANES)
            bot_sl = slice(LANES, 2 * LANES) if k == 0 else slice(0, LANES)
            v_top = jnp.concatenate([jnp.where(low, vr[:, top_sl], 0) for vr in v_rows], axis=0)
            v_bot = jnp.concatenate([jnp.where(low, 0, vr[:, bot_sl]) for vr in v_rows], axis=0)
            v2 = jnp.concatenate([jnp.concatenate([v_top.astype(BF16), ones_top], axis=1),
                                  jnp.concatenate([v_bot.astype(BF16), ones_bot], axis=1)], axis=0)
            for p in range(A_HEADS // A_KV_HEADS // 2):
                g = k * (A_HEADS // A_KV_HEADS // 2) + p
                q2 = q_ref[rows, g * LANES:(g + 1) * LANES]
                s2 = jnp.dot(q2, rhs, preferred_element_type=F32)
                es = []
                sink_terms = []
                for u in range(2):
                    base = u * n_key
                    sink = sink_ref[2 * g + u]
                    s_prev = jnp.where(prev_ok, s2[:, base:base + T], -jnp.inf)
                    s_cur = s2[:, base + T:base + 2 * T]
                    s_next = jnp.where(next_ok, s2[:, base + 2 * T:base + 3 * T], -jnp.inf)
                    s_ctx = s2[:, base + n_loc:base + n_key]
                    m = jnp.maximum(jnp.maximum(s_prev, s_cur), s_next)
                    for cc in range(n_ctx // T):
                        m = jnp.maximum(m, s_ctx[:, cc * T:(cc + 1) * T])
                    m = jnp.maximum(jnp.max(m, axis=1, keepdims=True), sink)
                    es += [jnp.exp(s_prev - m), jnp.exp(s_cur - m), jnp.exp(s_next - m), jnp.exp(s_ctx - m)]
                    sink_terms.append(jnp.exp(sink - m))
                p2 = jnp.concatenate(es, axis=1).astype(BF16)
                o = jnp.dot(p2, v2, preferred_element_type=F32)
                den = o[:, LANES:] + jnp.where(low, sink_terms[0], sink_terms[1])
                o_ref[rows, g * LANES:(g + 1) * LANES] = (o[:, :LANES] / den).astype(BF16)


def _attn_call(sink, qa, kat, va, kat_c, va_c, n_batch, seq_len, n_ctx):
    T = A_BLOCK
    NQ = ATT_Q_BLOCKS
    nb = seq_len // T
    nj = nb // NQ
    r = n_batch * seq_len
    prv = lambda b, j: b * nb + jnp.maximum(j * NQ - 1, 0)
    nxt = lambda b, j: b * nb + jnp.minimum((j + 1) * NQ, nb - 1)
    kedge = lambda f: pl.BlockSpec((1, A_KV_WIDTH, T), lambda b, j: (f(b, j), 0, 0))
    vedge = lambda f: pl.BlockSpec((T, 2 * A_KV_WIDTH), lambda b, j: (f(b, j), 0))
    return pl.pallas_call(
        functools.partial(_attn_kernel, n_ctx=n_ctx),
        grid=(n_batch, nj),
        in_specs=[pl.BlockSpec(memory_space=pltpu.SMEM),
                  pl.BlockSpec((NQ * T, A_WIDTH), lambda b, j: (b * nj + j, 0)),
                  kedge(prv), pl.BlockSpec((NQ, A_KV_WIDTH, T), lambda b, j: (b * nj + j, 0, 0)), kedge(nxt),
                  vedge(prv), pl.BlockSpec((NQ * T, 2 * A_KV_WIDTH), lambda b, j: (b * nj + j, 0)), vedge(nxt),
                  pl.BlockSpec((n_ctx // T, A_KV_WIDTH, T), lambda b, j: (b, 0, 0)),
                  pl.BlockSpec((n_ctx, 2 * A_KV_WIDTH), lambda b, j: (b, 0))],
        out_specs=pl.BlockSpec((NQ * T, A_WIDTH), lambda b, j: (b * nj + j, 0)),
        out_shape=jax.ShapeDtypeStruct((r, A_WIDTH), BF16),
        compiler_params=_params(2),
        name="attn",
    )(sink, qa, kat, kat, kat, va, va, va, kat_c, va_c)


def _mix_ffn_kernel(hf_ref, hb_ref, op_ref, a_ref, x_ref, mod_ref, mg_ref, wo_ref, g1_ref, b1_ref,
                    w13_hbm, w2_hbm, g2_ref, b2_ref, o_ref,
                    w13_ref, w2_ref, st13_ref, st2_ref, sem, h_ref, acc_ref, *, layer, which):
    @pl.when(pl.program_id(0) == 0)
    def _():
        stage = _weight_stager(w13_hbm, w2_hbm, w13_ref, w2_ref, st13_ref, st2_ref, sem, layer, which)
        for j in range(D_FF // FF_CHUNK):
            stage(j)

    rows_per = hf_ref.shape[0] // FFN_PARTS
    x2 = []
    for p in range(FFN_PARTS):
        rs = slice(p * rows_per, (p + 1) * rows_per)
        h = hf_ref[rs, :].astype(F32) + hb_ref[rs, :].astype(F32)
        gate = _sigmoid(op_ref[rs, :].astype(F32))
        parts = []
        for hd in range(M_HEADS):
            sl = slice(hd * M_DH, (hd + 1) * M_DH)
            seg = h[:, sl]
            mu = jnp.mean(seg, axis=-1, keepdims=True)
            sc = seg - mu
            var = jnp.mean(sc * sc, axis=-1, keepdims=True)
            parts.append((sc * lax.rsqrt(var + LN_EPS) * mg_ref[:, sl] * gate[:, sl]).astype(BF16))
        mixed = jnp.concatenate(parts + [a_ref[rs, :]], axis=1)
        y = jnp.dot(mixed, wo_ref[...], preferred_element_type=F32)
        x2.append(_layer_norm(ALPHA * x_ref[rs, :] + mod_ref[0, 5:6, :] * y, g1_ref[...], b1_ref[...]))
    for p in range(FFN_PARTS):
        rs = slice(p * rows_per, (p + 1) * rows_per)
        o_ref[rs, :] = _ffn_body(x2[p], mod_ref, 6, w13_ref, w2_ref, g2_ref, b2_ref, h_ref.at[p], acc_ref.at[p])


def _mix_ffn_call(hf, hb, o_pre, a_l, x1, mod, mod_index, mh_g, w_out, g1, b1, ffn_w13, ffn_w2, layer, which,
                  g2, b2, tm):
    r = x1.shape[0]
    row = lambda w: pl.BlockSpec((tm, w), lambda i: (i, 0))
    vec = lambda a: a.reshape(1, a.shape[-1])
    return pl.pallas_call(
        functools.partial(_mix_ffn_kernel, layer=layer, which=which),
        grid=(r // tm,),
        in_specs=[row(M_WIDTH), row(M_WIDTH), row(M_WIDTH), row(A_WIDTH), row(D_MODEL),
                  pl.BlockSpec((1, N_MOD, D_MODEL), lambda i: (mod_index(i), 0, 0)),
                  _resident((1, M_WIDTH)),
                  _resident((D_MODEL, D_MODEL)),
                  _resident((1, D_MODEL)),
                  _resident((1, D_MODEL)),
                  pl.BlockSpec(memory_space=pl.ANY),
                  pl.BlockSpec(memory_space=pl.ANY),
                  _resident((1, D_MODEL)),
                  _resident((1, D_MODEL))],
        out_specs=row(D_MODEL),
        out_shape=jax.ShapeDtypeStruct((r, D_MODEL), F32),
        scratch_shapes=_ffn_scratch((FFN_PARTS, tm // FFN_PARTS)),
        compiler_params=_params(1),
        name="mix_ffn",
    )(hf, hb, o_pre, a_l, x1, mod, vec(mh_g), w_out, vec(g1), vec(b1), ffn_w13, ffn_w2, vec(g2), vec(b2))


def _rope_tables(seq_len):
    t = np.arange(seq_len)
    row = (t // GRID_W).astype(np.float64)
    col = (t % GRID_W).astype(np.float64)
    n_freq = A_DH // 4
    inv = ROPE_BASE ** (-np.arange(n_freq, dtype=np.float64) / n_freq)
    ang = np.concatenate([row[:, None] * inv, col[:, None] * inv], -1)
    cos, sin = np.cos(ang), np.sin(ang)
    return (jnp.asarray(np.tile(cos, (1, 4)), F32),
            jnp.asarray(np.tile(np.concatenate([-sin, sin], -1), (1, 2)), F32))


def _spread_gates(a):
    lead = a.shape[:-1]
    a = a.reshape(lead + (4, M_HEADS))
    a = jnp.pad(a, [(0, 0)] * len(lead) + [(0, 0), (0, GATE_GROUP - M_HEADS)])
    a = a.reshape(lead + (4 * GATE_GROUP,))
    return jnp.pad(a, [(0, 0)] * len(lead) + [(0, LANES - 4 * GATE_GROUP)])


def kernel(x, c, ctx, c_ctx, w_ada, b_ada, ln_g, ln_b, ffn_w13, ffn_w2, w_in, b_gates, conv_w, conv_b,
           mh_norm_g, attn_sink, w_out):
    n_batch, seq_len, d = x.shape
    n_ctx = ctx.shape[1]
    l = 0
    tm = 512
    tm_ctx = n_ctx
    tm_ffn = FFN_PARTS * tm
    tiles_per_batch = seq_len // tm

    cc = jnp.concatenate([c, c_ctx[None, :], jnp.zeros((SUBLANES - n_batch - 1, d), F32)], axis=0)
    mod = _mod_call(cc, w_ada[l], b_ada[l])[:n_batch + 1].reshape(n_batch + 1, N_MOD, d)
    lat_mod = lambda i: i // tiles_per_batch
    ctx_mod = lambda i: n_batch

    assert OFF_QM == 0 and OFF_G == N_WA
    wi = w_in[l].astype(BF16)
    w_va = wi[:, OFF_VA:N_IN]
    wt = jnp.concatenate([wi[:, OFF_QA:OFF_VA], w_va, w_va[:, A_DH:], w_va[:, :A_DH],
                          _spread_gates(wi[:, OFF_G:OFF_QA])], axis=1)
    bg = _spread_gates(b_gates[l]).reshape(1, LANES)
    cb = conv_b[l].reshape(1, 2 * M_WIDTH)
    cos, sin = _rope_tables(seq_len)

    x_lat = x.reshape(n_batch * seq_len, d)
    x_ctx = ctx.reshape(n_batch * n_ctx, d)

    x1, xc1 = _ffn_call(x_lat, x_ctx, mod, seq_len // tm_ffn, ffn_w13, ffn_w2, l, 0, ln_g[l, 0], ln_b[l, 0],
                        tm_ffn)

    (_, kt_c, v_c, _, grow_c, _, kat_c, va_c) = _inproj_call(
        xc1, mod, ctx_mod, wi, wt, bg, conv_w[l], cb, None, None, tm_ctx, n_ctx)
    (q_l, kt_l, v_l, o_pre, grow_l, qa_l, kat_l, va_l) = _inproj_call(
        x1, mod, lat_mod, wi, wt, bg, conv_w[l], cb, cos, sin, tm, seq_len)

    c0 = jnp.zeros((n_batch * N_STREAMS, M_DH, M_AUG), F32)
    m0 = jnp.zeros((n_batch * N_STREAMS, LANES), F32)
    c_st, m_st = _mlstm_call(None, kt_c, v_c.reshape(n_batch, n_ctx, M_WIDTH), grow_c, c0, m0,
                             n_batch, n_ctx, False)
    hf, hb, _, _ = _mlstm_call(q_l.reshape(n_batch, seq_len, M_WIDTH), kt_l,
                               v_l.reshape(n_batch, seq_len, M_WIDTH), grow_l, c_st, m_st, n_batch, seq_len, True)
    hf = hf.reshape(n_batch * seq_len, M_WIDTH)
    hb = hb.reshape(n_batch * seq_len, M_WIDTH)

    a_l = _attn_call(attn_sink[l], qa_l, kat_l, va_l, kat_c, va_c, n_batch, seq_len, n_ctx)

    out = _mix_ffn_call(hf, hb, o_pre, a_l, x1, mod, lambda i: i // (seq_len // tm_ffn), mh_norm_g[l],
                        w_out[l].astype(BF16), ln_g[l, 1], ln_b[l, 1], ffn_w13, ffn_w2, l, 1,
                        ln_g[l, 2], ln_b[l, 2], tm_ffn)
    return out.reshape(n_batch, seq_len, d)
```

```python
import functools

import jax
import jax.numpy as jnp
import numpy as np
from jax import lax
from jax.experimental import pallas as pl
from jax.experimental.pallas import tpu as pltpu

F32 = jnp.float32
BF16 = jnp.bfloat16

D_MODEL = 1024
GRID_W = 64
M_HEADS = 4
M_DH = 128
M_WIDTH = M_HEADS * M_DH
A_HEADS = 8
A_KV_HEADS = 2
A_DH = 64
A_WIDTH = A_HEADS * A_DH
A_KV_WIDTH = A_KV_HEADS * A_DH
A_BLOCK = 128
CONV_W = 5
D_FF = 2816
ROPE_BASE = 10000.0
N_MOD = 9
LN_EPS = 1e-5
FFN_RES = 0.5
DEPTH = 1
ALPHA = (2.0 * DEPTH) ** 0.25

OFF_QM = 0
OFF_KM = OFF_QM + M_WIDTH
OFF_VM = OFF_KM + M_WIDTH
OFF_OM = OFF_VM + M_WIDTH
OFF_G = OFF_OM + M_WIDTH
N_GATES = 4 * M_HEADS
OFF_QA = OFF_G + N_GATES
OFF_KA = OFF_QA + A_WIDTH
OFF_VA = OFF_KA + A_KV_WIDTH
N_IN = OFF_VA + A_KV_WIDTH

LANES = 128
SUBLANES = 8
MXU_TILE = 256
VMEM_BYTES = 64 * 1024 * 1024
VMEM_LIMIT = VMEM_BYTES * 7 // 8
HALO = SUBLANES
MLSTM_CHUNK = LANES
MLSTM_STEP_CHUNKS = 4
FF_CHUNK = MXU_TILE
COL_BLOCK = MXU_TILE
ROW_PART = LANES
FFN_PARTS = 2
ATT_Q_BLOCKS = 16
MOD_COLS = 2304
MOD_STREAMS = 2
GATE_GROUP = SUBLANES


def _sigmoid(x):
    return 1.0 / (1.0 + jnp.exp(-x))


def _log_sigmoid(x):
    return jnp.minimum(x, 0.0) - jnp.log1p(jnp.exp(-jnp.abs(x)))


def _layer_norm(z, g, b):
    mu = jnp.mean(z, axis=-1, keepdims=True)
    zc = z - mu
    var = jnp.mean(zc * zc, axis=-1, keepdims=True)
    return zc * lax.rsqrt(var + LN_EPS) * g + b


def _params(n_grid):
    return pltpu.CompilerParams(dimension_semantics=("arbitrary",) * n_grid, vmem_limit_bytes=VMEM_LIMIT)


def _resident(shape, index=None):
    index = index or (0,) * len(shape)
    return pl.BlockSpec(shape, lambda *_: index, pipeline_mode=pl.Buffered(1))


def _mod_kernel(c_ref, *refs):
    *w_refs, b_ref, o_ref = refs
    c = c_ref[...]
    s = (c * _sigmoid(c)).astype(BF16)
    tn = w_refs[0].shape[1]
    for k, w_ref in enumerate(w_refs):
        cols = slice(k * tn, (k + 1) * tn)
        o_ref[:, cols] = jnp.dot(s, w_ref[...].astype(BF16), preferred_element_type=F32) + b_ref[:, cols]


def _mod_call(cc, w_ada, b_ada):
    n = w_ada.shape[1]
    tn = MOD_COLS
    w_specs = [pl.BlockSpec((D_MODEL, tn // MOD_STREAMS), lambda j, k=k: (0, MOD_STREAMS * j + k))
               for k in range(MOD_STREAMS)]
    return pl.pallas_call(
        _mod_kernel,
        grid=(n // tn,),
        in_specs=[pl.BlockSpec((SUBLANES, D_MODEL), lambda j: (0, 0))] + w_specs
                 + [pl.BlockSpec((1, tn), lambda j: (0, j))],
        out_specs=pl.BlockSpec((SUBLANES, tn), lambda j: (0, j)),
        out_shape=jax.ShapeDtypeStruct((SUBLANES, n), F32),
        compiler_params=_params(1),
        name="mod",
    )(cc, *([w_ada] * MOD_STREAMS), b_ada.reshape(1, n))


def _ffn_body(x, mod_ref, k0, w13_ref, w2_ref, g_ref, b_ref, h_ref, acc_ref, before_chunk=None):
    shift = mod_ref[0, k0:k0 + 1, :]
    scale = mod_ref[0, k0 + 1:k0 + 2, :]
    gate = mod_ref[0, k0 + 2:k0 + 3, :]
    h_ref[...] = (x * (1.0 + scale) + shift).astype(BF16)
    for j in range(D_FF // FF_CHUNK):
        lo = j * FF_CHUNK
        if before_chunk is not None:
            before_chunk(j)
        h = h_ref[...]
        a1 = jnp.dot(h, w13_ref[:, lo:lo + FF_CHUNK], preferred_element_type=F32)
        a3 = jnp.dot(h, w13_ref[:, D_FF + lo:D_FF + lo + FF_CHUNK], preferred_element_type=F32)
        g = (a1 * _sigmoid(a1) * a3).astype(BF16)
        y = jnp.dot(g, w2_ref[lo:lo + FF_CHUNK, :], preferred_element_type=F32)
        if j == 0:
            acc_ref[...] = y
        else:
            acc_ref[...] += y
    z = ALPHA * x + (FFN_RES * gate) * acc_ref[...]
    return _layer_norm(z, g_ref[...], b_ref[...])


def _weight_stager(w13_hbm, w2_hbm, w13_ref, w2_ref, st13_ref, st2_ref, sem, layer, which):
    n_chunks = D_FF // FF_CHUNK

    def copies(j):
        lo, slot = j * FF_CHUNK, j % 2
        return (pltpu.make_async_copy(w13_hbm.at[layer, which, :, lo:lo + FF_CHUNK],
                                      st13_ref.at[slot, 0], sem.at[slot, 0]),
                pltpu.make_async_copy(w13_hbm.at[layer, which, :, D_FF + lo:D_FF + lo + FF_CHUNK],
                                      st13_ref.at[slot, 1], sem.at[slot, 1]),
                pltpu.make_async_copy(w2_hbm.at[layer, which, lo:lo + FF_CHUNK, :],
                                      st2_ref.at[slot], sem.at[slot, 2]))

    def before_chunk(j):
        if j == 0:
            for cp in copies(0):
                cp.start()
        if j + 1 < n_chunks:
            for cp in copies(j + 1):
                cp.start()
        for cp in copies(j):
            cp.wait()
        lo, slot = j * FF_CHUNK, j % 2
        w13_ref[:, lo:lo + FF_CHUNK] = st13_ref[slot, 0].astype(BF16)
        w13_ref[:, D_FF + lo:D_FF + lo + FF_CHUNK] = st13_ref[slot, 1].astype(BF16)
        w2_ref[lo:lo + FF_CHUNK, :] = st2_ref[slot].astype(BF16)

    return before_chunk


def _ffn_kernel(x_ref, xc_ref, mod_ref, w13_hbm, w2_hbm, g_ref, b_ref, o_ref, oc_ref,
                w13_ref, w2_ref, st13_ref, st2_ref, sem, h_ref, acc_ref, *, layer, which):
    i = pl.program_id(0)
    ffn = functools.partial(_ffn_body, mod_ref=mod_ref, k0=0, w13_ref=w13_ref, w2_ref=w2_ref, g_ref=g_ref,
                            b_ref=b_ref)

    @pl.when(i == 0)
    def _():
        stage = _weight_stager(w13_hbm, w2_hbm, w13_ref, w2_ref, st13_ref, st2_ref, sem, layer, which)
        oc_ref[...] = ffn(xc_ref[...], h_ref=h_ref.at[0], acc_ref=acc_ref.at[0], before_chunk=stage)

    @pl.when(i > 0)
    def _():
        rows = o_ref.shape[0] // FFN_PARTS
        for p in range(FFN_PARTS):
            part = slice(p * rows, (p + 1) * rows)
            o_ref[part, :] = ffn(x_ref[part, :], h_ref=h_ref.at[p], acc_ref=acc_ref.at[p])


def _ffn_scratch(rows):
    rows = rows if isinstance(rows, tuple) else (rows,)
    return [pltpu.VMEM((D_MODEL, 2 * D_FF), BF16), pltpu.VMEM((D_FF, D_MODEL), BF16),
            pltpu.VMEM((2, 2, D_MODEL, FF_CHUNK), F32), pltpu.VMEM((2, FF_CHUNK, D_MODEL), F32),
            pltpu.SemaphoreType.DMA((2, 3)),
            pltpu.VMEM(rows + (D_MODEL,), BF16), pltpu.VMEM(rows + (D_MODEL,), F32)]


def _ffn_call(x, x_ctx, mod, tiles_per_batch, ffn_w13, ffn_w2, layer, which, ln_g, ln_b, tm):
    r = x.shape[0]
    n_lat = r // tm
    rows = tm // FFN_PARTS
    assert x_ctx.shape[0] == rows
    n_batch = mod.shape[0] - 1
    lat = lambda i: jnp.maximum(i - 1, 0)
    return pl.pallas_call(
        functools.partial(_ffn_kernel, layer=layer, which=which),
        grid=(n_lat + 1,),
        in_specs=[pl.BlockSpec((tm, D_MODEL), lambda i: (lat(i), 0)),
                  _resident((rows, D_MODEL)),
                  pl.BlockSpec((1, N_MOD, D_MODEL),
                               lambda i: (jnp.where(i == 0, n_batch, lat(i) // tiles_per_batch), 0, 0)),
                  pl.BlockSpec(memory_space=pl.ANY),
                  pl.BlockSpec(memory_space=pl.ANY),
                  _resident((1, D_MODEL)),
                  _resident((1, D_MODEL))],
        out_specs=[pl.BlockSpec((tm, D_MODEL), lambda i: (lat(i), 0)),
                   pl.BlockSpec((rows, D_MODEL), lambda i: (0, 0))],
        out_shape=[jax.ShapeDtypeStruct((r, D_MODEL), F32), jax.ShapeDtypeStruct((rows, D_MODEL), F32)],
        scratch_shapes=_ffn_scratch((FFN_PARTS, rows)),
        compiler_params=_params(1),
        name="ffn",
    )(x, x_ctx, mod, ffn_w13, ffn_w2, ln_g.reshape(1, D_MODEL), ln_b.reshape(1, D_MODEL))


N_WA = 2 * M_WIDTH + 2 * M_WIDTH
OFF_TG = A_WIDTH + A_KV_WIDTH + 2 * A_KV_WIDTH
N_WT = OFF_TG + LANES
ROW_A = (0, 3 * GATE_GROUP)
ROW_B = (GATE_GROUP, 4 * GATE_GROUP)
ROW_C = (2 * GATE_GROUP, 5 * GATE_GROUP)
GATE_ROWS = 6 * GATE_GROUP


def _scan(x, lane, op, reverse):
    fill = 0.0 if op is jnp.add else -jnp.inf
    k = 1
    while k < LANES:
        if reverse:
            other = jnp.where(lane < LANES - k, pltpu.roll(x, LANES - k, 1), fill)
        else:
            other = jnp.where(lane >= k, pltpu.roll(x, k, 1), fill)
        x = op(x, other)
        k *= 2
    return x


def _chunk_scan(x, lane, op, reverse):
    return jnp.concatenate([_scan(x[:, c * LANES:(c + 1) * LANES], lane, op, reverse)
                            for c in range(x.shape[1] // LANES)], axis=1)


def _rope(x, cos, sin_signed, first_half):
    swapped = jnp.where(first_half, pltpu.roll(x, LANES - A_DH // 2, 1), pltpu.roll(x, A_DH // 2, 1))
    return x * cos + swapped * sin_signed


def _inproj_kernel(*refs, tiles_per_seq, rope, tm):
    n_qk = 2 * M_WIDTH // COL_BLOCK
    raw_refs = refs[-n_qk:]
    refs = refs[:-n_qk]
    if rope:
        (xp_ref, x_ref, xn_ref, mod_ref, wa_ref, wt_ref, bg_ref, cw_ref, cb_ref, cos_ref, sin_ref,
         q_ref, kt_ref, v_ref, o_ref, grow_ref, qa_ref, kat_ref, va_ref, h_ref, pt_ref) = refs
    else:
        (xp_ref, x_ref, xn_ref, mod_ref, wa_ref, wt_ref, bg_ref, cw_ref, cb_ref,
         q_ref, kt_ref, v_ref, o_ref, grow_ref, qa_ref, kat_ref, va_ref, h_ref, pt_ref) = refs
    pos = pl.program_id(0) % tiles_per_seq
    slot0 = jnp.minimum(pl.program_id(0), 0)
    n_ext = tm + 2 * HALO
    shift = mod_ref[0, 3:4, :]
    scale1 = 1.0 + mod_ref[0, 4:5, :]
    h_ref[0:tm, :] = (x_ref[...] * scale1 + shift).astype(BF16)
    xh = jnp.concatenate([xp_ref[...], xn_ref[...]], axis=0)
    h_ref[tm:n_ext, :] = (xh * scale1 + shift).astype(BF16)

    def mm(w_ref, lo):
        return jnp.dot(h_ref[0:tm, :], w_ref[:, lo:lo + COL_BLOCK], preferred_element_type=F32)

    def qk_matmul(blk):
        cols = slice(blk * COL_BLOCK, (blk + 1) * COL_BLOCK)
        y = jnp.dot(h_ref[...], wa_ref[:, cols], preferred_element_type=F32)
        raw_refs[blk][0, 0:HALO, :] = jnp.where(pos == 0, 0.0, y[tm:tm + HALO])
        raw_refs[blk][0, HALO:HALO + tm, :] = y[0:tm]
        raw_refs[blk][0, HALO + tm:n_ext, :] = jnp.where(pos == tiles_per_seq - 1, 0.0, y[tm + HALO:n_ext])

    def qk_epilogue(blk, part):
        cols = slice(blk * COL_BLOCK, (blk + 1) * COL_BLOCK)
        r0 = part * ROW_PART
        base = HALO - CONV_W // 2 + r0
        acc = cb_ref[:, cols] + cw_ref[0:1, cols] * raw_refs[blk][slot0, pl.ds(base, ROW_PART), :]
        for j in range(1, CONV_W):
            acc = acc + cw_ref[j:j + 1, cols] * raw_refs[blk][slot0, pl.ds(base + j, ROW_PART), :]
        qk = acc * _sigmoid(acc)
        if blk < M_WIDTH // COL_BLOCK:
            q_ref[r0:r0 + ROW_PART, cols] = (qk * (M_DH ** -0.5)).astype(BF16)
        else:
            kt_ref[0, part, blk * COL_BLOCK - M_WIDTH:(blk + 1) * COL_BLOCK - M_WIDTH, :] = qk.T.astype(BF16)

    def v_matmul(blk):
        v_ref[:, blk * COL_BLOCK:(blk + 1) * COL_BLOCK] = mm(wa_ref, 2 * M_WIDTH + blk * COL_BLOCK).astype(BF16)

    def o_matmul(blk):
        o_ref[:, blk * COL_BLOCK:(blk + 1) * COL_BLOCK] = mm(wa_ref, 3 * M_WIDTH + blk * COL_BLOCK).astype(BF16)

    def t_matmul(blk):
        pt_ref[:, blk * COL_BLOCK:(blk + 1) * COL_BLOCK] = mm(wt_ref, blk * COL_BLOCK)

    def pt_cols(lo, width):
        return pt_ref[:, lo:lo + width]

    def gate_epilogue():
        pgt = (pt_cols(OFF_TG, LANES) + bg_ref[...]).T
        li_f = pgt[0:GATE_GROUP]
        lf_f = _log_sigmoid(pgt[GATE_GROUP:2 * GATE_GROUP])
        li_b = pgt[2 * GATE_GROUP:3 * GATE_GROUP]
        lf_b = _log_sigmoid(pgt[3 * GATE_GROUP:4 * GATE_GROUP])
        lane = lax.broadcasted_iota(jnp.int32, (GATE_GROUP, LANES), 1)
        b_f = _chunk_scan(lf_f, lane, jnp.add, False)
        e_b = _chunk_scan(lf_b, lane, jnp.add, True)
        a_f = li_f - b_f
        a_b = li_b - e_b
        rows = jnp.concatenate([a_f, b_f, _chunk_scan(a_f, lane, jnp.maximum, False),
                                a_b, e_b, _chunk_scan(a_b, lane, jnp.maximum, True)], axis=0)
        for c in range(tm // MLSTM_CHUNK):
            grow_ref[0, c] = rows[:, c * MLSTM_CHUNK:(c + 1) * MLSTM_CHUNK]
        va_ref[:, A_KV_WIDTH:] = pt_cols(A_WIDTH + 2 * A_KV_WIDTH, A_KV_WIDTH).astype(BF16)

    def rotate(lo):
        xg = pt_cols(lo, LANES)
        if not rope:
            return xg
        lane_t = lax.broadcasted_iota(jnp.int32, (tm, LANES), 1)
        return _rope(xg, cos_ref[...], sin_ref[...], (lane_t % A_DH) < (A_DH // 2))

    def qa_epilogue(blk):
        for g in range(blk * COL_BLOCK // LANES, (blk + 1) * COL_BLOCK // LANES):
            qa_ref[:, g * LANES:(g + 1) * LANES] = (rotate(g * LANES) * (A_DH ** -0.5)).astype(BF16)

    def kv_epilogue():
        kat = rotate(A_WIDTH).T.astype(BF16)
        for c in range(tm // A_BLOCK):
            kat_ref[c] = kat[:, c * A_BLOCK:(c + 1) * A_BLOCK]
        va_ref[:, :A_KV_WIDTH] = pt_cols(A_WIDTH + A_KV_WIDTH, A_KV_WIDTH).astype(BF16)

    for blk in range(n_qk):
        qk_matmul(blk)
    for blk in range(N_WT // COL_BLOCK):
        t_matmul(blk)
    for blk in range(M_WIDTH // COL_BLOCK):
        v_matmul(blk)
        o_matmul(blk)
    for blk in range(n_qk):
        for part in range(tm // ROW_PART):
            qk_epilogue(blk, part)
    gate_epilogue()
    qa_epilogue(0)
    qa_epilogue(1)
    kv_epilogue()


def _inproj_call(x, mod, mod_index, wa, wt, bg, conv_w, conv_b, cos, sin, tm, seq_len):
    r = x.shape[0]
    tiles_per_seq = seq_len // tm
    rope = cos is not None
    hb = tm // HALO
    n_halo = r // HALO
    in_specs = [pl.BlockSpec((HALO, D_MODEL), lambda i: (jnp.maximum(i * hb - 1, 0), 0)),
                pl.BlockSpec((tm, D_MODEL), lambda i: (i, 0)),
                pl.BlockSpec((HALO, D_MODEL), lambda i: (jnp.minimum((i + 1) * hb, n_halo - 1), 0)),
                pl.BlockSpec((1, N_MOD, D_MODEL), lambda i: (mod_index(i), 0, 0)),
                _resident((D_MODEL, N_WA)),
                _resident((D_MODEL, N_WT)),
                _resident((1, LANES)),
                _resident((CONV_W, 2 * M_WIDTH)),
                _resident((1, 2 * M_WIDTH))]
    args = [x, x, x, mod, wa, wt, bg, conv_w, conv_b]
    if rope:
        in_specs += [pl.BlockSpec((tm, LANES), lambda i: (i % tiles_per_seq, 0)),
                     pl.BlockSpec((tm, LANES), lambda i: (i % tiles_per_seq, 0))]
        args += [cos, sin]
    assert MLSTM_CHUNK == A_BLOCK == ROW_PART
    n_seq = r // seq_len
    cps = seq_len // MLSTM_CHUNK
    cpt = tm // MLSTM_CHUNK
    row = lambda w: pl.BlockSpec((tm, w), lambda i: (i, 0))
    seq_col = lambda h: pl.BlockSpec((1, cpt, h, MLSTM_CHUNK),
                                     lambda i: (i // tiles_per_seq, i % tiles_per_seq, 0, 0))
    out_specs = [row(M_WIDTH), seq_col(M_WIDTH), row(M_WIDTH), row(M_WIDTH), seq_col(GATE_ROWS),
                 row(A_WIDTH), pl.BlockSpec((cpt, A_KV_WIDTH, A_BLOCK), lambda i: (i, 0, 0)),
                 row(2 * A_KV_WIDTH)]
    out_shape = [jax.ShapeDtypeStruct((r, M_WIDTH), BF16),
                 jax.ShapeDtypeStruct((n_seq, cps, M_WIDTH, MLSTM_CHUNK), BF16),
                 jax.ShapeDtypeStruct((r, M_WIDTH), BF16),
                 jax.ShapeDtypeStruct((r, M_WIDTH), BF16),
                 jax.ShapeDtypeStruct((n_seq, cps, GATE_ROWS, MLSTM_CHUNK), F32),
                 jax.ShapeDtypeStruct((r, A_WIDTH), BF16),
                 jax.ShapeDtypeStruct((r // A_BLOCK, A_KV_WIDTH, A_BLOCK), BF16),
                 jax.ShapeDtypeStruct((r, 2 * A_KV_WIDTH), BF16)]
    return pl.pallas_call(
        functools.partial(_inproj_kernel, tiles_per_seq=tiles_per_seq, rope=rope, tm=tm),
        grid=(r // tm,),
        in_specs=in_specs,
        out_specs=out_specs,
        out_shape=out_shape,
        scratch_shapes=([pltpu.VMEM((tm + 2 * HALO, D_MODEL), BF16), pltpu.VMEM((tm, N_WT), F32)]
                        + [pltpu.VMEM((1, tm + 2 * HALO, COL_BLOCK), F32)] * (2 * M_WIDTH // COL_BLOCK)),
        compiler_params=_params(1),
        name="inproj",
    )(*args)


M_AUG = 2 * M_DH
N_STREAMS = 2 * M_HEADS


def _col_replicated(row):
    return jnp.broadcast_to(row, (LANES, row.shape[1])).T


def _mlstm_kernel(*refs, emit_h):
    if emit_h:
        (qf_ref, ktf_ref, vf_ref, grf_ref, qb_ref, ktb_ref, vb_ref, grb_ref, c0_ref, m0_ref,
         hf_ref, hb_ref, cout_ref, mout_ref, c_ref, m_ref) = refs
        q_refs, h_refs = (qf_ref, qb_ref), (hf_ref, hb_ref)
    else:
        (ktf_ref, vf_ref, grf_ref, ktb_ref, vb_ref, grb_ref, c0_ref, m0_ref,
         cout_ref, mout_ref, c_ref, m_ref) = refs
    kt_refs, v_refs, gr_refs = (ktf_ref, ktb_ref), (vf_ref, vb_ref), (grf_ref, grb_ref)
    L = MLSTM_CHUNK
    n_batch = c_ref.shape[0] // N_STREAMS
    n_sub = grf_ref.shape[1]
    c = pl.program_id(0)

    @pl.when(c == 0)
    def _():
        c_ref[...] = c0_ref[...]
        m_ref[...] = m0_ref[...]

    t_idx = lax.broadcasted_iota(jnp.int32, (L, L), 0)
    s_idx = lax.broadcasted_iota(jnp.int32, (L, L), 1)
    masks = (s_idx <= t_idx, s_idx >= t_idx)
    ones = jnp.ones((L, M_DH), BF16)
    zeros_k = jnp.zeros((M_DH, L), BF16)
    m_all = m_ref[...]
    m_cur = [m_all[r:r + 1] for r in range(n_batch * N_STREAMS)]
    for sub in range(n_sub):
        for b in range(n_batch):
            for d in range(2):
                last = (L - 1, 0)[d]
                ch = sub if d == 0 else n_sub - 1 - sub
                tok = slice(ch * L, (ch + 1) * L)
                gr = gr_refs[d][b, ch]
                for pair in range(M_HEADS // 2):
                    if emit_h:
                        rows = slice(2 * pair * M_DH, (2 * pair + 2) * M_DH)
                        kt2 = kt_refs[d][b, ch, rows, :]
                        rhs = jnp.concatenate(
                            [jnp.concatenate([kt2[:M_DH], zeros_k], axis=1),
                             jnp.concatenate([zeros_k, kt2[M_DH:]], axis=1)], axis=0)
                        s2 = jnp.dot(q_refs[d][b, tok, rows], rhs, preferred_element_type=F32)
                    for hh in range(2):
                        h = 2 * pair + hh
                        r = (b * 2 + d) * M_HEADS + h
                        sl = slice(h * M_DH, (h + 1) * M_DH)
                        a_row = gr[ROW_A[d] + h:ROW_A[d] + h + 1]
                        b_row = gr[ROW_B[d] + h:ROW_B[d] + h + 1]
                        m_row = m_cur[r]
                        ct = c_ref[r]
                        kt = kt_refs[d][b, ch, sl, :]
                        v_aug = jnp.concatenate([v_refs[d][b, tok, sl], ones], axis=1)
                        if emit_h:
                            mu_row = jnp.maximum(m_row, gr[ROW_C[d] + h:ROW_C[d] + h + 1])
                            mu = _col_replicated(mu_row)
                            mt = _col_replicated(b_row + mu_row)
                            dmat = jnp.where(masks[d], jnp.exp(a_row - mu), 0.0)
                            p = (s2[:, hh * L:(hh + 1) * L] * dmat).astype(BF16)
                            qs = (q_refs[d][b, tok, sl].astype(F32) * jnp.exp(m_row - mu)).astype(BF16)
                            tot = jnp.dot(jnp.concatenate([p, qs], axis=1),
                                          jnp.concatenate([v_aug, ct.astype(BF16)], axis=0),
                                          preferred_element_type=F32)
                            h_dir = tot[:, :M_DH] / jnp.maximum(jnp.abs(tot[:, M_DH:]), jnp.exp(-mt))
                            h_refs[d][b, tok, sl] = h_dir.astype(BF16)
                        mp = jnp.maximum(m_row, jnp.max(a_row, axis=1, keepdims=True))
                        ktw = (kt.astype(F32) * jnp.exp(a_row - mp)).astype(BF16)
                        decay = jnp.exp(m_row - mp)
                        c_ref[r] = (jnp.concatenate([decay, decay], axis=1) * ct
                                    + jnp.dot(ktw, v_aug, preferred_element_type=F32))
                        m_cur[r] = b_row[:, last:last + 1] + mp
    m_ref[...] = jnp.concatenate(m_cur, axis=0)

    @pl.when(c == pl.num_programs(0) - 1)
    def _():
        cout_ref[...] = c_ref[...]
        mout_ref[...] = m_ref[...]


def _mlstm_call(q, kt, v, grow, c0, m0, n_batch, seq_len, emit_h):
    L = MLSTM_CHUNK
    n_sub = min(MLSTM_STEP_CHUNKS, seq_len // L)
    nc = seq_len // (L * n_sub)
    fwd = lambda c: c
    bwd = lambda c: nc - 1 - c
    row = lambda w, f: pl.BlockSpec((n_batch, n_sub * L, w), lambda c: (0, f(c), 0))
    col = lambda hgt, f: pl.BlockSpec((n_batch, n_sub, hgt, L), lambda c: (0, f(c), 0, 0))
    n_str = n_batch * N_STREAMS
    state_specs = [pl.BlockSpec((n_str, M_DH, M_AUG), lambda c: (0, 0, 0)),
                   pl.BlockSpec((n_str, LANES), lambda c: (0, 0))]
    state_shapes = [jax.ShapeDtypeStruct((n_str, M_DH, M_AUG), F32),
                    jax.ShapeDtypeStruct((n_str, LANES), F32)]
    if emit_h:
        in_specs = [row(M_WIDTH, fwd), col(M_WIDTH, fwd), row(M_WIDTH, fwd), col(GATE_ROWS, fwd),
                    row(M_WIDTH, bwd), col(M_WIDTH, bwd), row(M_WIDTH, bwd), col(GATE_ROWS, bwd)]
        args = [q, kt, v, grow, q, kt, v, grow]
        out_specs = [row(M_WIDTH, fwd), row(M_WIDTH, bwd)] + state_specs
        out_shape = [jax.ShapeDtypeStruct((n_batch, seq_len, M_WIDTH), BF16)] * 2 + state_shapes
    else:
        in_specs = [col(M_WIDTH, fwd), row(M_WIDTH, fwd), col(GATE_ROWS, fwd),
                    col(M_WIDTH, bwd), row(M_WIDTH, bwd), col(GATE_ROWS, bwd)]
        args = [kt, v, grow, kt, v, grow]
        out_specs = state_specs
        out_shape = state_shapes
    return pl.pallas_call(
        functools.partial(_mlstm_kernel, emit_h=emit_h),
        grid=(nc,),
        in_specs=in_specs + state_specs,
        out_specs=out_specs,
        out_shape=out_shape,
        scratch_shapes=[pltpu.VMEM((n_str, M_DH, M_AUG), F32), pltpu.VMEM((n_str, LANES), F32)],
        compiler_params=_params(1),
        name="mlstm" if emit_h else "mlstm_ctx",
    )(*args, c0, m0)


def _attn_kernel(sink_ref, q_ref, ktp_ref, ktm_ref, ktn_ref, vp_ref, vm_ref, vn_ref, ktx_ref, vx_ref, o_ref,
                 *, n_ctx):
    j = pl.program_id(1)
    nj = pl.num_programs(1)
    T = A_BLOCK
    NQ = ATT_Q_BLOCKS
    n_loc = 3 * T
    n_key = n_loc + n_ctx
    half = LANES // 2
    i_idx = lax.broadcasted_iota(jnp.int32, (T, T), 0)
    r_idx = lax.broadcasted_iota(jnp.int32, (T, T), 1)
    lane_v = lax.broadcasted_iota(jnp.int32, (1, LANES), 1)
    low = lane_v < half
    zk = jnp.zeros((A_DH, n_key), BF16)
    ones_top = jnp.broadcast_to(jnp.where(low, 1.0, 0.0).astype(BF16), (n_key, LANES))
    ones_bot = jnp.broadcast_to(jnp.where(low, 0.0, 1.0).astype(BF16), (n_key, LANES))
    for t in range(NQ):
        rows = slice(t * T, (t + 1) * T)
        prev_ok = (r_idx >= i_idx) & (j > 0) if t == 0 else r_idx >= i_idx
        next_ok = (r_idx <= i_idx) & (j < nj - 1) if t == NQ - 1 else r_idx <= i_idx
        for k in range(A_KV_HEADS):
            ks = slice(k * A_DH, (k + 1) * A_DH)
            kt_prev = ktp_ref[0, ks, :] if t == 0 else ktm_ref[t - 1, ks, :]
            kt_next = ktn_ref[0, ks, :] if t == NQ - 1 else ktm_ref[t + 1, ks, :]
            v_prev = vp_ref[...] if t == 0 else vm_ref[(t - 1) * T:t * T, :]
            v_next = vn_ref[...] if t == NQ - 1 else vm_ref[(t + 1) * T:(t + 2) * T, :]
            v_rows = (v_prev, vm_ref[rows, :], v_next, vx_ref[...])
            kt_all = jnp.concatenate([kt_prev, ktm_ref[t, ks, :], kt_next]
                                     + [ktx_ref[cc, ks, :] for cc in range(n_ctx // T)], axis=1)
            rhs = jnp.concatenate([jnp.concatenate([kt_all, zk], axis=1),
                                   jnp.concatenate([zk, kt_all], axis=1)], axis=0)
            top_sl = slice(0, LANES) if k == 0 else slice(LANES, 2 * LANES)
            bot_sl = slice(LANES, 2 * LANES) if k == 0 else slice(0, LANES)
            v_top = jnp.concatenate([jnp.where(low, vr[:, top_sl], 0) for vr in v_rows], axis=0)
            v_bot = jnp.concatenate([jnp.where(low, 0, vr[:, bot_sl]) for vr in v_rows], axis=0)
            v2 = jnp.concatenate([jnp.concatenate([v_top.astype(BF16), ones_top], axis=1),
                                  jnp.concatenate([v_bot.astype(BF16), ones_bot], axis=1)], axis=0)
            for p in range(A_HEADS // A_KV_HEADS // 2):
                g = k * (A_HEADS // A_KV_HEADS // 2) + p
                q2 = q_ref[rows, g * LANES:(g + 1) * LANES]
                s2 = jnp.dot(q2, rhs, preferred_element_type=F32)
                es = []
                sink_terms = []
                for u in range(2):
                    base = u * n_key
                    sink = sink_ref[2 * g + u]
                    s_prev = jnp.where(prev_ok, s2[:, base:base + T], -jnp.inf)
                    s_cur = s2[:, base + T:base + 2 * T]
                    s_next = jnp.where(next_ok, s2[:, base + 2 * T:base + 3 * T], -jnp.inf)
                    s_ctx = s2[:, base + n_loc:base + n_key]
                    m = jnp.maximum(jnp.maximum(s_prev, s_cur), s_next)
                    for cc in range(n_ctx // T):
                        m = jnp.maximum(m, s_ctx[:, cc * T:(cc + 1) * T])
                    m = jnp.maximum(jnp.max(m, axis=1, keepdims=True), sink)
                    es += [jnp.exp(s_prev - m), jnp.exp(s_cur - m), jnp.exp(s_next - m), jnp.exp(s_ctx - m)]
                    sink_terms.append(jnp.exp(sink - m))
                p2 = jnp.concatenate(es, axis=1).astype(BF16)
                o = jnp.dot(p2, v2, preferred_element_type=F32)
                den = o[:, LANES:] + jnp.where(low, sink_terms[0], sink_terms[1])
                o_ref[rows, g * LANES:(g + 1) * LANES] = (o[:, :LANES] / den).astype(BF16)


def _attn_call(sink, qa, kat, va, kat_c, va_c, n_batch, seq_len, n_ctx):
    T = A_BLOCK
    NQ = ATT_Q_BLOCKS
    nb = seq_len // T
    nj = nb // NQ
    r = n_batch * seq_len
    prv = lambda b, j: b * nb + jnp.maximum(j * NQ - 1, 0)
    nxt = lambda b, j: b * nb + jnp.minimum((j + 1) * NQ, nb - 1)
    kedge = lambda f: pl.BlockSpec((1, A_KV_WIDTH, T), lambda b, j: (f(b, j), 0, 0))
    vedge = lambda f: pl.BlockSpec((T, 2 * A_KV_WIDTH), lambda b, j: (f(b, j), 0))
    return pl.pallas_call(
        functools.partial(_attn_kernel, n_ctx=n_ctx),
        grid=(n_batch, nj),
        in_specs=[pl.BlockSpec(memory_space=pltpu.SMEM),
                  pl.BlockSpec((NQ * T, A_WIDTH), lambda b, j: (b * nj + j, 0)),
                  kedge(prv), pl.BlockSpec((NQ, A_KV_WIDTH, T), lambda b, j: (b * nj + j, 0, 0)), kedge(nxt),
                  vedge(prv), pl.BlockSpec((NQ * T, 2 * A_KV_WIDTH), lambda b, j: (b * nj + j, 0)), vedge(nxt),
                  pl.BlockSpec((n_ctx // T, A_KV_WIDTH, T), lambda b, j: (b, 0, 0)),
                  pl.BlockSpec((n_ctx, 2 * A_KV_WIDTH), lambda b, j: (b, 0))],
        out_specs=pl.BlockSpec((NQ * T, A_WIDTH), lambda b, j: (b * nj + j, 0)),
        out_shape=jax.ShapeDtypeStruct((r, A_WIDTH), BF16),
        compiler_params=_params(2),
        name="attn",
    )(sink, qa, kat, kat, kat, va, va, va, kat_c, va_c)


def _mix_ffn_kernel(hf_ref, hb_ref, op_ref, a_ref, x_ref, mod_ref, mg_ref, wo_ref, g1_ref, b1_ref,
                    w13_hbm, w2_hbm, g2_ref, b2_ref, o_ref,
                    w13_ref, w2_ref, st13_ref, st2_ref, sem, h_ref, acc_ref, *, layer, which):
    @pl.when(pl.program_id(0) == 0)
    def _():
        stage = _weight_stager(w13_hbm, w2_hbm, w13_ref, w2_ref, st13_ref, st2_ref, sem, layer, which)
        for j in range(D_FF // FF_CHUNK):
            stage(j)

    rows_per = hf_ref.shape[0] // FFN_PARTS
    x2 = []
    for p in range(FFN_PARTS):
        rs = slice(p * rows_per, (p + 1) * rows_per)
        h = hf_ref[rs, :].astype(F32) + hb_ref[rs, :].astype(F32)
        gate = _sigmoid(op_ref[rs, :].astype(F32))
        parts = []
        for hd in range(M_HEADS):
            sl = slice(hd * M_DH, (hd + 1) * M_DH)
            seg = h[:, sl]
            mu = jnp.mean(seg, axis=-1, keepdims=True)
            sc = seg - mu
            var = jnp.mean(sc * sc, axis=-1, keepdims=True)
            parts.append((sc * lax.rsqrt(var + LN_EPS) * mg_ref[:, sl] * gate[:, sl]).astype(BF16))
        mixed = jnp.concatenate(parts + [a_ref[rs, :]], axis=1)
        y = jnp.dot(mixed, wo_ref[...], preferred_element_type=F32)
        x2.append(_layer_norm(ALPHA * x_ref[rs, :] + mod_ref[0, 5:6, :] * y, g1_ref[...], b1_ref[...]))
    for p in range(FFN_PARTS):
        rs = slice(p * rows_per, (p + 1) * rows_per)
        o_ref[rs, :] = _ffn_body(x2[p], mod_ref, 6, w13_ref, w2_ref, g2_ref, b2_ref, h_ref.at[p], acc_ref.at[p])


def _mix_ffn_call(hf, hb, o_pre, a_l, x1, mod, mod_index, mh_g, w_out, g1, b1, ffn_w13, ffn_w2, layer, which,
                  g2, b2, tm):
    r = x1.shape[0]
    row = lambda w: pl.BlockSpec((tm, w), lambda i: (i, 0))
    vec = lambda a: a.reshape(1, a.shape[-1])
    return pl.pallas_call(
        functools.partial(_mix_ffn_kernel, layer=layer, which=which),
        grid=(r // tm,),
        in_specs=[row(M_WIDTH), row(M_WIDTH), row(M_WIDTH), row(A_WIDTH), row(D_MODEL),
                  pl.BlockSpec((1, N_MOD, D_MODEL), lambda i: (mod_index(i), 0, 0)),
                  _resident((1, M_WIDTH)),
                  _resident((D_MODEL, D_MODEL), (0, w_out.shape[1] // D_MODEL - 1)),
                  _resident((1, D_MODEL)),
                  _resident((1, D_MODEL)),
                  pl.BlockSpec(memory_space=pl.ANY),
                  pl.BlockSpec(memory_space=pl.ANY),
                  _resident((1, D_MODEL)),
                  _resident((1, D_MODEL))],
        out_specs=row(D_MODEL),
        out_shape=jax.ShapeDtypeStruct((r, D_MODEL), F32),
        scratch_shapes=_ffn_scratch((FFN_PARTS, tm // FFN_PARTS)),
        compiler_params=_params(1),
        name="mix_ffn",
    )(hf, hb, o_pre, a_l, x1, mod, vec(mh_g), w_out, vec(g1), vec(b1), ffn_w13, ffn_w2, vec(g2), vec(b2))


def _rope_tables(seq_len):
    t = np.arange(seq_len)
    row = (t // GRID_W).astype(np.float64)
    col = (t % GRID_W).astype(np.float64)
    n_freq = A_DH // 4
    inv = ROPE_BASE ** (-np.arange(n_freq, dtype=np.float64) / n_freq)
    ang = np.concatenate([row[:, None] * inv, col[:, None] * inv], -1)
    cos, sin = np.cos(ang), np.sin(ang)
    return (jnp.asarray(np.tile(cos, (1, 4)), F32),
            jnp.asarray(np.tile(np.concatenate([-sin, sin], -1), (1, 2)), F32))


def _spread_gates(a):
    lead = a.shape[:-1]
    a = a.reshape(lead + (4, M_HEADS))
    a = jnp.pad(a, [(0, 0)] * len(lead) + [(0, 0), (0, GATE_GROUP - M_HEADS)])
    a = a.reshape(lead + (4 * GATE_GROUP,))
    return jnp.pad(a, [(0, 0)] * len(lead) + [(0, LANES - 4 * GATE_GROUP)])


def kernel(x, c, ctx, c_ctx, w_ada, b_ada, ln_g, ln_b, ffn_w13, ffn_w2, w_in, b_gates, conv_w, conv_b,
           mh_norm_g, attn_sink, w_out):
    n_batch, seq_len, d = x.shape
    n_ctx = ctx.shape[1]
    l = 0
    tm = 512
    tm_ctx = n_ctx
    tm_ffn = FFN_PARTS * tm
    tiles_per_batch = seq_len // tm

    cc = jnp.concatenate([c, c_ctx[None, :], jnp.zeros((SUBLANES - n_batch - 1, d), F32)], axis=0)
    mod = _mod_call(cc, w_ada[l], b_ada[l])[:n_batch + 1].reshape(n_batch + 1, N_MOD, d)
    lat_mod = lambda i: i // tiles_per_batch
    ctx_mod = lambda i: n_batch

    bg = _spread_gates(b_gates[l]).reshape(1, LANES)
    cb = conv_b[l].reshape(1, 2 * M_WIDTH)
    cos, sin = _rope_tables(seq_len)

    x_lat = x.reshape(n_batch * seq_len, d)
    x_ctx = ctx.reshape(n_batch * n_ctx, d)

    x1, xc1 = _ffn_call(x_lat, x_ctx, mod, seq_len // tm_ffn, ffn_w13, ffn_w2, l, 0, ln_g[l, 0], ln_b[l, 0],
                        tm_ffn)

    assert OFF_QM == 0 and OFF_G == N_WA and N_WT % D_MODEL == 0
    wi, wo, x1 = lax.optimization_barrier((w_in[l], w_out[l], x1))
    wi = wi.astype(BF16)
    w_va = wi[:, OFF_VA:N_IN]
    wt = jnp.concatenate([wi[:, OFF_QA:OFF_VA], w_va, w_va[:, A_DH:], w_va[:, :A_DH],
                          _spread_gates(wi[:, OFF_G:OFF_QA]), wo.astype(BF16)], axis=1)

    (_, kt_c, v_c, _, grow_c, _, kat_c, va_c) = _inproj_call(
        xc1, mod, ctx_mod, wi, wt, bg, conv_w[l], cb, None, None, tm_ctx, n_ctx)
    (q_l, kt_l, v_l, o_pre, grow_l, qa_l, kat_l, va_l) = _inproj_call(
        x1, mod, lat_mod, wi, wt, bg, conv_w[l], cb, cos, sin, tm, seq_len)

    c0 = jnp.zeros((n_batch * N_STREAMS, M_DH, M_AUG), F32)
    m0 = jnp.zeros((n_batch * N_STREAMS, LANES), F32)
    c_st, m_st = _mlstm_call(None, kt_c, v_c.reshape(n_batch, n_ctx, M_WIDTH), grow_c, c0, m0,
                             n_batch, n_ctx, False)
    hf, hb, _, _ = _mlstm_call(q_l.reshape(n_batch, seq_len, M_WIDTH), kt_l,
                               v_l.reshape(n_batch, seq_len, M_WIDTH), grow_l, c_st, m_st, n_batch, seq_len, True)
    hf = hf.reshape(n_batch * seq_len, M_WIDTH)
    hb = hb.reshape(n_batch * seq_len, M_WIDTH)

    a_l = _attn_call(attn_sink[l], qa_l, kat_l, va_l, kat_c, va_c, n_batch, seq_len, n_ctx)

    out = _mix_ffn_call(hf, hb, o_pre, a_l, x1, mod, lambda i: i // (seq_len // tm_ffn), mh_norm_g[l],
                        wt, ln_g[l, 1], ln_b[l, 1], ffn_w13, ffn_w2, l, 1,
                        ln_g[l, 2], ln_b[l, 2], tm_ffn)
    return out.reshape(n_batch, seq_len, d)
```
